```python
import jax, jax.numpy as jnp
from jax import lax
import numpy as np

D_MODEL = 1024
BATCH = 4
SEQ = 8192
DEPTH = 1
DEC_BATCH = 32
DEC_SEQ = 1
PAST_LEN = 16384
PAGE_SIZE = 128

N_HEADS = 16
HEAD_DIM = 64
N_KV_HEADS = 4
ROT_DIM = HEAD_DIM // 4
ROPE_THETA = 500000.0
QK_EPS = 1e-6
N_IDX_HEADS = 8
IDX_DIM = 64
TOPK_MAX = 256
Q_BLOCK = 128
D_RNN = 1280
N_RNN_BLOCKS = 16
RNN_BLOCK_W = D_RNN // N_RNN_BLOCKS
CONV_W = 4
LRU_C = 8.0
PEER_HEADS = 8
PEER_KEY_DIM = 128
N_KEYS = 128
N_EXPERTS = N_KEYS * N_KEYS
PEER_TOPK = 16
PEER_BLOCK = 256
NORM_EPS = 1e-6

Q_W = N_HEADS * HEAD_DIM
KV_W = N_KV_HEADS * HEAD_DIM
QI_W = N_IDX_HEADS * IDX_DIM
D_IN = Q_W + 2 * KV_W + QI_W + IDX_DIM + N_IDX_HEADS + 2 * D_RNN + 2 * D_MODEL

kernel_name = "hybrid_dsa_rglru_peer_step"


def rms_norm(x, g, eps=NORM_EPS):
    xf = x.astype(jnp.float32)
    y = xf * lax.rsqrt(jnp.mean(xf * xf, axis=-1, keepdims=True) + eps)
    return (y * g.astype(jnp.float32)).astype(x.dtype)


def rope(x, pos):
    half = ROT_DIM // 2
    inv = ROPE_THETA ** (-jnp.arange(half, dtype=jnp.float32) / half)
    ang = pos.astype(jnp.float32)[:, None] * inv[None, :]
    c = jnp.cos(ang)[None, :, None, :]
    s = jnp.sin(ang)[None, :, None, :]
    xf = x.astype(jnp.float32)
    x1, x2, xp = xf[..., :half], xf[..., half:ROT_DIM], xf[..., ROT_DIM:]
    out = jnp.concatenate([x1 * c - x2 * s, x2 * c + x1 * s, xp], axis=-1)
    return out.astype(x.dtype)


def split_in(p):
    sizes = [Q_W, KV_W, KV_W, QI_W, IDX_DIM, N_IDX_HEADS, D_RNN, D_RNN, D_MODEL, D_MODEL]
    offs = []
    acc = 0
    for s in sizes[:-1]:
        acc += s
        offs.append(acc)
    return jnp.split(p, offs, axis=-1)


def front(h, pos, w_in, b_gates, q_norm_g, k_norm_g):
    B, T, _ = h.shape
    p = h @ w_in
    q, k, v, qi, ki, wi, xr, yr, ga, gr = split_in(p)
    q = rope(rms_norm(q.reshape(B, T, N_HEADS, HEAD_DIM), q_norm_g, QK_EPS), pos)
    k = rope(rms_norm(k.reshape(B, T, N_KV_HEADS, HEAD_DIM), k_norm_g, QK_EPS), pos)
    v = v.reshape(B, T, N_KV_HEADS, HEAD_DIM)
    qi = rope(qi.reshape(B, T, N_IDX_HEADS, IDX_DIM), pos)
    ki = rope(ki[:, :, None, :], pos)[:, :, 0, :]
    wi = wi * (N_IDX_HEADS ** -0.5)
    g = jax.nn.sigmoid(jnp.concatenate([ga, gr], axis=-1) + b_gates)
    g_a, g_r = g[..., :D_MODEL], g[..., D_MODEL:]
    return q, k, v, qi, ki, wi, xr, yr, g_a, g_r


def index_scores(qi, wi, ki):
    s = jnp.einsum('bthd,bsd->bths', qi.astype(jnp.float32), ki.astype(jnp.float32)) * (IDX_DIM ** -0.5)
    return jnp.einsum('bths,bth->bts', jax.nn.relu(s), wi.astype(jnp.float32))


def sparse_attend(q, k_sel, v_sel, valid):
    B, T = q.shape[:2]
    qg = q.reshape(B, T, N_KV_HEADS, N_HEADS // N_KV_HEADS, HEAD_DIM).astype(jnp.float32)
    s = jnp.einsum('btgrd,btkgd->btgrk', qg, k_sel.astype(jnp.float32)) * (HEAD_DIM ** -0.5)
    s = jnp.where(valid[:, :, None, None, :], s, -jnp.inf)
    p = jax.nn.softmax(s, axis=-1)
    o = jnp.einsum('btgrk,btkgd->btgrd', p, v_sel.astype(jnp.float32))
    return o.reshape(B, T, Q_W).astype(q.dtype)


def prompt_attention(q, k, v, qi, ki, wi):
    B, S = q.shape[:2]
    top_k = min(TOPK_MAX, S // 4)
    nb = S // Q_BLOCK
    key_pos = jnp.arange(S)

    def blocks(a):
        return jnp.moveaxis(a.reshape((B, nb, Q_BLOCK) + a.shape[2:]), 1, 0)

    qpos = jnp.arange(S).reshape(nb, Q_BLOCK)

    def one(args):
        qb, qib, wb, pb = args
        sc = index_scores(qib, wb, ki)
        sc = jnp.where(key_pos[None, None, :] <= pb[None, :, None], sc, -jnp.inf)
        _, idx = lax.top_k(sc, top_k)
        valid = idx <= pb[None, :, None]
        k_sel = jax.vmap(lambda kk, ii: kk[ii])(k, idx)
        v_sel = jax.vmap(lambda vv, ii: vv[ii])(v, idx)
        return sparse_attend(qb, k_sel, v_sel, valid)

    out = lax.map(one, (blocks(q), blocks(qi), blocks(wi), qpos))
    return jnp.moveaxis(out, 0, 1).reshape(B, S, Q_W)


def sample_attention(q, k_new, v_new, qi, ki_new, wi, cache_k, cache_v, cache_kidx, page_table):
    DB, T = q.shape[:2]
    L = PAST_LEN + T
    top_k = min(TOPK_MAX, L // 4)
    ki_past = cache_kidx[page_table].reshape(DB, PAST_LEN, IDX_DIM)
    ki_all = jnp.concatenate([ki_past, ki_new.astype(ki_past.dtype)], axis=1)
    sc = index_scores(qi, wi, ki_all)
    qpos = PAST_LEN + jnp.arange(T)
    key_pos = jnp.arange(L)
    sc = jnp.where(key_pos[None, None, :] <= qpos[None, :, None], sc, -jnp.inf)
    _, idx = lax.top_k(sc, top_k)
    valid = idx <= qpos[None, :, None]
    b_ix = jnp.arange(DB)[:, None, None]
    is_new = (idx >= PAST_LEN)[..., None, None]
    pidx = jnp.minimum(idx, PAST_LEN - 1)
    phys = page_table[b_ix, pidx // PAGE_SIZE]
    row = pidx % PAGE_SIZE
    nidx = jnp.clip(idx - PAST_LEN, 0, T - 1)
    k_sel = jnp.where(is_new, k_new[b_ix, nidx].astype(cache_k.dtype), cache_k[phys, row])
    v_sel = jnp.where(is_new, v_new[b_ix, nidx].astype(cache_v.dtype), cache_v[phys, row])
    return sparse_attend(q, k_sel, v_sel, valid)


def rglru_branch(xr, yr, conv_prev, h0, conv_w, conv_b, w_rgate, b_rgate, w_igate, b_igate, lru_lambda):
    B, T, _ = xr.shape
    xpad = jnp.concatenate([conv_prev.astype(xr.dtype), xr], axis=1)
    xc = conv_b + conv_w[0] * xpad[:, 0:T]
    for j in range(1, CONV_W):
        xc = xc + conv_w[j] * xpad[:, j:j + T]
    new_conv = xpad[:, T:]
    xb = xc.reshape(B, T, N_RNN_BLOCKS, RNN_BLOCK_W)
    r = jax.nn.sigmoid(jnp.einsum('btnd,nde->btne', xb, w_rgate).reshape(B, T, D_RNN) + b_rgate)
    i = jax.nn.sigmoid(jnp.einsum('btnd,nde->btne', xb, w_igate).reshape(B, T, D_RNN) + b_igate)
    log_a = -LRU_C * r.astype(jnp.float32) * jax.nn.softplus(-lru_lambda.astype(jnp.float32))
    a = jnp.exp(log_a)
    b = jnp.sqrt(-jnp.expm1(2.0 * log_a)) * (i * xc).astype(jnp.float32)
    b = b.at[:, 0].add(a[:, 0] * h0.astype(jnp.float32))

    def combine(left, right):
        a1, b1 = left
        a2, b2 = right
        return a1 * a2, a2 * b1 + b2

    _, h = lax.associative_scan(combine, (a, b), axis=1)
    out = (h * jax.nn.gelu(yr.astype(jnp.float32))).astype(xr.dtype)
    return out, new_conv, h[:, -1].astype(h0.dtype)


def peer(h, w_peer_q, peer_k1, peer_k2, peer_u, peer_v):
    B, T, D = h.shape
    n = B * T
    blk = min(PEER_BLOCK, n)
    nb = -(-n // blk)
    flat = jnp.pad(h.reshape(n, D), ((0, nb * blk - n), (0, 0)))
    half = PEER_KEY_DIM // 2

    def one(hb):
        qh = (hb @ w_peer_q).reshape(blk, PEER_HEADS, PEER_KEY_DIM).astype(jnp.float32)
        s1 = jnp.einsum('thd,nd->thn', qh[..., :half], peer_k1.astype(jnp.float32))
        s2 = jnp.einsum('thd,nd->thn', qh[..., half:], peer_k2.astype(jnp.float32))
        v1, i1 = lax.top_k(s1, PEER_TOPK)
        v2, i2 = lax.top_k(s2, PEER_TOPK)
        cand = (v1[..., :, None] + v2[..., None, :]).reshape(blk, PEER_HEADS, PEER_TOPK * PEER_TOPK)
        cidx = (i1[..., :, None] * N_KEYS + i2[..., None, :]).reshape(blk, PEER_HEADS, PEER_TOPK * PEER_TOPK)
        sv, si = lax.top_k(cand, PEER_TOPK)
        eidx = jnp.take_along_axis(cidx, si, axis=-1)
        g = jax.nn.softmax(sv, axis=-1)
        act = jax.nn.gelu(jnp.einsum('thkd,td->thk', peer_u[eidx].astype(jnp.float32), hb.astype(jnp.float32)))
        out = jnp.einsum('thk,thkd->td', g * act, peer_v[eidx].astype(jnp.float32))
        return out.astype(hb.dtype)

    out = lax.map(one, flat.reshape(nb, blk, D))
    return out.reshape(nb * blk, D)[:n].reshape(B, T, D)


def setup_inputs(seed: int = 0) -> dict:
    key = jax.random.key(seed)
    ks = jax.random.split(key, 32)
    n_pages = PAST_LEN // PAGE_SIZE
    n_pool = (DEC_BATCH * n_pages * 5) // 4
    nrm = jax.random.normal
    f32 = jnp.float32
    a0 = jax.random.uniform(ks[16], (D_RNN,), f32, 0.9, 0.999)
    return {
        "x_prompt": nrm(ks[0], (BATCH, SEQ, D_MODEL), f32),
        "x_sample": nrm(ks[1], (DEC_BATCH, DEC_SEQ, D_MODEL), f32),
        "cache_k": nrm(ks[2], (n_pool, PAGE_SIZE, N_KV_HEADS, HEAD_DIM), f32),
        "cache_v": nrm(ks[3], (n_pool, PAGE_SIZE, N_KV_HEADS, HEAD_DIM), f32),
        "cache_kidx": nrm(ks[4], (n_pool, PAGE_SIZE, IDX_DIM), f32),
        "page_table": jax.random.permutation(ks[5], n_pool)[:DEC_BATCH * n_pages].reshape(DEC_BATCH, n_pages).astype(jnp.int32),
        "state_conv": nrm(ks[6], (DEC_BATCH, CONV_W - 1, D_RNN), f32),
        "state_rglru": 0.5 * nrm(ks[7], (DEC_BATCH, D_RNN), f32),
        "norm1_g": 1.0 + 0.02 * nrm(ks[8], (D_MODEL,), f32),
        "w_in": nrm(ks[9], (D_MODEL, D_IN), f32) * D_MODEL ** -0.5,
        "b_gates": 0.02 * nrm(ks[10], (2 * D_MODEL,), f32),
        "q_norm_g": 1.0 + 0.02 * nrm(ks[11], (HEAD_DIM,), f32),
        "k_norm_g": 1.0 + 0.02 * nrm(ks[12], (HEAD_DIM,), f32),
        "conv_w": nrm(ks[13], (CONV_W, D_RNN), f32) * CONV_W ** -0.5,
        "conv_b": 0.02 * nrm(ks[14], (D_RNN,), f32),
        "w_rgate": nrm(ks[15], (N_RNN_BLOCKS, RNN_BLOCK_W, RNN_BLOCK_W), f32) * RNN_BLOCK_W ** -0.5,
        "b_rgate": 0.02 * nrm(ks[17], (D_RNN,), f32),
        "w_igate": nrm(ks[18], (N_RNN_BLOCKS, RNN_BLOCK_W, RNN_BLOCK_W), f32) * RNN_BLOCK_W ** -0.5,
        "b_igate": 0.02 * nrm(ks[19], (D_RNN,), f32),
        "lru_lambda": jnp.log(a0) - jnp.log1p(-a0),
        "w_o_attn": nrm(ks[20], (Q_W, D_MODEL), f32) * Q_W ** -0.5,
        "w_o_rnn": nrm(ks[21], (D_RNN, D_MODEL), f32) * D_RNN ** -0.5,
        "w_out": nrm(ks[22], (D_MODEL, D_MODEL), f32) * D_MODEL ** -0.5,
        "norm2_g": 1.0 + 0.02 * nrm(ks[23], (D_MODEL,), f32),
        "w_peer_q": nrm(ks[24], (D_MODEL, PEER_HEADS * PEER_KEY_DIM), f32) * D_MODEL ** -0.5,
        "peer_k1": nrm(ks[25], (N_KEYS, PEER_KEY_DIM // 2), f32) * (PEER_KEY_DIM // 2) ** -0.5,
        "peer_k2": nrm(ks[26], (N_KEYS, PEER_KEY_DIM // 2), f32) * (PEER_KEY_DIM // 2) ** -0.5,
        "peer_u": nrm(ks[27], (N_EXPERTS, D_MODEL), f32) * D_MODEL ** -0.5,
        "peer_v": nrm(ks[28], (N_EXPERTS, D_MODEL), f32) * PEER_HEADS ** -0.5,
    }


def reference(x_prompt, x_sample, cache_k, cache_v, cache_kidx, page_table, state_conv, state_rglru,
              norm1_g, w_in, b_gates, q_norm_g, k_norm_g, conv_w, conv_b, w_rgate, b_rgate,
              w_igate, b_igate, lru_lambda, w_o_attn, w_o_rnn, w_out, norm2_g,
              w_peer_q, peer_k1, peer_k2, peer_u, peer_v):
    rnn_w = (conv_w, conv_b, w_rgate, b_rgate, w_igate, b_igate, lru_lambda)
    peer_w = (w_peer_q, peer_k1, peer_k2, peer_u, peer_v)

    def back(x, attn_o, rnn_o, g_a, g_r):
        mixed = g_a * (attn_o @ w_o_attn) + g_r * (rnn_o @ w_o_rnn)
        x1 = x + mixed @ w_out
        return x1 + peer(rms_norm(x1, norm2_g), *peer_w)

    Bp, Sp, _ = x_prompt.shape
    hp = rms_norm(x_prompt, norm1_g)
    pos_p = jnp.arange(Sp, dtype=jnp.float32)
    q, k_p, v_p, qi, ki_p, wi, xr, yr, g_a, g_r = front(hp, pos_p, w_in, b_gates, q_norm_g, k_norm_g)
    attn_p = prompt_attention(q, k_p, v_p, qi, ki_p, wi)
    conv0 = jnp.zeros((Bp, CONV_W - 1, D_RNN), x_prompt.dtype)
    h0 = jnp.zeros((Bp, D_RNN), x_prompt.dtype)
    rnn_p, conv_p, h_p = rglru_branch(xr, yr, conv0, h0, *rnn_w)
    y_prompt = back(x_prompt, attn_p, rnn_p, g_a, g_r)

    Ts = x_sample.shape[1]
    hs = rms_norm(x_sample, norm1_g)
    pos_s = PAST_LEN + jnp.arange(Ts, dtype=jnp.float32)
    q, k_s, v_s, qi, ki_s, wi, xr, yr, g_a, g_r = front(hs, pos_s, w_in, b_gates, q_norm_g, k_norm_g)
    attn_s = sample_attention(q, k_s, v_s, qi, ki_s, wi, cache_k, cache_v, cache_kidx, page_table)
    rnn_s, conv_s, h_s = rglru_branch(xr, yr, state_conv, state_rglru, *rnn_w)
    y_sample = back(x_sample, attn_s, rnn_s, g_a, g_r)

    return (y_prompt, y_sample, k_p, v_p, ki_p, conv_p, h_p, k_s, v_s, ki_s, conv_s, h_s)
```

```python
import functools

import numpy as np
import jax
import jax.numpy as jnp
from jax import lax
from jax.experimental import pallas as pl
from jax.experimental.pallas import tpu as pltpu

f32 = jnp.float32
bf16 = jnp.bfloat16
i32 = jnp.int32

N_HEADS = 16
HEAD_DIM = 64
N_KV_HEADS = 4
KV_GROUP = N_HEADS // N_KV_HEADS
ROT_DIM = HEAD_DIM // 4
ROT_HALF = ROT_DIM // 2
ROPE_THETA = 500000.0
QK_EPS = 1e-6
N_IDX_HEADS = 8
IDX_DIM = 64
TOPK_MAX = 256
PAGE_SIZE = 128
CONV_W = 4
LRU_C = 8.0
PEER_HEADS = 8
PEER_KEY_DIM = 128
N_KEYS = 128
PEER_TOPK = 16
NORM_EPS = 1e-6

LANE = 128
SUBLANE = 8
VMEM_LIMIT_BYTES = 56 * 1024 * 1024

INT_MIN = -(2 ** 31)
NEG_BIG = -1e30

TOKEN_TILE = 256
Q_TILE = 128
KEY_CHUNK = 512
RNN_CHUNK = 256
PEER_TOKENS = 64
PAGES_PER_STEP = 8


def _params(*sem):
    return pltpu.CompilerParams(dimension_semantics=sem, vmem_limit_bytes=VMEM_LIMIT_BYTES)


def _resident(shape):
    nd = len(shape)
    return pl.BlockSpec(shape, lambda *_: (0,) * nd, pipeline_mode=pl.Buffered(1))


def _gelu(x):
    return 0.5 * x * (1.0 + jnp.tanh(np.sqrt(2.0 / np.pi).astype(np.float32) * (x + 0.044715 * (x * x * x))))


def _dot(a, b):
    return jnp.dot(a, b, preferred_element_type=f32)


def _dot_nt(a, b):
    return lax.dot_general(a, b, (((1,), (1,)), ((), ())), preferred_element_type=f32)


def _front_kernel(x_ref, g1_ref, wa_ref, wb_ref, bg_ref, qg_ref, kg_ref, c_ref, s1_ref, s2_ref, bd_ref,
                  q_ref, k_ref, v_ref, qi_ref, ki_ref, wi_ref, xr_ref, yr_ref, ga_ref, gr_ref,
                  *, d_model, d_rnn):
    x = x_ref[...]
    h = (x * lax.rsqrt(jnp.mean(x * x, axis=-1, keepdims=True) + NORM_EPS) * g1_ref[...]).astype(bf16)
    cos, sin_hi, sin_lo = c_ref[...], s1_ref[...], s2_ref[...]
    bd = bd_ref[...]

    def rope(t):
        return t * cos + pltpu.roll(t, LANE - ROT_HALF, 1) * sin_hi + pltpu.roll(t, ROT_HALF, 1) * sin_lo

    def head_norm(t, g):
        sq = t * t
        hi = sq.astype(bf16)
        lo = (sq - hi.astype(f32)).astype(bf16)
        ss = _dot(hi, bd) + _dot(lo, bd)
        return t * lax.rsqrt(ss * (1.0 / HEAD_DIM) + QK_EPS) * g

    q_w = N_HEADS * HEAD_DIM
    kv_w = N_KV_HEADS * HEAD_DIM
    qi_w = N_IDX_HEADS * IDX_DIM
    pa = _dot(h, wa_ref[...])
    off = 0
    for j in range(q_w // LANE):
        t = pa[:, off + j * LANE: off + (j + 1) * LANE]
        q_ref[:, j * LANE:(j + 1) * LANE] = (rope(head_norm(t, qg_ref[...])) * (HEAD_DIM ** -0.5)).astype(bf16)
    off += q_w
    for j in range(kv_w // LANE):
        t = pa[:, off + j * LANE: off + (j + 1) * LANE]
        k_ref[:, j * LANE:(j + 1) * LANE] = rope(head_norm(t, kg_ref[...]))
    off += kv_w
    v_ref[...] = pa[:, off:off + kv_w]
    off += kv_w
    for j in range(qi_w // LANE):
        t = pa[:, off + j * LANE: off + (j + 1) * LANE]
        qi_ref[:, j * LANE:(j + 1) * LANE] = rope(t).astype(bf16)
    off += qi_w
    ki_ref[...] = rope(pa[:, off:off + LANE])[:, :IDX_DIM]
    off += LANE
    wi_ref[...] = pa[:, off:off + LANE] * (N_IDX_HEADS ** -0.5)

    xr_ref[...] = _dot(h, wb_ref[:, 0:d_rnn])
    yr_ref[...] = _dot(h, wb_ref[:, d_rnn:2 * d_rnn])
    o = 2 * d_rnn
    ga_ref[...] = jax.nn.sigmoid(_dot(h, wb_ref[:, o:o + d_model]) + bg_ref[:, 0:d_model])
    gr_ref[...] = jax.nn.sigmoid(_dot(h, wb_ref[:, o + d_model:o + 2 * d_model]) + bg_ref[:, d_model:2 * d_model])


def _rope_tables(pos):
    inv = ROPE_THETA ** (-jnp.arange(ROT_HALF, dtype=f32) / ROT_HALF)
    ang = pos.astype(f32)[:, None] * inv[None, :]
    c, s = jnp.cos(ang), jnp.sin(ang)
    n = pos.shape[0]
    z = lambda w: jnp.zeros((n, w), f32)
    cos = jnp.concatenate([c, c, jnp.ones((n, HEAD_DIM - ROT_DIM), f32)], axis=1)
    s_hi = jnp.concatenate([-s, z(HEAD_DIM - ROT_HALF)], axis=1)
    s_lo = jnp.concatenate([z(ROT_HALF), s, z(HEAD_DIM - ROT_DIM)], axis=1)
    rep = LANE // HEAD_DIM
    return jnp.tile(cos, (1, rep)), jnp.tile(s_hi, (1, rep)), jnp.tile(s_lo, (1, rep))


def _front(x, pos, norm1_g, w_in, b_gates, q_norm_g, k_norm_g, d_rnn):
    n, d_model = x.shape
    q_w, kv_w, qi_w = N_HEADS * HEAD_DIM, N_KV_HEADS * HEAD_DIM, N_IDX_HEADS * IDX_DIM
    sizes = [q_w, kv_w, kv_w, qi_w, IDX_DIM, N_IDX_HEADS, d_rnn, d_rnn, d_model, d_model]
    offs = np.cumsum([0] + sizes)
    parts = [w_in[:, offs[i]:offs[i + 1]] for i in range(len(sizes))]
    padl = lambda w: jnp.pad(w, ((0, 0), (0, LANE - w.shape[1])))
    wa = jnp.concatenate(parts[:4] + [padl(parts[4]), padl(parts[5])], axis=1).astype(bf16)
    wb = jnp.concatenate(parts[6:], axis=1).astype(bf16)
    cos, s_hi, s_lo = _rope_tables(pos)
    rep = LANE // HEAD_DIM
    head_of_lane = np.arange(LANE) // HEAD_DIM
    bd = jnp.asarray(head_of_lane[:, None] == head_of_lane[None, :], bf16)
    tm = TOKEN_TILE
    row = lambda w: pl.BlockSpec((tm, w), lambda i: (i, 0))
    outs = [(q_w, bf16), (kv_w, f32), (kv_w, f32), (qi_w, bf16), (IDX_DIM, f32), (LANE, f32),
            (d_rnn, f32), (d_rnn, f32), (d_model, f32), (d_model, f32)]
    return pl.pallas_call(
        functools.partial(_front_kernel, d_model=d_model, d_rnn=d_rnn),
        grid=(n // tm,),
        in_specs=[row(d_model), _resident((1, d_model)), _resident(wa.shape), _resident(wb.shape),
                  _resident((1, 2 * d_model)), _resident((1, LANE)), _resident((1, LANE)),
                  row(LANE), row(LANE), row(LANE), _resident((LANE, LANE))],
        out_specs=[row(w) for w, _ in outs],
        out_shape=[jax.ShapeDtypeStruct((n, w), dt) for w, dt in outs],
        compiler_params=_params("parallel"),
        name="front",
    )(x, norm1_g.reshape(1, -1), wa, wb, b_gates.reshape(1, -1),
      jnp.tile(q_norm_g, rep).reshape(1, LANE), jnp.tile(k_norm_g, rep).reshape(1, LANE),
      cos, s_hi, s_lo, bd)


def _score_key(score):
    score = jnp.where(score == 0.0, 0.0, score)
    bits = pltpu.bitcast(score, i32)
    return bits ^ ((bits >> 31) & 0x7FFFFFFF)


def _kth_largest_key(count_ge, k, like):
    ans = jnp.where(count_ge(jnp.zeros_like(like)) >= k, 0, INT_MIN).astype(i32)

    def body(i, ans):
        cand = ans + lax.shift_left(jnp.int32(1), jnp.int32(30) - i)
        return jnp.where(count_ge(cand) >= k, cand, ans)

    ans = lax.fori_loop(0, 31, body, ans)
    return jnp.maximum(ans, INT_MIN + 1)


def _prompt_attn_kernel(q_ref, qi_ref, wi_ref, kit_ref, kt_ref, v_ref, tri_ref, o_ref, keys_ref, *, top_k):
    qb = pl.program_id(1)
    tq, ck = Q_TILE, KEY_CHUNK
    n_chunks = (qb * tq + tq + ck - 1) // ck
    qpos = qb * tq + lax.broadcasted_iota(i32, (tq, ck), 0)
    lane = lax.broadcasted_iota(i32, (tq, ck), 1)
    w = wi_ref[0, 0]

    def score_chunk(c, carry):
        kc = kit_ref[0, c]
        acc = jnp.zeros((tq, ck), f32)
        for h in range(N_IDX_HEADS):
            s = _dot(qi_ref[0, 0, h * tq:(h + 1) * tq, :], kc)
            acc = acc + jnp.maximum(s, 0.0) * w[:, h:h + 1]
        keys_ref[c] = jnp.where(c * ck + lane <= qpos, _score_key(acc), INT_MIN)
        return carry

    lax.fori_loop(0, n_chunks, score_chunk, 0)

    def count_ge(cand):
        def body(c, acc):
            m = (keys_ref[c] >= cand).astype(f32)
            for j in range(ck // LANE):
                acc = acc + m[:, j * LANE:(j + 1) * LANE]
            return acc
        acc = lax.fori_loop(0, n_chunks, body, jnp.zeros((tq, LANE), f32))
        return jnp.sum(acc, axis=1, keepdims=True)

    thr = _kth_largest_key(count_ge, float(top_k), jnp.zeros((tq, 1), i32))

    n_ge = count_ge(thr)

    @pl.when(jnp.max(n_ge) > top_k)
    def _():
        need = top_k - count_ge(thr + 1)

        def body(c, before):
            kc = keys_ref[c]
            eq = kc == thr
            eqb = eq.astype(bf16)
            rank = _dot(eqb, tri_ref[...]) + before
            keys_ref[c] = jnp.where(eq & (rank >= need), INT_MIN, kc)
            return before + jnp.sum(eqb.astype(f32), axis=1, keepdims=True)

        lax.fori_loop(0, n_chunks, body, jnp.zeros((tq, 1), f32))

    rows = KV_GROUP * tq
    for g in range(N_KV_HEADS):
        qg = q_ref[0, 0, g]

        def attend(c, carry):
            m, l, acc = carry
            s = _dot(qg, kt_ref[0, g, c])
            sel = keys_ref[c] >= thr
            s = jnp.where(sel[None], s.reshape(KV_GROUP, tq, ck), NEG_BIG).reshape(rows, ck)
            m_new = jnp.maximum(m, jnp.max(s, axis=1, keepdims=True))
            alpha = jnp.exp(m - m_new)
            p = jnp.exp(s - m_new)
            l = alpha * l + jnp.sum(p, axis=1, keepdims=True)
            acc = alpha * acc + _dot(p.astype(bf16), v_ref[0, g, c])
            return m_new, l, acc

        init = (jnp.full((rows, 1), NEG_BIG, f32), jnp.zeros((rows, 1), f32), jnp.zeros((rows, HEAD_DIM), f32))
        _, l, acc = lax.fori_loop(0, n_chunks, attend, init)
        o_ref[0, 0, g] = (acc / l).astype(o_ref.dtype)


def _prompt_attention(q, k, v, qi, ki, wi):
    b, s, _ = q.shape
    top_k = min(TOPK_MAX, s // 4)
    tq, ck = Q_TILE, KEY_CHUNK
    nq, nc = s // tq, s // ck
    rows = KV_GROUP * tq
    qg = q.reshape(b, nq, tq, N_KV_HEADS, KV_GROUP, HEAD_DIM).transpose(0, 1, 3, 4, 2, 5).reshape(b, nq, N_KV_HEADS, rows, HEAD_DIM)
    qig = qi.reshape(b, nq, tq, N_IDX_HEADS, IDX_DIM).transpose(0, 1, 3, 2, 4).reshape(b, nq, N_IDX_HEADS * tq, IDX_DIM)
    wig = (wi * (IDX_DIM ** -0.5)).reshape(b, nq, tq, N_IDX_HEADS)
    kit = ki.astype(bf16).reshape(b, nc, ck, IDX_DIM).transpose(0, 1, 3, 2)
    kt = k.astype(bf16).reshape(b, nc, ck, N_KV_HEADS, HEAD_DIM).transpose(0, 3, 1, 4, 2)
    vg = v.astype(bf16).reshape(b, nc, ck, N_KV_HEADS, HEAD_DIM).transpose(0, 3, 1, 2, 4)
    tri = jnp.asarray(np.arange(ck)[:, None] < np.arange(ck)[None, :], bf16)
    out = pl.pallas_call(
        functools.partial(_prompt_attn_kernel, top_k=top_k),
        grid=(b, nq),
        in_specs=[
            pl.BlockSpec((1, 1, N_KV_HEADS, rows, HEAD_DIM), lambda i, j: (i, j, 0, 0, 0)),
            pl.BlockSpec((1, 1, N_IDX_HEADS * tq, IDX_DIM), lambda i, j: (i, j, 0, 0)),
            pl.BlockSpec((1, 1, tq, N_IDX_HEADS), lambda i, j: (i, j, 0, 0)),
            pl.BlockSpec((1, nc, IDX_DIM, ck), lambda i, j: (i, 0, 0, 0)),
            pl.BlockSpec((1, N_KV_HEADS, nc, HEAD_DIM, ck), lambda i, j: (i, 0, 0, 0, 0)),
            pl.BlockSpec((1, N_KV_HEADS, nc, ck, HEAD_DIM), lambda i, j: (i, 0, 0, 0, 0)),
            _resident((ck, ck)),
        ],
        out_specs=pl.BlockSpec((1, 1, N_KV_HEADS, rows, HEAD_DIM), lambda i, j: (i, j, 0, 0, 0)),
        out_shape=jax.ShapeDtypeStruct((b, nq, N_KV_HEADS, rows, HEAD_DIM), bf16),
        scratch_shapes=[pltpu.VMEM((nc, tq, ck), i32)],
        compiler_params=_params("parallel", "arbitrary"),
        name="prompt_attention",
    )(qg, qig, wig, kit, kt, vg, tri)
    return out.reshape(b, nq, N_KV_HEADS, KV_GROUP, tq, HEAD_DIM).transpose(0, 1, 4, 2, 3, 5).reshape(b, s, N_HEADS * HEAD_DIM)


def _sample_select_kernel(pt_ref, qi_ref, wi_ref, kin_ref, tri_ref, lt_ref, *rest, top_k, n_pages):
    page_refs, (sel_ref, seln_ref, keys_ref) = rest[:PAGES_PER_STEP], rest[PAGES_PER_STEP:]
    j = pl.program_id(1)
    qi = qi_ref[0]
    w = wi_ref[0]

    for i in range(PAGES_PER_STEP):
        s = _dot_nt(qi, page_refs[i][0].astype(bf16))
        score = jnp.sum(jnp.maximum(s, 0.0) * w, axis=0, keepdims=True)
        keys_ref[pl.ds(j * PAGES_PER_STEP + i, 1), :] = _score_key(score)

    @pl.when(j == pl.num_programs(1) - 1)
    def _():
        kn = kin_ref[0].astype(bf16).astype(f32)
        s_new = jnp.sum(qi.astype(f32) * kn, axis=1, keepdims=True)
        key_new = _score_key(jnp.sum(jnp.maximum(s_new, 0.0) * w, axis=0, keepdims=True))
        keys = keys_ref[...]

        def count_ge(cand):
            c = jnp.sum((keys >= cand).astype(f32), axis=1, keepdims=True)
            return jnp.sum(c, axis=0, keepdims=True) + (key_new >= cand).astype(f32)

        thr = _kth_largest_key(count_ge, float(top_k), jnp.zeros((1, 1), i32))
        need = top_k - (count_ge(thr + 1))
        eq = keys == thr
        eqb = eq.astype(bf16)
        in_row = _dot(eqb, tri_ref[...])
        per_page = jnp.sum(eqb.astype(f32), axis=1, keepdims=True)
        before = _dot(lt_ref[...], per_page.astype(bf16) * jnp.ones((1, LANE), bf16))[:, 0:1]
        rank = in_row + before
        sel = (keys > thr) | (eq & (rank < need))
        sel_ref[0] = sel.astype(f32)
        n_eq = jnp.sum(per_page, axis=0, keepdims=True)
        sel_new = (key_new > thr) | ((key_new == thr) & (n_eq < need))
        seln_ref[0] = jnp.broadcast_to(sel_new.astype(f32), (1, LANE))


def _sample_attend_kernel(pt_ref, q_ref, sel_ref, seln_ref, kn_ref, vn_ref, gm_ref, *rest, n_pages):
    k_refs = rest[:PAGES_PER_STEP]
    v_refs = rest[PAGES_PER_STEP:2 * PAGES_PER_STEP]
    o_ref, m_ref, l_ref, acc_ref = rest[2 * PAGES_PER_STEP:]
    j = pl.program_id(1)
    q = q_ref[0]

    @pl.when(j == 0)
    def _():
        m_ref[...] = jnp.full(m_ref.shape, NEG_BIG, f32)
        l_ref[...] = jnp.zeros(l_ref.shape, f32)
        acc_ref[...] = jnp.zeros(acc_ref.shape, f32)

    def update(s, vals):
        m = m_ref[...]
        m_new = jnp.maximum(m, jnp.max(s, axis=1, keepdims=True))
        alpha = jnp.exp(m - m_new)
        p = jnp.exp(s - m_new)
        l_ref[...] = alpha * l_ref[...] + jnp.sum(p, axis=1, keepdims=True)
        acc_ref[...] = alpha * acc_ref[...] + vals(p)
        m_ref[...] = m_new

    for i in range(PAGES_PER_STEP):
        sel = sel_ref[0, pl.ds(j * PAGES_PER_STEP + i, 1), :] > 0.5
        s = jnp.where(sel, _dot_nt(q, k_refs[i][0].astype(bf16)), NEG_BIG)
        update(s, lambda p, i=i: _dot(p.astype(bf16), v_refs[i][0].astype(bf16)))

    @pl.when(j == pl.num_programs(1) - 1)
    def _():
        kn = kn_ref[0].astype(bf16).astype(f32)
        vn = vn_ref[0].astype(bf16).astype(f32)
        s = jnp.sum(q.astype(f32) * kn, axis=1, keepdims=True)
        s = jnp.where(seln_ref[0][:, 0:1] > 0.5, s, NEG_BIG)
        update(s, lambda p: p.astype(bf16).astype(f32) * vn)
        full = acc_ref[...] / l_ref[...] * gm_ref[...]
        out = full[:, 0:HEAD_DIM]
        for g in range(1, N_KV_HEADS):
            out = out + full[:, g * HEAD_DIM:(g + 1) * HEAD_DIM]
        o_ref[0] = out.astype(o_ref.dtype)


def _sample_attention(q, k_new, v_new, qi, ki_new, wi, cache_k, cache_v, cache_kidx, page_table):
    db = q.shape[0]
    n_pages = page_table.shape[1]
    past = n_pages * PAGE_SIZE
    top_k = min(TOPK_MAX, (past + 1) // 4)
    pps = PAGES_PER_STEP
    steps = n_pages // pps
    kv_w = N_KV_HEADS * HEAD_DIM
    n_pool = cache_k.shape[0]
    ck2 = cache_k.reshape(n_pool, PAGE_SIZE, kv_w)
    cv2 = cache_v.reshape(n_pool, PAGE_SIZE, kv_w)

    def page_spec(width, i):
        return pl.BlockSpec((1, PAGE_SIZE, width), lambda b, j, pt, i=i: (pt[b, j * pps + i], 0, 0))

    per_seq = lambda *shape: pl.BlockSpec((1,) + shape, lambda b, j, pt: (b,) + (0,) * len(shape))
    const = lambda *shape: pl.BlockSpec(shape, lambda b, j, pt: (0,) * len(shape))
    tri = jnp.asarray(np.arange(PAGE_SIZE)[:, None] < np.arange(PAGE_SIZE)[None, :], bf16)
    lower = jnp.asarray(np.arange(n_pages)[:, None] > np.arange(n_pages)[None, :], bf16)

    sel, sel_new = pl.pallas_call(
        functools.partial(_sample_select_kernel, top_k=top_k, n_pages=n_pages),
        grid_spec=pltpu.PrefetchScalarGridSpec(
            num_scalar_prefetch=1,
            grid=(db, steps),
            in_specs=[per_seq(N_IDX_HEADS, IDX_DIM), per_seq(N_IDX_HEADS, 1), per_seq(1, IDX_DIM),
                      const(PAGE_SIZE, PAGE_SIZE), const(n_pages, n_pages)]
                     + [page_spec(IDX_DIM, i) for i in range(pps)],
            out_specs=[per_seq(n_pages, PAGE_SIZE), per_seq(1, LANE)],
            scratch_shapes=[pltpu.VMEM((n_pages, PAGE_SIZE), i32)],
        ),
        out_shape=[jax.ShapeDtypeStruct((db, n_pages, PAGE_SIZE), f32), jax.ShapeDtypeStruct((db, 1, LANE), f32)],
        compiler_params=_params("parallel", "arbitrary"),
        name="sample_select",
    )(page_table, qi.reshape(db, N_IDX_HEADS, IDX_DIM), (wi * (IDX_DIM ** -0.5)).reshape(db, N_IDX_HEADS, 1),
      ki_new.reshape(db, 1, IDX_DIM), tri, lower, *([cache_kidx] * pps))

    group_mask = np.repeat(np.arange(N_HEADS)[:, None] // KV_GROUP == np.arange(N_KV_HEADS)[None, :], HEAD_DIM, axis=1)
    q_bd = jnp.where(group_mask[None], jnp.tile(q.reshape(db, N_HEADS, HEAD_DIM), (1, 1, N_KV_HEADS)), 0).astype(bf16)
    out = pl.pallas_call(
        functools.partial(_sample_attend_kernel, n_pages=n_pages),
        grid_spec=pltpu.PrefetchScalarGridSpec(
            num_scalar_prefetch=1,
            grid=(db, steps),
            in_specs=[per_seq(N_HEADS, kv_w), per_seq(n_pages, PAGE_SIZE), per_seq(1, LANE),
                      per_seq(1, kv_w), per_seq(1, kv_w), const(N_HEADS, kv_w)]
                     + [page_spec(kv_w, i) for i in range(pps)] * 2,
            out_specs=per_seq(N_HEADS, HEAD_DIM),
            scratch_shapes=[pltpu.VMEM((N_HEADS, 1), f32), pltpu.VMEM((N_HEADS, 1), f32), pltpu.VMEM((N_HEADS, kv_w), f32)],
        ),
        out_shape=jax.ShapeDtypeStruct((db, N_HEADS, HEAD_DIM), bf16),
        compiler_params=_params("parallel", "arbitrary"),
        name="sample_attend",
    )(page_table, q_bd, sel, sel_new, k_new.reshape(db, 1, kv_w), v_new.reshape(db, 1, kv_w),
      jnp.asarray(group_mask, f32), *([ck2] * pps), *([cv2] * pps))
    return out.reshape(db, N_HEADS * HEAD_DIM)


def _rglru_coeffs(xc, wr_ref, br_ref, wi_ref, bi_ref, lam_ref):
    xb = xc.astype(bf16)
    r = jax.nn.sigmoid(_dot(xb, wr_ref[...]) + br_ref[...])
    i = jax.nn.sigmoid(_dot(xb, wi_ref[...]) + bi_ref[...])
    z = -lam_ref[...]
    softplus = jnp.maximum(z, 0.0) + jnp.log1p(jnp.exp(-jnp.abs(z)))
    log_a = -LRU_C * r * softplus
    a = jnp.exp(log_a)
    th = jnp.tanh(log_a)
    b = jnp.sqrt(-2.0 * th / (1.0 - th)) * (i * xc)
    return a, b


def _rglru_seq_kernel(xr_ref, yr_ref, cw_ref, cb_ref, wr_ref, br_ref, wi_ref, bi_ref, lam_ref,
                      o_ref, hl_ref, xprev_ref, h_ref, a_ref, b_ref):
    t = pl.program_id(1)
    tc, d = xr_ref.shape[1], xr_ref.shape[2]

    @pl.when(t == 0)
    def _():
        xprev_ref[...] = jnp.zeros(xprev_ref.shape, f32)
        h_ref[...] = jnp.zeros(h_ref.shape, f32)

    xr = xr_ref[0]
    xcat = jnp.concatenate([xprev_ref[...], xr], axis=0)
    xc = cb_ref[...] + cw_ref[0:1, :] * xcat[SUBLANE - 3:SUBLANE - 3 + tc]
    for j in range(1, CONV_W):
        xc = xc + cw_ref[j:j + 1, :] * xcat[SUBLANE - 3 + j:SUBLANE - 3 + j + tc]
    xprev_ref[...] = xr[tc - SUBLANE:, :]
    a, b = _rglru_coeffs(xc, wr_ref, br_ref, wi_ref, bi_ref, lam_ref)
    a_ref[...] = a
    b_ref[...] = b
    row = lax.broadcasted_iota(i32, (SUBLANE, d), 0)

    def slab(s, h_prev):
        r0 = pl.multiple_of(s * SUBLANE, SUBLANE)
        a8 = a_ref[pl.ds(r0, SUBLANE), :]
        b8 = b_ref[pl.ds(r0, SUBLANE), :]
        for sh in (1, 2, 4):
            keep = row >= sh
            b8 = jnp.where(keep, a8 * pltpu.roll(b8, sh, 0) + b8, b8)
            a8 = jnp.where(keep, a8 * pltpu.roll(a8, sh, 0), a8)
        h8 = a8 * h_prev + b8
        o_ref[0, pl.ds(r0, SUBLANE), :] = (h8 * _gelu(yr_ref[0, pl.ds(r0, SUBLANE), :])).astype(o_ref.dtype)
        return h8[SUBLANE - 1:SUBLANE, :]

    h_last = lax.fori_loop(0, tc // SUBLANE, slab, h_ref[...])
    h_ref[...] = h_last
    hl_ref[0] = h_last


def _block_diag(w):
    n, a, b = w.shape
    eye = jnp.eye(n, dtype=w.dtype)
    return (eye[:, None, :, None] * w[:, :, None, :]).reshape(n * a, n * b).astype(bf16)


def _rglru_prompt(xr, yr, conv_w, conv_b, w_rgate, b_rgate, w_igate, b_igate, lru_lambda):
    b, t, d = xr.shape
    tc = RNN_CHUNK
    vec = lambda v: v.reshape(1, d)
    seq = pl.BlockSpec((1, tc, d), lambda i, j: (i, j, 0))
    out, h_last = pl.pallas_call(
        _rglru_seq_kernel,
        grid=(b, t // tc),
        in_specs=[seq, seq, _resident((CONV_W, d)), _resident((1, d)), _resident((d, d)), _resident((1, d)),
                  _resident((d, d)), _resident((1, d)), _resident((1, d))],
        out_specs=[seq, pl.BlockSpec((1, 1, d), lambda i, j: (i, 0, 0))],
        out_shape=[jax.ShapeDtypeStruct((b, t, d), bf16), jax.ShapeDtypeStruct((b, 1, d), f32)],
        scratch_shapes=[pltpu.VMEM((SUBLANE, d), f32), pltpu.VMEM((1, d), f32),
                        pltpu.VMEM((tc, d), f32), pltpu.VMEM((tc, d), f32)],
        compiler_params=_params("parallel", "arbitrary"),
        name="rglru_prompt",
    )(xr, yr, conv_w, vec(conv_b), _block_diag(w_rgate), vec(b_rgate), _block_diag(w_igate), vec(b_igate),
      vec(lru_lambda))
    return out, h_last.reshape(b, d)


def _rglru_step_kernel(xr_ref, yr_ref, c0_ref, c1_ref, c2_ref, h0_ref, cw_ref, cb_ref, wr_ref, br_ref, wi_ref,
                       bi_ref, lam_ref, o_ref, h_ref):
    xc = cb_ref[...] + cw_ref[0:1, :] * c0_ref[...]
    xc = xc + cw_ref[1:2, :] * c1_ref[...]
    xc = xc + cw_ref[2:3, :] * c2_ref[...]
    xc = xc + cw_ref[3:4, :] * xr_ref[...]
    a, b = _rglru_coeffs(xc, wr_ref, br_ref, wi_ref, bi_ref, lam_ref)
    h = a * h0_ref[...] + b
    h_ref[...] = h
    o_ref[...] = (h * _gelu(yr_ref[...])).astype(o_ref.dtype)


def _rglru_sample(xr, yr, state_conv, h0, conv_w, conv_b, w_rgate, b_rgate, w_igate, b_igate, lru_lambda):
    db, d = xr.shape
    vec = lambda v: v.reshape(1, d)
    return pl.pallas_call(
        _rglru_step_kernel,
        out_shape=[jax.ShapeDtypeStruct((db, d), bf16), jax.ShapeDtypeStruct((db, d), f32)],
        compiler_params=pltpu.CompilerParams(vmem_limit_bytes=VMEM_LIMIT_BYTES),
        name="rglru_sample",
    )(xr, yr, state_conv[:, 0], state_conv[:, 1], state_conv[:, 2], h0, conv_w, vec(conv_b),
      _block_diag(w_rgate), vec(b_rgate), _block_diag(w_igate), vec(b_igate), vec(lru_lambda))


def _top_rows(x, k, payload=None):
    n = x.shape[0]
    row = lax.broadcasted_iota(i32, x.shape, 0)
    vals, picked = [], []
    for _ in range(k):
        m = jnp.max(x, axis=0, keepdims=True)
        r = jnp.min(jnp.where(x == m, row, n), axis=0, keepdims=True)
        hit = row == r
        vals.append(m)
        picked.append(r if payload is None else jnp.sum(jnp.where(hit, payload, 0), axis=0, keepdims=True))
        x = jnp.where(hit, -jnp.inf, x)
    return jnp.concatenate(vals, axis=0), jnp.concatenate(picked, axis=0)


_PAIR_COUNTS = [PEER_TOPK // (r1 + 1) for r1 in range(PEER_TOPK)]


def _back_kernel(x_ref, ao_ref, ro_ref, ga_ref, gr_ref, woa_ref, wor_ref, wout_ref, g2_ref, wqt_ref, k1_ref, k2_ref,
                 x1_ref, hn_ref, eidx_ref, gate_ref):
    mixed = ga_ref[...] * _dot(ao_ref[...], woa_ref[...]) + gr_ref[...] * _dot(ro_ref[...], wor_ref[...])
    x1 = x_ref[...] + _dot(mixed.astype(bf16), wout_ref[...])
    x1_ref[...] = x1
    hn = (x1 * lax.rsqrt(jnp.mean(x1 * x1, axis=-1, keepdims=True) + NORM_EPS) * g2_ref[...]).astype(bf16)
    hn_ref[...] = hn.astype(f32)
    qt = _dot_nt(wqt_ref[...], hn).astype(bf16)
    half = PEER_KEY_DIM // 2
    for h in range(PEER_HEADS):
        base = h * PEER_KEY_DIM
        v1, i1 = _top_rows(_dot(k1_ref[...], qt[base:base + half, :]), PEER_TOPK)
        v2, i2 = _top_rows(_dot(k2_ref[...], qt[base + half:base + PEER_KEY_DIM, :]), PEER_TOPK)
        cand, cid = [], []
        for r1, n2 in enumerate(_PAIR_COUNTS):
            cand.append(v1[r1:r1 + 1, :] + v2[0:n2, :])
            cid.append(i1[r1:r1 + 1, :] * N_KEYS + i2[0:n2, :])
        pad = -sum(_PAIR_COUNTS) % SUBLANE
        tm = v1.shape[1]
        cand.append(jnp.full((pad, tm), -jnp.inf, f32))
        cid.append(jnp.zeros((pad, tm), i32))
        sv, eidx = _top_rows(jnp.concatenate(cand, axis=0), PEER_TOPK, jnp.concatenate(cid, axis=0))
        p = jnp.exp(sv - sv[0:1, :])
        eidx_ref[h * PEER_TOPK:(h + 1) * PEER_TOPK, :] = eidx
        gate_ref[h * PEER_TOPK:(h + 1) * PEER_TOPK, :] = p / jnp.sum(p, axis=0, keepdims=True)


def _back(x, attn_o, rnn_o, g_a, g_r, w_o_attn, w_o_rnn, w_out, norm2_g, w_peer_q, peer_k1, peer_k2):
    n, d = x.shape
    tm = TOKEN_TILE
    row = lambda w: pl.BlockSpec((tm, w), lambda i: (i, 0))
    col = pl.BlockSpec((PEER_HEADS * PEER_TOPK, tm), lambda i: (0, i))
    ws = [w_o_attn.astype(bf16), w_o_rnn.astype(bf16), w_out.astype(bf16), norm2_g.reshape(1, d),
          w_peer_q.T.astype(bf16), peer_k1.astype(bf16), peer_k2.astype(bf16)]
    return pl.pallas_call(
        _back_kernel,
        grid=(n // tm,),
        in_specs=[row(d), row(attn_o.shape[1]), row(rnn_o.shape[1]), row(d), row(d)] + [_resident(w.shape) for w in ws],
        out_specs=[row(d), row(d), col, col],
        out_shape=[jax.ShapeDtypeStruct((n, d), f32), jax.ShapeDtypeStruct((n, d), f32),
                   jax.ShapeDtypeStruct((PEER_HEADS * PEER_TOPK, n), i32),
                   jax.ShapeDtypeStruct((PEER_HEADS * PEER_TOPK, n), f32)],
        compiler_params=_params("parallel"),
        name="back",
    )(x, attn_o, rnn_o, g_a, g_r, *ws)


def _pack_rows(table):
    e, d = table.shape
    bits = lax.bitcast_convert_type(table.astype(bf16), jnp.uint16).astype(jnp.uint32).reshape(e // 2, 2, d // LANE, LANE)
    return lax.bitcast_convert_type(bits[:, 0] | (bits[:, 1] << 16), i32)


def _expert_row(table_ref, e):
    tile = table_ref[e >> 1]
    shift = (1 - (e & 1)) * 16
    return pltpu.bitcast((tile << shift) & jnp.int32(-65536), f32)


def _sublane_sums(tiles):
    row = lax.broadcasted_iota(i32, (SUBLANE, LANE), 0)
    width = SUBLANE
    tiles = [tiles[int(f"{p:03b}"[::-1], 2)] for p in range(SUBLANE)]
    while len(tiles) > 1:
        width //= 2
        low = (row & width) == 0
        nxt = []
        for a, b in zip(tiles[0::2], tiles[1::2]):
            nxt.append(jnp.where(low, a, pltpu.roll(b, width, 0)) + jnp.where(low, pltpu.roll(a, SUBLANE - width, 0), b))
        tiles = nxt
    return tiles[0]


def _peer_act_kernel(idx_ref, hn_ref, gate_ref, u_ref, coef_ref):
    n_slots = PEER_HEADS * PEER_TOPK

    def token(t, carry):
        ht = hn_ref[t]
        groups = []
        for g in range(n_slots // SUBLANE):
            prods = [_expert_row(u_ref, idx_ref[t * n_slots + g * SUBLANE + j]) * ht for j in range(SUBLANE)]
            groups.append(_sublane_sums(prods))
        m = jnp.concatenate(groups, axis=0)
        act = jnp.sum(m.T, axis=0, keepdims=True)
        coef_ref[pl.ds(t, 1), :] = gate_ref[pl.ds(t, 1), :] * _gelu(act)
        return carry

    lax.fori_loop(0, hn_ref.shape[0], token, 0)


def _peer_out_kernel(idx_ref, coef_ref, x1_ref, v_ref, y_ref):
    n_slots = PEER_HEADS * PEER_TOPK

    def token(t, carry):
        acc = jnp.zeros((SUBLANE, LANE), f32)
        for s in range(n_slots):
            acc = acc + _expert_row(v_ref, idx_ref[t * n_slots + s]) * coef_ref[t * n_slots + s]
        y_ref[t] = x1_ref[t] + acc
        return carry

    lax.fori_loop(0, x1_ref.shape[0], token, 0)


def _peer(hn, x1, eidx_t, gate_t, peer_u, peer_v):
    n, d = hn.shape
    n_slots = PEER_HEADS * PEER_TOPK
    tt = PEER_TOKENS
    rows = d // LANE
    idx = eidx_t.T.reshape(n * n_slots)
    gate = gate_t.T
    smem = pl.BlockSpec((tt * n_slots,), lambda i: (i,), memory_space=pltpu.SMEM)
    tiles = pl.BlockSpec((tt, rows, LANE), lambda i: (i, 0, 0))
    slots = pl.BlockSpec((tt, n_slots), lambda i: (i, 0))
    table = _resident((peer_u.shape[0] // 2, rows, LANE))
    coef = pl.pallas_call(
        _peer_act_kernel,
        grid=(n // tt,),
        in_specs=[smem, tiles, slots, table],
        out_specs=slots,
        out_shape=jax.ShapeDtypeStruct((n, n_slots), f32),
        compiler_params=_params("parallel"),
        name="peer_act",
    )(idx, hn.reshape(n, rows, LANE), gate, _pack_rows(peer_u))
    y = pl.pallas_call(
        _peer_out_kernel,
        grid=(n // tt,),
        in_specs=[smem, smem, tiles, table],
        out_specs=tiles,
        out_shape=jax.ShapeDtypeStruct((n, rows, LANE), f32),
        compiler_params=_params("parallel"),
        name="peer_out",
    )(idx, coef.reshape(n * n_slots), x1.reshape(n, rows, LANE), _pack_rows(peer_v))
    return y.reshape(n, d)


def kernel(x_prompt, x_sample, cache_k, cache_v, cache_kidx, page_table, state_conv, state_rglru, norm1_g, w_in, b_gates, q_norm_g, k_norm_g, conv_w, conv_b, w_rgate, b_rgate, w_igate, b_igate, lru_lambda, w_o_attn, w_o_rnn, w_out, norm2_g, w_peer_q, peer_k1, peer_k2, peer_u, peer_v):
    bp, sp, d = x_prompt.shape
    db, ts, _ = x_sample.shape
    assert ts == 1, "the sample path handles one new token per sequence"
    d_rnn = conv_w.shape[1]
    past = page_table.shape[1] * PAGE_SIZE
    n_p, n_s = bp * sp, db * ts
    n = -(-(n_p + n_s) // TOKEN_TILE) * TOKEN_TILE
    rnn_w = (conv_w, conv_b, w_rgate, b_rgate, w_igate, b_igate, lru_lambda)

    x = jnp.concatenate([x_prompt.reshape(n_p, d), x_sample.reshape(n_s, d), jnp.zeros((n - n_p - n_s, d), f32)])
    pos = jnp.concatenate([jnp.tile(jnp.arange(sp, dtype=f32), bp), past + jnp.tile(jnp.arange(ts, dtype=f32), db),
                           jnp.zeros((n - n_p - n_s,), f32)])
    q, k, v, qi, ki, wi, xr, yr, g_a, g_r = _front(x, pos, norm1_g, w_in, b_gates, q_norm_g, k_norm_g, d_rnn)
    wi = wi[:, :N_IDX_HEADS]
    pr = lambda a: a[:n_p].reshape(bp, sp, a.shape[1])
    sm = lambda a: a[n_p:n_p + n_s]

    attn_p = _prompt_attention(pr(q), pr(k), pr(v), pr(qi), pr(ki), pr(wi))
    rnn_p, h_p = _rglru_prompt(pr(xr), pr(yr), *rnn_w)
    attn_s = _sample_attention(sm(q), sm(k), sm(v), sm(qi), sm(ki), sm(wi), cache_k, cache_v, cache_kidx, page_table)
    rnn_s, h_s = _rglru_sample(sm(xr), sm(yr), state_conv, state_rglru, *rnn_w)

    tail = lambda w: jnp.zeros((n - n_p - n_s, w), bf16)
    attn_o = jnp.concatenate([attn_p.reshape(n_p, -1), attn_s, tail(attn_s.shape[1])])
    rnn_o = jnp.concatenate([rnn_p.reshape(n_p, d_rnn), rnn_s, tail(d_rnn)])
    x1, hn, eidx_t, gate_t = _back(x, attn_o, rnn_o, g_a, g_r, w_o_attn, w_o_rnn, w_out, norm2_g, w_peer_q,
                                   peer_k1, peer_k2)
    y = _peer(hn, x1, eidx_t, gate_t, peer_u, peer_v)

    conv_p = pr(xr)[:, sp - (CONV_W - 1):, :]
    conv_s = jnp.concatenate([state_conv[:, ts:, :], sm(xr).reshape(db, ts, d_rnn)], axis=1)
    heads = lambda a, b_, t_: a.reshape(b_, t_, N_KV_HEADS, HEAD_DIM)
    return (y[:n_p].reshape(bp, sp, d), y[n_p:n_p + n_s].reshape(db, ts, d),
            heads(k[:n_p], bp, sp), heads(v[:n_p], bp, sp), pr(ki),
            conv_p, h_p,
            heads(sm(k), db, ts), heads(sm(v), db, ts), sm(ki).reshape(db, ts, IDX_DIM),
            conv_s, h_s)
```

```python
import functools

import numpy as np
import jax
import jax.numpy as jnp
from jax import lax
from jax.experimental import pallas as pl
from jax.experimental.pallas import tpu as pltpu

f32 = jnp.float32
bf16 = jnp.bfloat16
i32 = jnp.int32

N_HEADS = 16
HEAD_DIM = 64
N_KV_HEADS = 4
KV_GROUP = N_HEADS // N_KV_HEADS
ROT_DIM = HEAD_DIM // 4
ROT_HALF = ROT_DIM // 2
ROPE_THETA = 500000.0
QK_EPS = 1e-6
N_IDX_HEADS = 8
IDX_DIM = 64
TOPK_MAX = 256
PAGE_SIZE = 128
CONV_W = 4
LRU_C = 8.0
PEER_HEADS = 8
PEER_KEY_DIM = 128
N_KEYS = 128
PEER_TOPK = 16
PEER_SLOTS = PEER_HEADS * PEER_TOPK
NORM_EPS = 1e-6

LANE = 128
SUBLANE = 8
PACKED_ROWS = 16
VMEM_LIMIT_BYTES = 56 * 1024 * 1024

INT_MIN = -(2 ** 31)
NEG_BIG = -1e30

TOKEN_TILE = 256
Q_TILE = LANE
KEY_CHUNK = 512
KEY_SUB = 256
SOFTMAX_PARTIALS = 32
RNN_CHUNK = 256
PEER_TOKENS = 64
PAGES_PER_STEP = 8
PEER_X = PEER_SLOTS * PACKED_ROWS


def _params(*sem):
    return pltpu.CompilerParams(dimension_semantics=sem, vmem_limit_bytes=VMEM_LIMIT_BYTES)


def _resident(shape):
    nd = len(shape)
    return pl.BlockSpec(shape, lambda *_: (0,) * nd, pipeline_mode=pl.Buffered(1))


def _gelu(x):
    return 0.5 * x * (1.0 + jnp.tanh(np.sqrt(2.0 / np.pi).astype(np.float32) * (x + 0.044715 * (x * x * x))))


def _dot(a, b):
    return jnp.dot(a, b, preferred_element_type=f32)


def _dot_nt(a, b):
    return lax.dot_general(a, b, (((1,), (1,)), ((), ())), preferred_element_type=f32)


def _front_kernel(x_ref, g1_ref, wa_ref, wb_ref, bg_ref, qg_ref, kg_ref, c_ref, s1_ref, s2_ref, bd_ref,
                  q_ref, k_ref, v_ref, qi_ref, ki_ref, wi_ref, xr_ref, yr_ref, ga_ref, gr_ref,
                  *, d_model, d_rnn):
    x = x_ref[...]
    h = (x * lax.rsqrt(jnp.mean(x * x, axis=-1, keepdims=True) + NORM_EPS) * g1_ref[...]).astype(bf16)
    cos, sin_hi, sin_lo = c_ref[...], s1_ref[...], s2_ref[...]
    bd = bd_ref[...]

    def rope(t):
        return t * cos + pltpu.roll(t, LANE - ROT_HALF, 1) * sin_hi + pltpu.roll(t, ROT_HALF, 1) * sin_lo

    def head_norm(t, g):
        sq = t * t
        hi = sq.astype(bf16)
        lo = (sq - hi.astype(f32)).astype(bf16)
        ss = _dot(hi, bd) + _dot(lo, bd)
        return t * lax.rsqrt(ss * (1.0 / HEAD_DIM) + QK_EPS) * g

    q_w = N_HEADS * HEAD_DIM
    kv_w = N_KV_HEADS * HEAD_DIM
    qi_w = N_IDX_HEADS * IDX_DIM
    pa = _dot(h, wa_ref[...])
    off = 0
    for j in range(q_w // LANE):
        t = pa[:, off + j * LANE: off + (j + 1) * LANE]
        q_ref[:, j * LANE:(j + 1) * LANE] = (rope(head_norm(t, qg_ref[...])) * (HEAD_DIM ** -0.5)).astype(bf16)
    off += q_w
    for j in range(kv_w // LANE):
        t = pa[:, off + j * LANE: off + (j + 1) * LANE]
        k_ref[:, j * LANE:(j + 1) * LANE] = rope(head_norm(t, kg_ref[...]))
    off += kv_w
    v_ref[...] = pa[:, off:off + kv_w]
    off += kv_w
    for j in range(qi_w // LANE):
        t = pa[:, off + j * LANE: off + (j + 1) * LANE]
        qi_ref[:, j * LANE:(j + 1) * LANE] = rope(t).astype(bf16)
    off += qi_w
    ki_ref[...] = rope(pa[:, off:off + LANE])[:, :IDX_DIM]
    off += LANE
    wi_ref[...] = pa[:, off:off + LANE] * (N_IDX_HEADS ** -0.5)

    xr_ref[...] = _dot(h, wb_ref[:, 0:d_rnn])
    yr_ref[...] = _dot(h, wb_ref[:, d_rnn:2 * d_rnn])
    o = 2 * d_rnn
    ga_ref[...] = jax.nn.sigmoid(_dot(h, wb_ref[:, o:o + d_model]) + bg_ref[:, 0:d_model])
    gr_ref[...] = jax.nn.sigmoid(_dot(h, wb_ref[:, o + d_model:o + 2 * d_model]) + bg_ref[:, d_model:2 * d_model])


def _rope_tables(pos):
    inv = ROPE_THETA ** (-jnp.arange(ROT_HALF, dtype=f32) / ROT_HALF)
    ang = pos.astype(f32)[:, None] * inv[None, :]
    c, s = jnp.cos(ang), jnp.sin(ang)
    n = pos.shape[0]
    z = lambda w: jnp.zeros((n, w), f32)
    cos = jnp.concatenate([c, c, jnp.ones((n, HEAD_DIM - ROT_DIM), f32)], axis=1)
    s_hi = jnp.concatenate([-s, z(HEAD_DIM - ROT_HALF)], axis=1)
    s_lo = jnp.concatenate([z(ROT_HALF), s, z(HEAD_DIM - ROT_DIM)], axis=1)
    rep = LANE // HEAD_DIM
    return jnp.tile(cos, (1, rep)), jnp.tile(s_hi, (1, rep)), jnp.tile(s_lo, (1, rep))


def _front(x, pos, norm1_g, w_in, b_gates, q_norm_g, k_norm_g, d_rnn):
    n, d_model = x.shape
    q_w, kv_w, qi_w = N_HEADS * HEAD_DIM, N_KV_HEADS * HEAD_DIM, N_IDX_HEADS * IDX_DIM
    sizes = [q_w, kv_w, kv_w, qi_w, IDX_DIM, N_IDX_HEADS, d_rnn, d_rnn, d_model, d_model]
    offs = np.cumsum([0] + sizes)
    parts = [w_in[:, offs[i]:offs[i + 1]] for i in range(len(sizes))]
    padl = lambda w: jnp.pad(w, ((0, 0), (0, LANE - w.shape[1])))
    wa = jnp.concatenate(parts[:4] + [padl(parts[4]), padl(parts[5])], axis=1).astype(bf16)
    wb = jnp.concatenate(parts[6:], axis=1).astype(bf16)
    cos, s_hi, s_lo = _rope_tables(pos)
    rep = LANE // HEAD_DIM
    head_of_lane = np.arange(LANE) // HEAD_DIM
    bd = jnp.asarray(head_of_lane[:, None] == head_of_lane[None, :], bf16)
    tm = TOKEN_TILE
    row = lambda w: pl.BlockSpec((tm, w), lambda i: (i, 0))
    outs = [(q_w, bf16), (kv_w, f32), (kv_w, f32), (qi_w, bf16), (IDX_DIM, f32), (LANE, f32),
            (d_rnn, f32), (d_rnn, f32), (d_model, f32), (d_model, f32)]
    return pl.pallas_call(
        functools.partial(_front_kernel, d_model=d_model, d_rnn=d_rnn),
        grid=(n // tm,),
        in_specs=[row(d_model), _resident((1, d_model)), _resident(wa.shape), _resident(wb.shape),
                  _resident((1, 2 * d_model)), _resident((1, LANE)), _resident((1, LANE)),
                  row(LANE), row(LANE), row(LANE), _resident((LANE, LANE))],
        out_specs=[row(w) for w, _ in outs],
        out_shape=[jax.ShapeDtypeStruct((n, w), dt) for w, dt in outs],
        compiler_params=_params("parallel"),
        name="front",
    )(x, norm1_g.reshape(1, -1), wa, wb, b_gates.reshape(1, -1),
      jnp.tile(q_norm_g, rep).reshape(1, LANE), jnp.tile(k_norm_g, rep).reshape(1, LANE),
      cos, s_hi, s_lo, bd)


def _score_key(score):
    score = jnp.where(score == 0.0, 0.0, score)
    bits = pltpu.bitcast(score, i32)
    return bits ^ ((bits >> 31) & 0x7FFFFFFF)


def _kth_largest_key(count_ge, k, like):
    n_keys = count_ge(jnp.full_like(like, INT_MIN + 1))
    c0 = count_ge(jnp.zeros_like(like))
    ans = jnp.where(c0 >= k, 0, INT_MIN).astype(i32)
    cnt = jnp.where(c0 >= k, c0, n_keys)

    def unsettled(st):
        i, _, cnt = st
        return (i < 31) & (jnp.max(jnp.where((cnt == k) | (n_keys < k), 0, 1)) > 0)

    def body(st):
        i, ans, cnt = st
        cand = ans + lax.shift_left(jnp.int32(1), jnp.int32(30) - i)
        c = count_ge(cand)
        return i + 1, jnp.where(c >= k, cand, ans), jnp.where(c >= k, c, cnt)

    _, ans, cnt = lax.while_loop(unsettled, body, (jnp.int32(0), ans, cnt))
    return jnp.maximum(ans, INT_MIN + 1), cnt


def _prompt_attn_kernel(qt_ref, qit_ref, wi_ref, ki_ref, k_ref, vt_ref, low_ref, o_ref, keys_ref, bias_ref,
                        m_ref, top_ref, l_ref, acc_ref, *, top_k):
    qb = pl.program_id(1)
    tq, ck, sub = Q_TILE, KEY_CHUNK, KEY_SUB
    n_chunks = (qb * tq + tq + ck - 1) // ck
    qpos = qb * tq + lax.broadcasted_iota(i32, (sub, tq), 1)
    krow = lax.broadcasted_iota(i32, (sub, tq), 0)
    w = wi_ref[0, 0]

    def score_chunk(c, carry):
        for u in range(ck // sub):
            kc = ki_ref[0, c, u * sub:(u + 1) * sub, :]
            acc = jnp.zeros((sub, tq), f32)
            for h in range(N_IDX_HEADS):
                acc = acc + jnp.maximum(_dot(kc, qit_ref[0, 0, h]), 0.0) * w[h:h + 1, :]
            causal = c * ck + u * sub + krow <= qpos
            keys_ref[c, u * sub:(u + 1) * sub, :] = jnp.where(causal, _score_key(acc), INT_MIN)
        return carry

    lax.fori_loop(0, n_chunks, score_chunk, 0)

    def count_ge(cand):
        lanes_of_acc = 8 * SUBLANE
        def body(c, acc):
            m = (keys_ref[c] >= cand).astype(f32)
            return acc + jnp.sum(m.reshape(ck // lanes_of_acc, lanes_of_acc, tq), axis=0)
        acc = lax.fori_loop(0, n_chunks, body, jnp.zeros((lanes_of_acc, tq), f32))
        return jnp.sum(acc, axis=0, keepdims=True)

    thr, n_ge = _kth_largest_key(count_ge, float(top_k), jnp.zeros((1, tq), i32))


    @pl.when(jnp.max(n_ge) > top_k)
    def _():
        need = top_k - count_ge(thr + 1)

        def body(c, before):
            kc = keys_ref[c]
            eq = kc == thr
            eqb = eq.astype(bf16)
            rank = _dot(low_ref[...], eqb) + before
            keys_ref[c] = jnp.where(eq & (rank >= need), INT_MIN, kc)
            return before + jnp.sum(eqb.astype(f32), axis=0, keepdims=True)

        lax.fori_loop(0, n_chunks, body, jnp.zeros((1, tq), f32))

    def to_bias(c, carry):
        bias_ref[c] = jnp.where(keys_ref[c] >= thr, 0.0, NEG_BIG)
        return carry

    lax.fori_loop(0, n_chunks, to_bias, 0)

    part = m_ref.shape[1]
    fold = lambda x, op: op(x.reshape(ck // part, part, tq), axis=0)
    m_ref[...] = jnp.full(m_ref.shape, NEG_BIG, f32)
    l_ref[...] = jnp.zeros(l_ref.shape, f32)
    acc_ref[...] = jnp.zeros(acc_ref.shape, f32)

    def col_max(c, carry):
        bias = bias_ref[c]
        for g in range(N_KV_HEADS):
            s4 = _dot(k_ref[0, g, c], qt_ref[0, 0, g])
            for r in range(KV_GROUP):
                h = g * KV_GROUP + r
                m_ref[h] = jnp.maximum(m_ref[h], fold(s4[:, r * tq:(r + 1) * tq] + bias, jnp.max))
        return carry

    lax.fori_loop(0, n_chunks, col_max, 0)
    for h in range(N_HEADS):
        top_ref[h] = jnp.max(m_ref[h], axis=0, keepdims=True)

    def accumulate(c, carry):
        bias = bias_ref[c]
        for g in range(N_KV_HEADS):
            s4 = _dot(k_ref[0, g, c], qt_ref[0, 0, g])
            ps = []
            for r in range(KV_GROUP):
                h = g * KV_GROUP + r
                p = jnp.exp(s4[:, r * tq:(r + 1) * tq] + bias - top_ref[h])
                l_ref[h] = l_ref[h] + fold(p, jnp.sum)
                ps.append(p.astype(bf16))
            acc_ref[g] = acc_ref[g] + _dot(vt_ref[0, g, c], jnp.concatenate(ps, axis=1))
        return carry

    lax.fori_loop(0, n_chunks, accumulate, 0)
    for g in range(N_KV_HEADS):
        l4 = jnp.concatenate([jnp.sum(l_ref[g * KV_GROUP + r], axis=0, keepdims=True) for r in range(KV_GROUP)], axis=1)
        o_ref[0, 0, g] = (acc_ref[g] / l4).astype(o_ref.dtype)


def _prompt_attention(q, k, v, qi, ki, wi):
    b, s, _ = q.shape
    top_k = min(TOPK_MAX, s // 4)
    tq, ck = Q_TILE, KEY_CHUNK
    nq, nc = s // tq, s // ck
    gq = KV_GROUP * tq
    qt = q.reshape(b, nq, tq, N_KV_HEADS, KV_GROUP, HEAD_DIM).transpose(0, 1, 3, 5, 4, 2).reshape(b, nq, N_KV_HEADS, HEAD_DIM, gq)
    qit = qi.reshape(b, nq, tq, N_IDX_HEADS, IDX_DIM).transpose(0, 1, 3, 4, 2)
    wit = (wi * (IDX_DIM ** -0.5)).reshape(b, nq, tq, N_IDX_HEADS).transpose(0, 1, 3, 2)
    kic = ki.astype(bf16).reshape(b, nc, ck, IDX_DIM)
    kc = k.astype(bf16).reshape(b, nc, ck, N_KV_HEADS, HEAD_DIM).transpose(0, 3, 1, 2, 4)
    vt = v.astype(bf16).reshape(b, nc, ck, N_KV_HEADS, HEAD_DIM).transpose(0, 3, 1, 4, 2)
    low = jnp.asarray(np.arange(ck)[:, None] > np.arange(ck)[None, :], bf16)
    out = pl.pallas_call(
        functools.partial(_prompt_attn_kernel, top_k=top_k),
        grid=(b, nq),
        in_specs=[
            pl.BlockSpec((1, 1, N_KV_HEADS, HEAD_DIM, gq), lambda i, j: (i, j, 0, 0, 0)),
            pl.BlockSpec((1, 1, N_IDX_HEADS, IDX_DIM, tq), lambda i, j: (i, j, 0, 0, 0)),
            pl.BlockSpec((1, 1, N_IDX_HEADS, tq), lambda i, j: (i, j, 0, 0)),
            pl.BlockSpec((1, nc, ck, IDX_DIM), lambda i, j: (i, 0, 0, 0)),
            pl.BlockSpec((1, N_KV_HEADS, nc, ck, HEAD_DIM), lambda i, j: (i, 0, 0, 0, 0)),
            pl.BlockSpec((1, N_KV_HEADS, nc, HEAD_DIM, ck), lambda i, j: (i, 0, 0, 0, 0)),
            _resident((ck, ck)),
        ],
        out_specs=pl.BlockSpec((1, 1, N_KV_HEADS, HEAD_DIM, gq), lambda i, j: (i, j, 0, 0, 0)),
        out_shape=jax.ShapeDtypeStruct((b, nq, N_KV_HEADS, HEAD_DIM, gq), bf16),
        scratch_shapes=[pltpu.VMEM((nc, ck, tq), i32), pltpu.VMEM((nc, ck, tq), f32),
                        pltpu.VMEM((N_HEADS, SOFTMAX_PARTIALS, tq), f32), pltpu.VMEM((N_HEADS, 1, tq), f32),
                        pltpu.VMEM((N_HEADS, SOFTMAX_PARTIALS, tq), f32),
                        pltpu.VMEM((N_KV_HEADS, HEAD_DIM, gq), f32)],
        compiler_params=_params("parallel", "arbitrary"),
        name="prompt_attention",
    )(qt, qit, wit, kic, kc, vt, low)
    out = out.reshape(b, nq, N_KV_HEADS, HEAD_DIM, KV_GROUP, tq).transpose(0, 1, 5, 2, 4, 3)
    return out.reshape(b, s, N_HEADS * HEAD_DIM)


def _sample_select_kernel(pt_ref, qi_ref, wi_ref, kin_ref, tri_ref, lt_ref, *rest, top_k, n_pages):
    page_refs, (sel_ref, seln_ref, keys_ref) = rest[:PAGES_PER_STEP], rest[PAGES_PER_STEP:]
    j = pl.program_id(1)
    qi = qi_ref[0]
    w = wi_ref[0]

    for i in range(PAGES_PER_STEP):
        s = _dot_nt(qi, page_refs[i][0].astype(bf16))
        score = jnp.sum(jnp.maximum(s, 0.0) * w, axis=0, keepdims=True)
        keys_ref[pl.ds(j * PAGES_PER_STEP + i, 1), :] = _score_key(score)

    @pl.when(j == pl.num_programs(1) - 1)
    def _():
        kn = kin_ref[0].astype(bf16).astype(f32)
        s_new = jnp.sum(qi.astype(f32) * kn, axis=1, keepdims=True)
        key_new = _score_key(jnp.sum(jnp.maximum(s_new, 0.0) * w, axis=0, keepdims=True))
        keys = keys_ref[...]

        def count_ge(cand):
            c = jnp.sum((keys >= cand).astype(f32), axis=1, keepdims=True)
            return jnp.sum(c, axis=0, keepdims=True) + (key_new >= cand).astype(f32)

        thr, _ = _kth_largest_key(count_ge, float(top_k), jnp.zeros((1, 1), i32))
        need = top_k - (count_ge(thr + 1))
        eq = keys == thr
        eqb = eq.astype(bf16)
        in_row = _dot(eqb, tri_ref[...])
        per_page = jnp.sum(eqb.astype(f32), axis=1, keepdims=True)
        before = _dot(lt_ref[...], per_page.astype(bf16) * jnp.ones((1, LANE), bf16))[:, 0:1]
        rank = in_row + before
        sel = (keys > thr) | (eq & (rank < need))
        sel_ref[0] = sel.astype(f32)
        n_eq = jnp.sum(per_page, axis=0, keepdims=True)
        sel_new = (key_new > thr) | ((key_new == thr) & (n_eq < need))
        seln_ref[0] = jnp.broadcast_to(sel_new.astype(f32), (1, LANE))


def _sample_attend_kernel(pt_ref, q_ref, sel_ref, seln_ref, kn_ref, vn_ref, gm_ref, *rest, n_pages):
    k_refs = rest[:PAGES_PER_STEP]
    v_refs = rest[PAGES_PER_STEP:2 * PAGES_PER_STEP]
    o_ref, m_ref, l_ref, acc_ref = rest[2 * PAGES_PER_STEP:]
    j = pl.program_id(1)
    q = q_ref[0]

    @pl.when(j == 0)
    def _():
        m_ref[...] = jnp.full(m_ref.shape, NEG_BIG, f32)
        l_ref[...] = jnp.zeros(l_ref.shape, f32)
        acc_ref[...] = jnp.zeros(acc_ref.shape, f32)

    def update(s, vals):
        m = m_ref[...]
        m_new = jnp.maximum(m, jnp.max(s, axis=1, keepdims=True))
        alpha = jnp.exp(m - m_new)
        p = jnp.exp(s - m_new)
        l_ref[...] = alpha * l_ref[...] + jnp.sum(p, axis=1, keepdims=True)
        acc_ref[...] = alpha * acc_ref[...] + vals(p)
        m_ref[...] = m_new

    for i in range(PAGES_PER_STEP):
        sel = sel_ref[0, pl.ds(j * PAGES_PER_STEP + i, 1), :] > 0.5
        s = jnp.where(sel, _dot_nt(q, k_refs[i][0].astype(bf16)), NEG_BIG)
        update(s, lambda p, i=i: _dot(p.astype(bf16), v_refs[i][0].astype(bf16)))

    @pl.when(j == pl.num_programs(1) - 1)
    def _():
        kn = kn_ref[0].astype(bf16).astype(f32)
        vn = vn_ref[0].astype(bf16).astype(f32)
        s = jnp.sum(q.astype(f32) * kn, axis=1, keepdims=True)
        s = jnp.where(seln_ref[0][:, 0:1] > 0.5, s, NEG_BIG)
        update(s, lambda p: p.astype(bf16).astype(f32) * vn)
        full = acc_ref[...] / l_ref[...] * gm_ref[...]
        out = full[:, 0:HEAD_DIM]
        for g in range(1, N_KV_HEADS):
            out = out + full[:, g * HEAD_DIM:(g + 1) * HEAD_DIM]
        o_ref[0] = out.astype(o_ref.dtype)


def _sample_attention(q, k_new, v_new, qi, ki_new, wi, cache_k, cache_v, cache_kidx, page_table):
    db = q.shape[0]
    n_pages = page_table.shape[1]
    past = n_pages * PAGE_SIZE
    top_k = min(TOPK_MAX, (past + 1) // 4)
    pps = PAGES_PER_STEP
    steps = n_pages // pps
    kv_w = N_KV_HEADS * HEAD_DIM
    n_pool = cache_k.shape[0]
    ck2 = cache_k.reshape(n_pool, PAGE_SIZE, kv_w)
    cv2 = cache_v.reshape(n_pool, PAGE_SIZE, kv_w)

    def page_spec(width, i):
        return pl.BlockSpec((1, PAGE_SIZE, width), lambda b, j, pt, i=i: (pt[b, j * pps + i], 0, 0))

    per_seq = lambda *shape: pl.BlockSpec((1,) + shape, lambda b, j, pt: (b,) + (0,) * len(shape))
    const = lambda *shape: pl.BlockSpec(shape, lambda b, j, pt: (0,) * len(shape))
    tri = jnp.asarray(np.arange(PAGE_SIZE)[:, None] < np.arange(PAGE_SIZE)[None, :], bf16)
    lower = jnp.asarray(np.arange(n_pages)[:, None] > np.arange(n_pages)[None, :], bf16)

    sel, sel_new = pl.pallas_call(
        functools.partial(_sample_select_kernel, top_k=top_k, n_pages=n_pages),
        grid_spec=pltpu.PrefetchScalarGridSpec(
            num_scalar_prefetch=1,
            grid=(db, steps),
            in_specs=[per_seq(N_IDX_HEADS, IDX_DIM), per_seq(N_IDX_HEADS, 1), per_seq(1, IDX_DIM),
                      const(PAGE_SIZE, PAGE_SIZE), const(n_pages, n_pages)]
                     + [page_spec(IDX_DIM, i) for i in range(pps)],
            out_specs=[per_seq(n_pages, PAGE_SIZE), per_seq(1, LANE)],
            scratch_shapes=[pltpu.VMEM((n_pages, PAGE_SIZE), i32)],
        ),
        out_shape=[jax.ShapeDtypeStruct((db, n_pages, PAGE_SIZE), f32), jax.ShapeDtypeStruct((db, 1, LANE), f32)],
        compiler_params=_params("parallel", "arbitrary"),
        name="sample_select",
    )(page_table, qi.reshape(db, N_IDX_HEADS, IDX_DIM), (wi * (IDX_DIM ** -0.5)).reshape(db, N_IDX_HEADS, 1),
      ki_new.reshape(db, 1, IDX_DIM), tri, lower, *([cache_kidx] * pps))

    group_mask = np.repeat(np.arange(N_HEADS)[:, None] // KV_GROUP == np.arange(N_KV_HEADS)[None, :], HEAD_DIM, axis=1)
    q_bd = jnp.where(group_mask[None], jnp.tile(q.reshape(db, N_HEADS, HEAD_DIM), (1, 1, N_KV_HEADS)), 0).astype(bf16)
    out = pl.pallas_call(
        functools.partial(_sample_attend_kernel, n_pages=n_pages),
        grid_spec=pltpu.PrefetchScalarGridSpec(
            num_scalar_prefetch=1,
            grid=(db, steps),
            in_specs=[per_seq(N_HEADS, kv_w), per_seq(n_pages, PAGE_SIZE), per_seq(1, LANE),
                      per_seq(1, kv_w), per_seq(1, kv_w), const(N_HEADS, kv_w)]
                     + [page_spec(kv_w, i) for i in range(pps)] * 2,
            out_specs=per_seq(N_HEADS, HEAD_DIM),
            scratch_shapes=[pltpu.VMEM((N_HEADS, 1), f32), pltpu.VMEM((N_HEADS, 1), f32), pltpu.VMEM((N_HEADS, kv_w), f32)],
        ),
        out_shape=jax.ShapeDtypeStruct((db, N_HEADS, HEAD_DIM), bf16),
        compiler_params=_params("parallel", "arbitrary"),
        name="sample_attend",
    )(page_table, q_bd, sel, sel_new, k_new.reshape(db, 1, kv_w), v_new.reshape(db, 1, kv_w),
      jnp.asarray(group_mask, f32), *([ck2] * pps), *([cv2] * pps))
    return out.reshape(db, N_HEADS * HEAD_DIM)


def _rglru_coeffs(xc, wr_ref, br_ref, wi_ref, bi_ref, lam_ref):
    xb = xc.astype(bf16)
    r = jax.nn.sigmoid(_dot(xb, wr_ref[...]) + br_ref[...])
    i = jax.nn.sigmoid(_dot(xb, wi_ref[...]) + bi_ref[...])
    z = -lam_ref[...]
    softplus = jnp.maximum(z, 0.0) + jnp.log1p(jnp.exp(-jnp.abs(z)))
    log_a = -LRU_C * r * softplus
    a = jnp.exp(log_a)
    th = jnp.tanh(log_a)
    b = jnp.sqrt(-2.0 * th / (1.0 - th)) * (i * xc)
    return a, b


def _rglru_seq_kernel(xr_ref, yr_ref, cw_ref, cb_ref, wr_ref, br_ref, wi_ref, bi_ref, lam_ref,
                      o_ref, hl_ref, xprev_ref, h_ref, a_ref, b_ref):
    t = pl.program_id(1)
    tc, d = xr_ref.shape[1], xr_ref.shape[2]

    @pl.when(t == 0)
    def _():
        xprev_ref[...] = jnp.zeros(xprev_ref.shape, f32)
        h_ref[...] = jnp.zeros(h_ref.shape, f32)

    xr = xr_ref[0]
    xcat = jnp.concatenate([xprev_ref[...], xr], axis=0)
    xc = cb_ref[...] + cw_ref[0:1, :] * xcat[SUBLANE - 3:SUBLANE - 3 + tc]
    for j in range(1, CONV_W):
        xc = xc + cw_ref[j:j + 1, :] * xcat[SUBLANE - 3 + j:SUBLANE - 3 + j + tc]
    xprev_ref[...] = xr[tc - SUBLANE:, :]
    a, b = _rglru_coeffs(xc, wr_ref, br_ref, wi_ref, bi_ref, lam_ref)
    a_ref[...] = a
    b_ref[...] = b
    row = lax.broadcasted_iota(i32, (SUBLANE, d), 0)

    def slab(s, h_prev):
        r0 = pl.multiple_of(s * SUBLANE, SUBLANE)
        a8 = a_ref[pl.ds(r0, SUBLANE), :]
        b8 = b_ref[pl.ds(r0, SUBLANE), :]
        for sh in (1, 2, 4):
            keep = row >= sh
            b8 = jnp.where(keep, a8 * pltpu.roll(b8, sh, 0) + b8, b8)
            a8 = jnp.where(keep, a8 * pltpu.roll(a8, sh, 0), a8)
        h8 = a8 * h_prev + b8
        o_ref[0, pl.ds(r0, SUBLANE), :] = (h8 * _gelu(yr_ref[0, pl.ds(r0, SUBLANE), :])).astype(o_ref.dtype)
        return h8[SUBLANE - 1:SUBLANE, :]

    h_last = lax.fori_loop(0, tc // SUBLANE, slab, h_ref[...])
    h_ref[...] = h_last
    hl_ref[0] = h_last


def _block_diag(w):
    n, a, b = w.shape
    eye = jnp.eye(n, dtype=w.dtype)
    return (eye[:, None, :, None] * w[:, :, None, :]).reshape(n * a, n * b).astype(bf16)


def _rglru_prompt(xr, yr, conv_w, conv_b, w_rgate, b_rgate, w_igate, b_igate, lru_lambda):
    b, t, d = xr.shape
    tc = RNN_CHUNK
    vec = lambda v: v.reshape(1, d)
    seq = pl.BlockSpec((1, tc, d), lambda i, j: (i, j, 0))
    out, h_last = pl.pallas_call(
        _rglru_seq_kernel,
        grid=(b, t // tc),
        in_specs=[seq, seq, _resident((CONV_W, d)), _resident((1, d)), _resident((d, d)), _resident((1, d)),
                  _resident((d, d)), _resident((1, d)), _resident((1, d))],
        out_specs=[seq, pl.BlockSpec((1, 1, d), lambda i, j: (i, 0, 0))],
        out_shape=[jax.ShapeDtypeStruct((b, t, d), bf16), jax.ShapeDtypeStruct((b, 1, d), f32)],
        scratch_shapes=[pltpu.VMEM((SUBLANE, d), f32), pltpu.VMEM((1, d), f32),
                        pltpu.VMEM((tc, d), f32), pltpu.VMEM((tc, d), f32)],
        compiler_params=_params("parallel", "arbitrary"),
        name="rglru_prompt",
    )(xr, yr, conv_w, vec(conv_b), _block_diag(w_rgate), vec(b_rgate), _block_diag(w_igate), vec(b_igate),
      vec(lru_lambda))
    return out, h_last.reshape(b, d)


def _rglru_step_kernel(xr_ref, yr_ref, c0_ref, c1_ref, c2_ref, h0_ref, cw_ref, cb_ref, wr_ref, br_ref, wi_ref,
                       bi_ref, lam_ref, o_ref, h_ref):
    xc = cb_ref[...] + cw_ref[0:1, :] * c0_ref[...]
    xc = xc + cw_ref[1:2, :] * c1_ref[...]
    xc = xc + cw_ref[2:3, :] * c2_ref[...]
    xc = xc + cw_ref[3:4, :] * xr_ref[...]
    a, b = _rglru_coeffs(xc, wr_ref, br_ref, wi_ref, bi_ref, lam_ref)
    h = a * h0_ref[...] + b
    h_ref[...] = h
    o_ref[...] = (h * _gelu(yr_ref[...])).astype(o_ref.dtype)


def _rglru_sample(xr, yr, state_conv, h0, conv_w, conv_b, w_rgate, b_rgate, w_igate, b_igate, lru_lambda):
    db, d = xr.shape
    vec = lambda v: v.reshape(1, d)
    return pl.pallas_call(
        _rglru_step_kernel,
        out_shape=[jax.ShapeDtypeStruct((db, d), bf16), jax.ShapeDtypeStruct((db, d), f32)],
        compiler_params=pltpu.CompilerParams(vmem_limit_bytes=VMEM_LIMIT_BYTES),
        name="rglru_sample",
    )(xr, yr, state_conv[:, 0], state_conv[:, 1], state_conv[:, 2], h0, conv_w, vec(conv_b),
      _block_diag(w_rgate), vec(b_rgate), _block_diag(w_igate), vec(b_igate), vec(lru_lambda))


def _top_rows(x, k, payload=None):
    n = x.shape[0]
    row = lax.broadcasted_iota(i32, x.shape, 0)
    vals, picked = [], []
    for _ in range(k):
        m = jnp.max(x, axis=0, keepdims=True)
        r = jnp.min(jnp.where(x == m, row, n), axis=0, keepdims=True)
        hit = row == r
        vals.append(m)
        picked.append(r if payload is None else jnp.sum(jnp.where(hit, payload, 0), axis=0, keepdims=True))
        x = jnp.where(hit, -jnp.inf, x)
    return jnp.concatenate(vals, axis=0), jnp.concatenate(picked, axis=0)


_PAIR_COUNTS = [PEER_TOPK // (r1 + 1) for r1 in range(PEER_TOPK)]


def _back_kernel(x_ref, ao_ref, ro_ref, ga_ref, gr_ref, woa_ref, wor_ref, wout_ref, g2_ref, wqt_ref, k1_ref, k2_ref,
                 x1_ref, hn_ref, eidx_ref, gate_ref):
    mixed = ga_ref[...] * _dot(ao_ref[...], woa_ref[...]) + gr_ref[...] * _dot(ro_ref[...], wor_ref[...])
    x1 = x_ref[...] + _dot(mixed.astype(bf16), wout_ref[...])
    x1_ref[...] = x1
    hn = (x1 * lax.rsqrt(jnp.mean(x1 * x1, axis=-1, keepdims=True) + NORM_EPS) * g2_ref[...]).astype(bf16)
    hn_ref[...] = hn
    qt = _dot_nt(wqt_ref[...], hn).astype(bf16)
    half = PEER_KEY_DIM // 2
    for h in range(PEER_HEADS):
        base = h * PEER_KEY_DIM
        v1, i1 = _top_rows(_dot(k1_ref[...], qt[base:base + half, :]), PEER_TOPK)
        v2, i2 = _top_rows(_dot(k2_ref[...], qt[base + half:base + PEER_KEY_DIM, :]), PEER_TOPK)
        cand, cid = [], []
        for r1, n2 in enumerate(_PAIR_COUNTS):
            cand.append(v1[r1:r1 + 1, :] + v2[0:n2, :])
            cid.append(i1[r1:r1 + 1, :] * N_KEYS + i2[0:n2, :])
        pad = -sum(_PAIR_COUNTS) % SUBLANE
        tm = v1.shape[1]
        cand.append(jnp.full((pad, tm), -jnp.inf, f32))
        cid.append(jnp.zeros((pad, tm), i32))
        sv, eidx = _top_rows(jnp.concatenate(cand, axis=0), PEER_TOPK, jnp.concatenate(cid, axis=0))
        p = jnp.exp(sv - sv[0:1, :])
        eidx_ref[h * PEER_TOPK:(h + 1) * PEER_TOPK, :] = eidx
        gate_ref[h * PEER_TOPK:(h + 1) * PEER_TOPK, :] = p / jnp.sum(p, axis=0, keepdims=True)


def _back(x, attn_o, rnn_o, g_a, g_r, w_o_attn, w_o_rnn, w_out, norm2_g, w_peer_q, peer_k1, peer_k2):
    n, d = x.shape
    tm = TOKEN_TILE
    row = lambda w: pl.BlockSpec((tm, w), lambda i: (i, 0))
    col = pl.BlockSpec((PEER_SLOTS, tm), lambda i: (0, i))
    ws = [w_o_attn.astype(bf16), w_o_rnn.astype(bf16), w_out.astype(bf16), norm2_g.reshape(1, d),
          w_peer_q.T.astype(bf16), peer_k1.astype(bf16), peer_k2.astype(bf16)]
    return pl.pallas_call(
        _back_kernel,
        grid=(n // tm,),
        in_specs=[row(d), row(attn_o.shape[1]), row(rnn_o.shape[1]), row(d), row(d)] + [_resident(w.shape) for w in ws],
        out_specs=[row(d), row(d), col, col],
        out_shape=[jax.ShapeDtypeStruct((n, d), f32), jax.ShapeDtypeStruct((n, d), bf16),
                   jax.ShapeDtypeStruct((PEER_SLOTS, n), i32), jax.ShapeDtypeStruct((PEER_SLOTS, n), f32)],
        compiler_params=_params("parallel"),
        name="back",
    )(x, attn_o, rnn_o, g_a, g_r, *ws)


def _pack_rows(table):
    e, d = table.shape
    bits = lax.bitcast_convert_type(table.astype(bf16), jnp.uint16).astype(jnp.uint32).reshape(e // 2, 2, d // LANE, LANE)
    return lax.bitcast_convert_type(bits[:, 0] | (bits[:, 1] << 16), i32)


def _gather_tiles(table_ref, tile_ref, t):
    tiles = [table_ref[tile_ref[t, s]] for s in range(PEER_SLOTS)]
    return pltpu.bitcast(jnp.concatenate(tiles, axis=0), bf16)


def _peer_act_kernel(tile_ref, hn_ref, want_ref, gate_ref, u_ref, coef_ref):
    row = lax.broadcasted_iota(i32, (SUBLANE, LANE), 0)
    row2 = (2 * row).astype(f32)
    lane = lax.broadcasted_iota(i32, (SUBLANE, LANE), 1)
    n_blk = PEER_X // LANE

    def token_group(gi, carry):
        t0 = pl.multiple_of(gi * SUBLANE, SUBLANE)
        want = want_ref[pl.ds(t0, SUBLANE), :]
        acts = [jnp.zeros((SUBLANE, LANE), f32) for _ in range(n_blk)]
        for j in range(SUBLANE):
            rows = _gather_tiles(u_ref, tile_ref, t0 + j)
            r = _dot_nt(hn_ref[t0 + j], rows)
            for blk in range(n_blk):
                cols = slice(blk * LANE, (blk + 1) * LANE)
                z = jnp.where(want[j:j + 1, cols] == row2, r[:, cols], 0.0)
                for sh in (4, 2, 1):
                    z = z + pltpu.roll(z, sh, 0)
                for sh in (8, 4, 2, 1):
                    z = z + jnp.where((lane & sh) == 0, pltpu.roll(z, LANE - sh, 1), pltpu.roll(z, sh, 1))
                acts[blk] = jnp.where(row == j, z, acts[blk])
        act = jnp.concatenate(acts, axis=1)
        coef_ref[pl.ds(t0, SUBLANE), :] = gate_ref[pl.ds(t0, SUBLANE), :] * _gelu(act)
        return carry

    lax.fori_loop(0, hn_ref.shape[0] // SUBLANE, token_group, 0)


def _peer_out_kernel(tile_ref, want_ref, coef_ref, x1_ref, v_ref, y_ref):
    row2 = (2 * lax.broadcasted_iota(i32, (SUBLANE, PEER_X), 0)).astype(f32)

    def token_group(gi, carry):
        t0 = pl.multiple_of(gi * SUBLANE, SUBLANE)
        want = want_ref[pl.ds(t0, SUBLANE), :]
        coef = coef_ref[pl.ds(t0, SUBLANE), :]
        for j in range(SUBLANE):
            rows = _gather_tiles(v_ref, tile_ref, t0 + j)
            pick = jnp.where(want[j:j + 1, :] == row2, coef[j:j + 1, :], 0.0).astype(bf16)
            y_ref[t0 + j] = x1_ref[t0 + j] + _dot(pick, rows)
        return carry

    lax.fori_loop(0, x1_ref.shape[0] // SUBLANE, token_group, 0)


def _peer(hn, x1, eidx_t, gate_t, peer_u, peer_v):
    n, d = x1.shape
    tt = PEER_TOKENS
    rows = d // LANE
    eidx = eidx_t.T
    tile = eidx >> 1
    want = (jnp.tile(jnp.arange(PACKED_ROWS, dtype=i32), PEER_SLOTS)[None, :]
            - jnp.repeat(eidx & 1, PACKED_ROWS, axis=1)).astype(f32)
    gate = jnp.repeat(gate_t.T, PACKED_ROWS, axis=1)
    smem = pl.BlockSpec((tt, PEER_SLOTS), lambda i: (i, 0), memory_space=pltpu.SMEM)
    tiles = pl.BlockSpec((tt, rows, LANE), lambda i: (i, 0, 0))
    wide = pl.BlockSpec((tt, PEER_X), lambda i: (i, 0))
    table = _resident((peer_u.shape[0] // 2, rows, LANE))
    coef = pl.pallas_call(
        _peer_act_kernel,
        grid=(n // tt,),
        in_specs=[smem, tiles, wide, wide, table],
        out_specs=wide,
        out_shape=jax.ShapeDtypeStruct((n, PEER_X), f32),
        compiler_params=_params("parallel"),
        name="peer_act",
    )(tile, hn.reshape(n, rows, LANE), want, gate, _pack_rows(peer_u))
    y = pl.pallas_call(
        _peer_out_kernel,
        grid=(n // tt,),
        in_specs=[smem, wide, wide, tiles, table],
        out_specs=tiles,
        out_shape=jax.ShapeDtypeStruct((n, rows, LANE), f32),
        compiler_params=_params("parallel"),
        name="peer_out",
    )(tile, want, coef, x1.reshape(n, rows, LANE), _pack_rows(peer_v))
    return y.reshape(n, d)


def kernel(x_prompt, x_sample, cache_k, cache_v, cache_kidx, page_table, state_conv, state_rglru, norm1_g, w_in, b_gates, q_norm_g, k_norm_g, conv_w, conv_b, w_rgate, b_rgate, w_igate, b_igate, lru_lambda, w_o_attn, w_o_rnn, w_out, norm2_g, w_peer_q, peer_k1, peer_k2, peer_u, peer_v):
    bp, sp, d = x_prompt.shape
    db, ts, _ = x_sample.shape
    assert ts == 1, "the sample path handles one new token per sequence"
    d_rnn = conv_w.shape[1]
    past = page_table.shape[1] * PAGE_SIZE
    n_p, n_s = bp * sp, db * ts
    n = -(-(n_p + n_s) // TOKEN_TILE) * TOKEN_TILE
    rnn_w = (conv_w, conv_b, w_rgate, b_rgate, w_igate, b_igate, lru_lambda)

    x = jnp.concatenate([x_prompt.reshape(n_p, d), x_sample.reshape(n_s, d), jnp.zeros((n - n_p - n_s, d), f32)])
    pos = jnp.concatenate([jnp.tile(jnp.arange(sp, dtype=f32), bp), past + jnp.tile(jnp.arange(ts, dtype=f32), db),
                           jnp.zeros((n - n_p - n_s,), f32)])
    q, k, v, qi, ki, wi, xr, yr, g_a, g_r = _front(x, pos, norm1_g, w_in, b_gates, q_norm_g, k_norm_g, d_rnn)
    wi = wi[:, :N_IDX_HEADS]
    pr = lambda a: a[:n_p].reshape(bp, sp, a.shape[1])
    sm = lambda a: a[n_p:n_p + n_s]

    attn_p = _prompt_attention(pr(q), pr(k), pr(v), pr(qi), pr(ki), pr(wi))
    rnn_p, h_p = _rglru_prompt(pr(xr), pr(yr), *rnn_w)
    attn_s = _sample_attention(sm(q), sm(k), sm(v), sm(qi), sm(ki), sm(wi), cache_k, cache_v, cache_kidx, page_table)
    rnn_s, h_s = _rglru_sample(sm(xr), sm(yr), state_conv, state_rglru, *rnn_w)

    tail = lambda w: jnp.zeros((n - n_p - n_s, w), bf16)
    attn_o = jnp.concatenate([attn_p.reshape(n_p, -1), attn_s, tail(attn_s.shape[1])])
    rnn_o = jnp.concatenate([rnn_p.reshape(n_p, d_rnn), rnn_s, tail(d_rnn)])
    x1, hn, eidx_t, gate_t = _back(x, attn_o, rnn_o, g_a, g_r, w_o_attn, w_o_rnn, w_out, norm2_g, w_peer_q,
                                   peer_k1, peer_k2)
    y = _peer(hn, x1, eidx_t, gate_t, peer_u, peer_v)

    conv_p = pr(xr)[:, sp - (CONV_W - 1):, :]
    conv_s = jnp.concatenate([state_conv[:, ts:, :], sm(xr).reshape(db, ts, d_rnn)], axis=1)
    heads = lambda a, b_, t_: a.reshape(b_, t_, N_KV_HEADS, HEAD_DIM)
    return (y[:n_p].reshape(bp, sp, d), y[n_p:n_p + n_s].reshape(db, ts, d),
            heads(k[:n_p], bp, sp), heads(v[:n_p], bp, sp), pr(ki),
            conv_p, h_p,
            heads(sm(k), db, ts), heads(sm(v), db, ts), sm(ki).reshape(db, ts, IDX_DIM),
            conv_s, h_s)
```

```python
import functools

import numpy as np
import jax
import jax.numpy as jnp
from jax import lax
from jax.experimental import pallas as pl
from jax.experimental.pallas import tpu as pltpu

f32 = jnp.float32
bf16 = jnp.bfloat16
i32 = jnp.int32

N_HEADS = 16
HEAD_DIM = 64
N_KV_HEADS = 4
KV_GROUP = N_HEADS // N_KV_HEADS
ROT_DIM = HEAD_DIM // 4
ROT_HALF = ROT_DIM // 2
ROPE_THETA = 500000.0
QK_EPS = 1e-6
N_IDX_HEADS = 8
IDX_DIM = 64
TOPK_MAX = 256
PAGE_SIZE = 128
CONV_W = 4
LRU_C = 8.0
PEER_HEADS = 8
PEER_KEY_DIM = 128
N_KEYS = 128
PEER_TOPK = 16
PEER_SLOTS = PEER_HEADS * PEER_TOPK
NORM_EPS = 1e-6

LANE = 128
SUBLANE = 8
PACKED_ROWS = 16
VMEM_LIMIT_BYTES = 56 * 1024 * 1024

INT_MIN = -(2 ** 31)
NEG_BIG = -1e30

TOKEN_TILE = 256
Q_TILE = LANE
KEY_CHUNK = 512
KEY_SUB = 256
SOFTMAX_PARTIALS = 32
V_ROWS = HEAD_DIM + PACKED_ROWS
BOUND_MARGIN = 1.02
MIN_TRUSTED_SUM = 1e-25
RNN_CHUNK = 256
PEER_TOKENS = 64
PAGES_PER_STEP = 8
PEER_X = PEER_SLOTS * PACKED_ROWS


def _params(*sem):
    return pltpu.CompilerParams(dimension_semantics=sem, vmem_limit_bytes=VMEM_LIMIT_BYTES)


def _resident(shape):
    nd = len(shape)
    return pl.BlockSpec(shape, lambda *_: (0,) * nd, pipeline_mode=pl.Buffered(1))


def _gelu(x):
    return 0.5 * x * (1.0 + jnp.tanh(np.sqrt(2.0 / np.pi).astype(np.float32) * (x + 0.044715 * (x * x * x))))


def _dot(a, b):
    return jnp.dot(a, b, preferred_element_type=f32)


def _dot_nt(a, b):
    return lax.dot_general(a, b, (((1,), (1,)), ((), ())), preferred_element_type=f32)


def _front_kernel(x_ref, g1_ref, wa_ref, wb_ref, bg_ref, qg_ref, kg_ref, c_ref, s1_ref, s2_ref, bd_ref,
                  q_ref, k_ref, v_ref, qi_ref, ki_ref, wi_ref, xr_ref, yr_ref, ga_ref, gr_ref,
                  *, d_model, d_rnn):
    x = x_ref[...]
    h = (x * lax.rsqrt(jnp.mean(x * x, axis=-1, keepdims=True) + NORM_EPS) * g1_ref[...]).astype(bf16)
    cos, sin_hi, sin_lo = c_ref[...], s1_ref[...], s2_ref[...]
    bd = bd_ref[...]

    def rope(t):
        return t * cos + pltpu.roll(t, LANE - ROT_HALF, 1) * sin_hi + pltpu.roll(t, ROT_HALF, 1) * sin_lo

    def head_norm(t, g):
        sq = t * t
        hi = sq.astype(bf16)
        lo = (sq - hi.astype(f32)).astype(bf16)
        ss = _dot(hi, bd) + _dot(lo, bd)
        return t * lax.rsqrt(ss * (1.0 / HEAD_DIM) + QK_EPS) * g

    q_w = N_HEADS * HEAD_DIM
    kv_w = N_KV_HEADS * HEAD_DIM
    qi_w = N_IDX_HEADS * IDX_DIM
    pa = _dot(h, wa_ref[...])
    off = 0
    for j in range(q_w // LANE):
        t = pa[:, off + j * LANE: off + (j + 1) * LANE]
        q_ref[:, j * LANE:(j + 1) * LANE] = (rope(head_norm(t, qg_ref[...])) * (HEAD_DIM ** -0.5)).astype(bf16)
    off += q_w
    for j in range(kv_w // LANE):
        t = pa[:, off + j * LANE: off + (j + 1) * LANE]
        k_ref[:, j * LANE:(j + 1) * LANE] = rope(head_norm(t, kg_ref[...]))
    off += kv_w
    v_ref[...] = pa[:, off:off + kv_w]
    off += kv_w
    for j in range(qi_w // LANE):
        t = pa[:, off + j * LANE: off + (j + 1) * LANE]
        qi_ref[:, j * LANE:(j + 1) * LANE] = rope(t).astype(bf16)
    off += qi_w
    ki_ref[...] = rope(pa[:, off:off + LANE])[:, :IDX_DIM]
    off += LANE
    wi_ref[...] = pa[:, off:off + LANE] * (N_IDX_HEADS ** -0.5)

    xr_ref[...] = _dot(h, wb_ref[:, 0:d_rnn])
    yr_ref[...] = _dot(h, wb_ref[:, d_rnn:2 * d_rnn])
    o = 2 * d_rnn
    ga_ref[...] = jax.nn.sigmoid(_dot(h, wb_ref[:, o:o + d_model]) + bg_ref[:, 0:d_model])
    gr_ref[...] = jax.nn.sigmoid(_dot(h, wb_ref[:, o + d_model:o + 2 * d_model]) + bg_ref[:, d_model:2 * d_model])


def _rope_tables(pos):
    inv = ROPE_THETA ** (-jnp.arange(ROT_HALF, dtype=f32) / ROT_HALF)
    ang = pos.astype(f32)[:, None] * inv[None, :]
    c, s = jnp.cos(ang), jnp.sin(ang)
    n = pos.shape[0]
    z = lambda w: jnp.zeros((n, w), f32)
    cos = jnp.concatenate([c, c, jnp.ones((n, HEAD_DIM - ROT_DIM), f32)], axis=1)
    s_hi = jnp.concatenate([-s, z(HEAD_DIM - ROT_HALF)], axis=1)
    s_lo = jnp.concatenate([z(ROT_HALF), s, z(HEAD_DIM - ROT_DIM)], axis=1)
    rep = LANE // HEAD_DIM
    return jnp.tile(cos, (1, rep)), jnp.tile(s_hi, (1, rep)), jnp.tile(s_lo, (1, rep))


def _front(x, pos, norm1_g, w_in, b_gates, q_norm_g, k_norm_g, d_rnn):
    n, d_model = x.shape
    q_w, kv_w, qi_w = N_HEADS * HEAD_DIM, N_KV_HEADS * HEAD_DIM, N_IDX_HEADS * IDX_DIM
    sizes = [q_w, kv_w, kv_w, qi_w, IDX_DIM, N_IDX_HEADS, d_rnn, d_rnn, d_model, d_model]
    offs = np.cumsum([0] + sizes)
    parts = [w_in[:, offs[i]:offs[i + 1]] for i in range(len(sizes))]
    padl = lambda w: jnp.pad(w, ((0, 0), (0, LANE - w.shape[1])))
    wa = jnp.concatenate(parts[:4] + [padl(parts[4]), padl(parts[5])], axis=1).astype(bf16)
    wb = jnp.concatenate(parts[6:], axis=1).astype(bf16)
    cos, s_hi, s_lo = _rope_tables(pos)
    rep = LANE // HEAD_DIM
    head_of_lane = np.arange(LANE) // HEAD_DIM
    bd = jnp.asarray(head_of_lane[:, None] == head_of_lane[None, :], bf16)
    tm = TOKEN_TILE
    row = lambda w: pl.BlockSpec((tm, w), lambda i: (i, 0))
    outs = [(q_w, bf16), (kv_w, f32), (kv_w, f32), (qi_w, bf16), (IDX_DIM, f32), (LANE, f32),
            (d_rnn, f32), (d_rnn, f32), (d_model, f32), (d_model, f32)]
    return pl.pallas_call(
        functools.partial(_front_kernel, d_model=d_model, d_rnn=d_rnn),
        grid=(n // tm,),
        in_specs=[row(d_model), _resident((1, d_model)), _resident(wa.shape), _resident(wb.shape),
                  _resident((1, 2 * d_model)), _resident((1, LANE)), _resident((1, LANE)),
                  row(LANE), row(LANE), row(LANE), _resident((LANE, LANE))],
        out_specs=[row(w) for w, _ in outs],
        out_shape=[jax.ShapeDtypeStruct((n, w), dt) for w, dt in outs],
        compiler_params=_params("parallel"),
        name="front",
    )(x, norm1_g.reshape(1, -1), wa, wb, b_gates.reshape(1, -1),
      jnp.tile(q_norm_g, rep).reshape(1, LANE), jnp.tile(k_norm_g, rep).reshape(1, LANE),
      cos, s_hi, s_lo, bd)


def _score_key(score):
    score = jnp.where(score == 0.0, 0.0, score)
    bits = pltpu.bitcast(score, i32)
    return bits ^ ((bits >> 31) & 0x7FFFFFFF)


def _kth_largest_key(count_ge, k, like):
    n_keys = count_ge(jnp.full_like(like, INT_MIN + 1))
    c0 = count_ge(jnp.zeros_like(like))
    ans = jnp.where(c0 >= k, 0, INT_MIN).astype(i32)
    cnt = jnp.where(c0 >= k, c0, n_keys)

    def unsettled(st):
        i, _, cnt = st
        return (i < 31) & (jnp.max(jnp.where((cnt == k) | (n_keys < k), 0, 1)) > 0)

    def body(st):
        i, ans, cnt = st
        cand = ans + lax.shift_left(jnp.int32(1), jnp.int32(30) - i)
        c = count_ge(cand)
        return i + 1, jnp.where(c >= k, cand, ans), jnp.where(c >= k, c, cnt)

    _, ans, cnt = lax.while_loop(unsettled, body, (jnp.int32(0), ans, cnt))
    return jnp.maximum(ans, INT_MIN + 1), cnt


def _prompt_attn_kernel(qt_ref, qit_ref, wi_ref, ki_ref, k_ref, vt_ref, low_ref, o_ref, keys_ref, bias_ref,
                        m_ref, top_ref, kabs_ref, acc_ref, *, top_k):
    qb = pl.program_id(1)
    tq, ck, sub = Q_TILE, KEY_CHUNK, KEY_SUB
    n_chunks = (qb * tq + tq + ck - 1) // ck
    qpos = qb * tq + lax.broadcasted_iota(i32, (sub, tq), 1)
    krow = lax.broadcasted_iota(i32, (sub, tq), 0)
    w = wi_ref[0, 0]

    def score_chunk(c, carry):
        for u in range(ck // sub):
            kc = ki_ref[0, c, u * sub:(u + 1) * sub, :]
            acc = jnp.zeros((sub, tq), f32)
            for h in range(N_IDX_HEADS):
                acc = acc + jnp.maximum(_dot(kc, qit_ref[0, 0, h]), 0.0) * w[h:h + 1, :]
            causal = c * ck + u * sub + krow <= qpos
            keys_ref[c, u * sub:(u + 1) * sub, :] = jnp.where(causal, _score_key(acc), INT_MIN)
        return carry

    lax.fori_loop(0, n_chunks, score_chunk, 0)

    def count_ge(cand):
        lanes_of_acc = 8 * SUBLANE
        def body(c, acc):
            m = (keys_ref[c] >= cand).astype(f32)
            return acc + jnp.sum(m.reshape(ck // lanes_of_acc, lanes_of_acc, tq), axis=0)
        acc = lax.fori_loop(0, n_chunks, body, jnp.zeros((lanes_of_acc, tq), f32))
        return jnp.sum(acc, axis=0, keepdims=True)

    thr, n_ge = _kth_largest_key(count_ge, float(top_k), jnp.zeros((1, tq), i32))


    @pl.when(jnp.max(n_ge) > top_k)
    def _():
        need = top_k - count_ge(thr + 1)

        def body(c, before):
            kc = keys_ref[c]
            eq = kc == thr
            eqb = eq.astype(bf16)
            rank = _dot(low_ref[...], eqb) + before
            keys_ref[c] = jnp.where(eq & (rank >= need), INT_MIN, kc)
            return before + jnp.sum(eqb.astype(f32), axis=0, keepdims=True)

        lax.fori_loop(0, n_chunks, body, jnp.zeros((1, tq), f32))

    def to_bias(c, carry):
        bias_ref[c] = jnp.where(keys_ref[c] >= thr, 0.0, NEG_BIG)
        return carry

    lax.fori_loop(0, n_chunks, to_bias, 0)

    @pl.when(qb == 0)
    def _():
        for g in range(N_KV_HEADS):
            def body(c, best, g=g):
                a = jnp.abs(k_ref[0, g, c].astype(f32))
                return jnp.maximum(best, jnp.max(a.reshape(ck // SUBLANE, SUBLANE, HEAD_DIM), axis=0))
            best = lax.fori_loop(0, k_ref.shape[2], body, jnp.zeros((SUBLANE, HEAD_DIM), f32))
            kabs_ref[g] = jnp.broadcast_to(jnp.max(best, axis=0, keepdims=True), (SUBLANE, HEAD_DIM))

    def sweep():
        acc_ref[...] = jnp.zeros(acc_ref.shape, f32)

        def accumulate(c, carry):
            bias = bias_ref[c]
            for g in range(N_KV_HEADS):
                s4 = _dot(k_ref[0, g, c], qt_ref[0, 0, g])
                ps = [jnp.exp(s4[:, r * tq:(r + 1) * tq] + bias - top_ref[g * KV_GROUP + r]).astype(bf16)
                      for r in range(KV_GROUP)]
                acc_ref[g] = acc_ref[g] + _dot(vt_ref[0, g, c], jnp.concatenate(ps, axis=1))
            return carry

        lax.fori_loop(0, n_chunks, accumulate, 0)

    for g in range(N_KV_HEADS):
        bound = _dot(kabs_ref[g].astype(bf16), jnp.abs(qt_ref[0, 0, g]))[0:1, :] * BOUND_MARGIN
        for r in range(KV_GROUP):
            top_ref[g * KV_GROUP + r] = bound[:, r * tq:(r + 1) * tq]
    sweep()
    smallest = acc_ref[0, HEAD_DIM:HEAD_DIM + 1, :]
    for g in range(1, N_KV_HEADS):
        smallest = jnp.minimum(smallest, acc_ref[g, HEAD_DIM:HEAD_DIM + 1, :])

    @pl.when(jnp.min(smallest) < MIN_TRUSTED_SUM)
    def _():
        part = m_ref.shape[1]
        m_ref[...] = jnp.full(m_ref.shape, NEG_BIG, f32)

        def col_max(c, carry):
            bias = bias_ref[c]
            for g in range(N_KV_HEADS):
                s4 = _dot(k_ref[0, g, c], qt_ref[0, 0, g])
                for r in range(KV_GROUP):
                    h = g * KV_GROUP + r
                    s = (s4[:, r * tq:(r + 1) * tq] + bias).reshape(ck // part, part, tq)
                    m_ref[h] = jnp.maximum(m_ref[h], jnp.max(s, axis=0))
            return carry

        lax.fori_loop(0, n_chunks, col_max, 0)
        for h in range(N_HEADS):
            top_ref[h] = jnp.max(m_ref[h], axis=0, keepdims=True)
        sweep()

    for g in range(N_KV_HEADS):
        o_ref[0, 0, g] = (acc_ref[g, 0:HEAD_DIM, :] / acc_ref[g, HEAD_DIM:HEAD_DIM + 1, :]).astype(o_ref.dtype)


def _prompt_attention(q, k, v, qi, ki, wi):
    b, s, _ = q.shape
    top_k = min(TOPK_MAX, s // 4)
    tq, ck = Q_TILE, KEY_CHUNK
    nq, nc = s // tq, s // ck
    gq = KV_GROUP * tq
    qt = q.reshape(b, nq, tq, N_KV_HEADS, KV_GROUP, HEAD_DIM).transpose(0, 1, 3, 5, 4, 2).reshape(b, nq, N_KV_HEADS, HEAD_DIM, gq)
    qit = qi.reshape(b, nq, tq, N_IDX_HEADS, IDX_DIM).transpose(0, 1, 3, 4, 2)
    wit = (wi * (IDX_DIM ** -0.5)).reshape(b, nq, tq, N_IDX_HEADS).transpose(0, 1, 3, 2)
    kic = ki.astype(bf16).reshape(b, nc, ck, IDX_DIM)
    kc = k.astype(bf16).reshape(b, nc, ck, N_KV_HEADS, HEAD_DIM).transpose(0, 3, 1, 2, 4)
    vt = v.astype(bf16).reshape(b, nc, ck, N_KV_HEADS, HEAD_DIM).transpose(0, 3, 1, 4, 2)
    extra = jnp.zeros((b, N_KV_HEADS, nc, V_ROWS - HEAD_DIM, ck), bf16).at[:, :, :, 0, :].set(1.0)
    vt = jnp.concatenate([vt, extra], axis=3)
    low =jnp.asarray(np.arange(ck)[:, None] > np.arange(ck)[None, :], bf16)
    out = pl.pallas_call(
        functools.partial(_prompt_attn_kernel, top_k=top_k),
        grid=(b, nq),
        in_specs=[
            pl.BlockSpec((1, 1, N_KV_HEADS, HEAD_DIM, gq), lambda i, j: (i, j, 0, 0, 0)),
            pl.BlockSpec((1, 1, N_IDX_HEADS, IDX_DIM, tq), lambda i, j: (i, j, 0, 0, 0)),
            pl.BlockSpec((1, 1, N_IDX_HEADS, tq), lambda i, j: (i, j, 0, 0)),
            pl.BlockSpec((1, nc, ck, IDX_DIM), lambda i, j: (i, 0, 0, 0)),
            pl.BlockSpec((1, N_KV_HEADS, nc, ck, HEAD_DIM), lambda i, j: (i, 0, 0, 0, 0)),
            pl.BlockSpec((1, N_KV_HEADS, nc, V_ROWS, ck), lambda i, j: (i, 0, 0, 0, 0)),
            _resident((ck, ck)),
        ],
        out_specs=pl.BlockSpec((1, 1, N_KV_HEADS, HEAD_DIM, gq), lambda i, j: (i, j, 0, 0, 0)),
        out_shape=jax.ShapeDtypeStruct((b, nq, N_KV_HEADS, HEAD_DIM, gq), bf16),
        scratch_shapes=[pltpu.VMEM((nc, ck, tq), i32), pltpu.VMEM((nc, ck, tq), f32),
                        pltpu.VMEM((N_HEADS, SOFTMAX_PARTIALS, tq), f32), pltpu.VMEM((N_HEADS, 1, tq), f32),
                        pltpu.VMEM((N_KV_HEADS, SUBLANE, HEAD_DIM), f32),
                        pltpu.VMEM((N_KV_HEADS, V_ROWS, gq), f32)],
        compiler_params=_params("parallel", "arbitrary"),
        name="prompt_attention",
    )(qt, qit, wit, kic, kc, vt, low)
    out = out.reshape(b, nq, N_KV_HEADS, HEAD_DIM, KV_GROUP, tq).transpose(0, 1, 5, 2, 4, 3)
    return out.reshape(b, s, N_HEADS * HEAD_DIM)


def _sample_select_kernel(pt_ref, qi_ref, wi_ref, kin_ref, tri_ref, lt_ref, *rest, top_k, n_pages):
    page_refs, (sel_ref, seln_ref, keys_ref) = rest[:PAGES_PER_STEP], rest[PAGES_PER_STEP:]
    j = pl.program_id(1)
    qi = qi_ref[0]
    w = wi_ref[0]

    for i in range(PAGES_PER_STEP):
        s = _dot_nt(qi, page_refs[i][0].astype(bf16))
        score = jnp.sum(jnp.maximum(s, 0.0) * w, axis=0, keepdims=True)
        keys_ref[pl.ds(j * PAGES_PER_STEP + i, 1), :] = _score_key(score)

    @pl.when(j == pl.num_programs(1) - 1)
    def _():
        kn = kin_ref[0].astype(bf16).astype(f32)
        s_new = jnp.sum(qi.astype(f32) * kn, axis=1, keepdims=True)
        key_new = _score_key(jnp.sum(jnp.maximum(s_new, 0.0) * w, axis=0, keepdims=True))
        keys = keys_ref[...]

        def count_ge(cand):
            c = jnp.sum((keys >= cand).astype(f32), axis=1, keepdims=True)
            return jnp.sum(c, axis=0, keepdims=True) + (key_new >= cand).astype(f32)

        thr, _ = _kth_largest_key(count_ge, float(top_k), jnp.zeros((1, 1), i32))
        need = top_k - (count_ge(thr + 1))
        eq = keys == thr
        eqb = eq.astype(bf16)
        in_row = _dot(eqb, tri_ref[...])
        per_page = jnp.sum(eqb.astype(f32), axis=1, keepdims=True)
        before = _dot(lt_ref[...], per_page.astype(bf16) * jnp.ones((1, LANE), bf16))[:, 0:1]
        rank = in_row + before
        sel = (keys > thr) | (eq & (rank < need))
        sel_ref[0] = sel.astype(f32)
        n_eq = jnp.sum(per_page, axis=0, keepdims=True)
        sel_new = (key_new > thr) | ((key_new == thr) & (n_eq < need))
        seln_ref[0] = jnp.broadcast_to(sel_new.astype(f32), (1, LANE))


def _sample_attend_kernel(pt_ref, q_ref, sel_ref, seln_ref, kn_ref, vn_ref, gm_ref, *rest, n_pages):
    k_refs = rest[:PAGES_PER_STEP]
    v_refs = rest[PAGES_PER_STEP:2 * PAGES_PER_STEP]
    o_ref, m_ref, l_ref, acc_ref = rest[2 * PAGES_PER_STEP:]
    j = pl.program_id(1)
    q = q_ref[0]

    @pl.when(j == 0)
    def _():
        m_ref[...] = jnp.full(m_ref.shape, NEG_BIG, f32)
        l_ref[...] = jnp.zeros(l_ref.shape, f32)
        acc_ref[...] = jnp.zeros(acc_ref.shape, f32)

    def update(s, vals):
        m = m_ref[...]
        m_new = jnp.maximum(m, jnp.max(s, axis=1, keepdims=True))
        alpha = jnp.exp(m - m_new)
        p = jnp.exp(s - m_new)
        l_ref[...] = alpha * l_ref[...] + jnp.sum(p, axis=1, keepdims=True)
        acc_ref[...] = alpha * acc_ref[...] + vals(p)
        m_ref[...] = m_new

    for i in range(PAGES_PER_STEP):
        sel = sel_ref[0, pl.ds(j * PAGES_PER_STEP + i, 1), :] > 0.5
        s = jnp.where(sel, _dot_nt(q, k_refs[i][0].astype(bf16)), NEG_BIG)
        update(s, lambda p, i=i: _dot(p.astype(bf16), v_refs[i][0].astype(bf16)))

    @pl.when(j == pl.num_programs(1) - 1)
    def _():
        kn = kn_ref[0].astype(bf16).astype(f32)
        vn = vn_ref[0].astype(bf16).astype(f32)
        s = jnp.sum(q.astype(f32) * kn, axis=1, keepdims=True)
        s = jnp.where(seln_ref[0][:, 0:1] > 0.5, s, NEG_BIG)
        update(s, lambda p: p.astype(bf16).astype(f32) * vn)
        full = acc_ref[...] / l_ref[...] * gm_ref[...]
        out = full[:, 0:HEAD_DIM]
        for g in range(1, N_KV_HEADS):
            out = out + full[:, g * HEAD_DIM:(g + 1) * HEAD_DIM]
        o_ref[0] = out.astype(o_ref.dtype)


def _sample_attention(q, k_new, v_new, qi, ki_new, wi, cache_k, cache_v, cache_kidx, page_table):
    db = q.shape[0]
    n_pages = page_table.shape[1]
    past = n_pages * PAGE_SIZE
    top_k = min(TOPK_MAX, (past + 1) // 4)
    pps = PAGES_PER_STEP
    steps = n_pages // pps
    kv_w = N_KV_HEADS * HEAD_DIM
    n_pool = cache_k.shape[0]
    ck2 = cache_k.reshape(n_pool, PAGE_SIZE, kv_w)
    cv2 = cache_v.reshape(n_pool, PAGE_SIZE, kv_w)

    def page_spec(width, i):
        return pl.BlockSpec((1, PAGE_SIZE, width), lambda b, j, pt, i=i: (pt[b, j * pps + i], 0, 0))

    per_seq = lambda *shape: pl.BlockSpec((1,) + shape, lambda b, j, pt: (b,) + (0,) * len(shape))
    const = lambda *shape: pl.BlockSpec(shape, lambda b, j, pt: (0,) * len(shape))
    tri = jnp.asarray(np.arange(PAGE_SIZE)[:, None] < np.arange(PAGE_SIZE)[None, :], bf16)
    lower = jnp.asarray(np.arange(n_pages)[:, None] > np.arange(n_pages)[None, :], bf16)

    sel, sel_new = pl.pallas_call(
        functools.partial(_sample_select_kernel, top_k=top_k, n_pages=n_pages),
        grid_spec=pltpu.PrefetchScalarGridSpec(
            num_scalar_prefetch=1,
            grid=(db, steps),
            in_specs=[per_seq(N_IDX_HEADS, IDX_DIM), per_seq(N_IDX_HEADS, 1), per_seq(1, IDX_DIM),
                      const(PAGE_SIZE, PAGE_SIZE), const(n_pages, n_pages)]
                     + [page_spec(IDX_DIM, i) for i in range(pps)],
            out_specs=[per_seq(n_pages, PAGE_SIZE), per_seq(1, LANE)],
            scratch_shapes=[pltpu.VMEM((n_pages, PAGE_SIZE), i32)],
        ),
        out_shape=[jax.ShapeDtypeStruct((db, n_pages, PAGE_SIZE), f32), jax.ShapeDtypeStruct((db, 1, LANE), f32)],
        compiler_params=_params("parallel", "arbitrary"),
        name="sample_select",
    )(page_table, qi.reshape(db, N_IDX_HEADS, IDX_DIM), (wi * (IDX_DIM ** -0.5)).reshape(db, N_IDX_HEADS, 1),
      ki_new.reshape(db, 1, IDX_DIM), tri, lower, *([cache_kidx] * pps))

    group_mask = np.repeat(np.arange(N_HEADS)[:, None] // KV_GROUP == np.arange(N_KV_HEADS)[None, :], HEAD_DIM, axis=1)
    q_bd = jnp.where(group_mask[None], jnp.tile(q.reshape(db, N_HEADS, HEAD_DIM), (1, 1, N_KV_HEADS)), 0).astype(bf16)
    out = pl.pallas_call(
        functools.partial(_sample_attend_kernel, n_pages=n_pages),
        grid_spec=pltpu.PrefetchScalarGridSpec(
            num_scalar_prefetch=1,
            grid=(db, steps),
            in_specs=[per_seq(N_HEADS, kv_w), per_seq(n_pages, PAGE_SIZE), per_seq(1, LANE),
                      per_seq(1, kv_w), per_seq(1, kv_w), const(N_HEADS, kv_w)]
                     + [page_spec(kv_w, i) for i in range(pps)] * 2,
            out_specs=per_seq(N_HEADS, HEAD_DIM),
            scratch_shapes=[pltpu.VMEM((N_HEADS, 1), f32), pltpu.VMEM((N_HEADS, 1), f32), pltpu.VMEM((N_HEADS, kv_w), f32)],
        ),
        out_shape=jax.ShapeDtypeStruct((db, N_HEADS, HEAD_DIM), bf16),
        compiler_params=_params("parallel", "arbitrary"),
        name="sample_attend",
    )(page_table, q_bd, sel, sel_new, k_new.reshape(db, 1, kv_w), v_new.reshape(db, 1, kv_w),
      jnp.asarray(group_mask, f32), *([ck2] * pps), *([cv2] * pps))
    return out.reshape(db, N_HEADS * HEAD_DIM)


def _rglru_coeffs(xc, wr_ref, br_ref, wi_ref, bi_ref, lam_ref):
    xb = xc.astype(bf16)
    r = jax.nn.sigmoid(_dot(xb, wr_ref[...]) + br_ref[...])
    i = jax.nn.sigmoid(_dot(xb, wi_ref[...]) + bi_ref[...])
    z = -lam_ref[...]
    softplus = jnp.maximum(z, 0.0) + jnp.log1p(jnp.exp(-jnp.abs(z)))
    log_a = -LRU_C * r * softplus
    a = jnp.exp(log_a)
    th = jnp.tanh(log_a)
    b = jnp.sqrt(-2.0 * th / (1.0 - th)) * (i * xc)
    return a, b


def _rglru_seq_kernel(xr_ref, yr_ref, cw_ref, cb_ref, wr_ref, br_ref, wi_ref, bi_ref, lam_ref,
                      o_ref, hl_ref, xprev_ref, h_ref, a_ref, b_ref):
    t = pl.program_id(1)
    tc, d = xr_ref.shape[1], xr_ref.shape[2]

    @pl.when(t == 0)
    def _():
        xprev_ref[...] = jnp.zeros(xprev_ref.shape, f32)
        h_ref[...] = jnp.zeros(h_ref.shape, f32)

    xr = xr_ref[0]
    xcat = jnp.concatenate([xprev_ref[...], xr], axis=0)
    xc = cb_ref[...] + cw_ref[0:1, :] * xcat[SUBLANE - 3:SUBLANE - 3 + tc]
    for j in range(1, CONV_W):
        xc = xc + cw_ref[j:j + 1, :] * xcat[SUBLANE - 3 + j:SUBLANE - 3 + j + tc]
    xprev_ref[...] = xr[tc - SUBLANE:, :]
    a, b = _rglru_coeffs(xc, wr_ref, br_ref, wi_ref, bi_ref, lam_ref)
    a_ref[...] = a
    b_ref[...] = b
    row = lax.broadcasted_iota(i32, (SUBLANE, d), 0)

    def slab(s, h_prev):
        r0 = pl.multiple_of(s * SUBLANE, SUBLANE)
        a8 = a_ref[pl.ds(r0, SUBLANE), :]
        b8 = b_ref[pl.ds(r0, SUBLANE), :]
        for sh in (1, 2, 4):
            keep = row >= sh
            b8 = jnp.where(keep, a8 * pltpu.roll(b8, sh, 0) + b8, b8)
            a8 = jnp.where(keep, a8 * pltpu.roll(a8, sh, 0), a8)
        h8 = a8 * h_prev + b8
        o_ref[0, pl.ds(r0, SUBLANE), :] = (h8 * _gelu(yr_ref[0, pl.ds(r0, SUBLANE), :])).astype(o_ref.dtype)
        return h8[SUBLANE - 1:SUBLANE, :]

    h_last = lax.fori_loop(0, tc // SUBLANE, slab, h_ref[...])
    h_ref[...] = h_last
    hl_ref[0] = h_last


def _block_diag(w):
    n, a, b = w.shape
    eye = jnp.eye(n, dtype=w.dtype)
    return (eye[:, None, :, None] * w[:, :, None, :]).reshape(n * a, n * b).astype(bf16)


def _rglru_prompt(xr, yr, conv_w, conv_b, w_rgate, b_rgate, w_igate, b_igate, lru_lambda):
    b, t, d = xr.shape
    tc = RNN_CHUNK
    vec = lambda v: v.reshape(1, d)
    seq = pl.BlockSpec((1, tc, d), lambda i, j: (i, j, 0))
    out, h_last = pl.pallas_call(
        _rglru_seq_kernel,
        grid=(b, t // tc),
        in_specs=[seq, seq, _resident((CONV_W, d)), _resident((1, d)), _resident((d, d)), _resident((1, d)),
                  _resident((d, d)), _resident((1, d)), _resident((1, d))],
        out_specs=[seq, pl.BlockSpec((1, 1, d), lambda i, j: (i, 0, 0))],
        out_shape=[jax.ShapeDtypeStruct((b, t, d), bf16), jax.ShapeDtypeStruct((b, 1, d), f32)],
        scratch_shapes=[pltpu.VMEM((SUBLANE, d), f32), pltpu.VMEM((1, d), f32),
                        pltpu.VMEM((tc, d), f32), pltpu.VMEM((tc, d), f32)],
        compiler_params=_params("parallel", "arbitrary"),
        name="rglru_prompt",
    )(xr, yr, conv_w, vec(conv_b), _block_diag(w_rgate), vec(b_rgate), _block_diag(w_igate), vec(b_igate),
      vec(lru_lambda))
    return out, h_last.reshape(b, d)


def _rglru_step_kernel(xr_ref, yr_ref, c0_ref, c1_ref, c2_ref, h0_ref, cw_ref, cb_ref, wr_ref, br_ref, wi_ref,
                       bi_ref, lam_ref, o_ref, h_ref):
    xc = cb_ref[...] + cw_ref[0:1, :] * c0_ref[...]
    xc = xc + cw_ref[1:2, :] * c1_ref[...]
    xc = xc + cw_ref[2:3, :] * c2_ref[...]
    xc = xc + cw_ref[3:4, :] * xr_ref[...]
    a, b = _rglru_coeffs(xc, wr_ref, br_ref, wi_ref, bi_ref, lam_ref)
    h = a * h0_ref[...] + b
    h_ref[...] = h
    o_ref[...] = (h * _gelu(yr_ref[...])).astype(o_ref.dtype)


def _rglru_sample(xr, yr, state_conv, h0, conv_w, conv_b, w_rgate, b_rgate, w_igate, b_igate, lru_lambda):
    db, d = xr.shape
    vec = lambda v: v.reshape(1, d)
    return pl.pallas_call(
        _rglru_step_kernel,
        out_shape=[jax.ShapeDtypeStruct((db, d), bf16), jax.ShapeDtypeStruct((db, d), f32)],
        compiler_params=pltpu.CompilerParams(vmem_limit_bytes=VMEM_LIMIT_BYTES),
        name="rglru_sample",
    )(xr, yr, state_conv[:, 0], state_conv[:, 1], state_conv[:, 2], h0, conv_w, vec(conv_b),
      _block_diag(w_rgate), vec(b_rgate), _block_diag(w_igate), vec(b_igate), vec(lru_lambda))


def _top_rows(x, k, payload=None):
    n = x.shape[0]
    row = lax.broadcasted_iota(i32, x.shape, 0)
    vals, picked = [], []
    for _ in range(k):
        m = jnp.max(x, axis=0, keepdims=True)
        r = jnp.min(jnp.where(x == m, row, n), axis=0, keepdims=True)
        hit = row == r
        vals.append(m)
        picked.append(r if payload is None else jnp.sum(jnp.where(hit, payload, 0), axis=0, keepdims=True))
        x = jnp.where(hit, -jnp.inf, x)
    return jnp.concatenate(vals, axis=0), jnp.concatenate(picked, axis=0)


_PAIR_COUNTS = [PEER_TOPK // (r1 + 1) for r1 in range(PEER_TOPK)]


def _back_kernel(x_ref, ao_ref, ro_ref, ga_ref, gr_ref, woa_ref, wor_ref, wout_ref, g2_ref, wqt_ref, k1_ref, k2_ref,
                 x1_ref, hn_ref, eidx_ref, gate_ref):
    mixed = ga_ref[...] * _dot(ao_ref[...], woa_ref[...]) + gr_ref[...] * _dot(ro_ref[...], wor_ref[...])
    x1 = x_ref[...] + _dot(mixed.astype(bf16), wout_ref[...])
    x1_ref[...] = x1
    hn = (x1 * lax.rsqrt(jnp.mean(x1 * x1, axis=-1, keepdims=True) + NORM_EPS) * g2_ref[...]).astype(bf16)
    hn_ref[...] = hn
    qt = _dot_nt(wqt_ref[...], hn).astype(bf16)
    half = PEER_KEY_DIM // 2
    for h in range(PEER_HEADS):
        base = h * PEER_KEY_DIM
        v1, i1 = _top_rows(_dot(k1_ref[...], qt[base:base + half, :]), PEER_TOPK)
        v2, i2 = _top_rows(_dot(k2_ref[...], qt[base + half:base + PEER_KEY_DIM, :]), PEER_TOPK)
        cand, cid = [], []
        for r1, n2 in enumerate(_PAIR_COUNTS):
            cand.append(v1[r1:r1 + 1, :] + v2[0:n2, :])
            cid.append(i1[r1:r1 + 1, :] * N_KEYS + i2[0:n2, :])
        pad = -sum(_PAIR_COUNTS) % SUBLANE
        tm = v1.shape[1]
        cand.append(jnp.full((pad, tm), -jnp.inf, f32))
        cid.append(jnp.zeros((pad, tm), i32))
        sv, eidx = _top_rows(jnp.concatenate(cand, axis=0), PEER_TOPK, jnp.concatenate(cid, axis=0))
        p = jnp.exp(sv - sv[0:1, :])
        eidx_ref[h * PEER_TOPK:(h + 1) * PEER_TOPK, :] = eidx
        gate_ref[h * PEER_TOPK:(h + 1) * PEER_TOPK, :] = p / jnp.sum(p, axis=0, keepdims=True)


def _back(x, attn_o, rnn_o, g_a, g_r, w_o_attn, w_o_rnn, w_out, norm2_g, w_peer_q, peer_k1, peer_k2):
    n, d = x.shape
    tm = TOKEN_TILE
    row = lambda w: pl.BlockSpec((tm, w), lambda i: (i, 0))
    col = pl.BlockSpec((PEER_SLOTS, tm), lambda i: (0, i))
    ws = [w_o_attn.astype(bf16), w_o_rnn.astype(bf16), w_out.astype(bf16), norm2_g.reshape(1, d),
          w_peer_q.T.astype(bf16), peer_k1.astype(bf16), peer_k2.astype(bf16)]
    return pl.pallas_call(
        _back_kernel,
        grid=(n // tm,),
        in_specs=[row(d), row(attn_o.shape[1]), row(rnn_o.shape[1]), row(d), row(d)] + [_resident(w.shape) for w in ws],
        out_specs=[row(d), row(d), col, col],
        out_shape=[jax.ShapeDtypeStruct((n, d), f32), jax.ShapeDtypeStruct((n, d), bf16),
                   jax.ShapeDtypeStruct((PEER_SLOTS, n), i32), jax.ShapeDtypeStruct((PEER_SLOTS, n), f32)],
        compiler_params=_params("parallel"),
        name="back",
    )(x, attn_o, rnn_o, g_a, g_r, *ws)


def _pack_rows(table, reverse_pieces=False):
    e, d = table.shape
    bits = lax.bitcast_convert_type(table.astype(bf16), jnp.uint16).astype(jnp.uint32).reshape(e // 2, 2, d // LANE, LANE)
    if reverse_pieces:
        bits = bits[:, :, ::-1, :]
    return lax.bitcast_convert_type(bits[:, 0] | (bits[:, 1] << 16), i32).reshape(e // 2 * (d // LANE), LANE)


def _gather_tiles(table_ref, row_ref, t):
    starts = row_ref.at[t]
    tiles = [table_ref[pl.ds(pl.multiple_of(starts[s], SUBLANE), SUBLANE), :] for s in range(PEER_SLOTS)]
    return pltpu.bitcast(jnp.concatenate(tiles, axis=0), bf16)


def _peer_act_kernel(start_ref, hn_ref, gate_ref, u_ref, coef_ref):
    row = lax.broadcasted_iota(i32, (SUBLANE, LANE), 0)
    n_blk = PEER_X // LANE

    def token_group(gi, carry):
        t0 = pl.multiple_of(gi * SUBLANE, SUBLANE)
        acts = [jnp.zeros((SUBLANE, LANE), f32) for _ in range(n_blk)]
        for j in range(SUBLANE):
            rows = _gather_tiles(u_ref, start_ref, t0 + j)
            r = _dot_nt(hn_ref[t0 + j], rows)
            for blk in range(n_blk):
                z = pltpu.roll(r[:, blk * LANE:(blk + 1) * LANE], 0, 1, stride=2, stride_axis=0)
                for sh in (4, 2, 1):
                    z = z + pltpu.roll(z, sh, 0)
                acts[blk] = jnp.where(row == j, z, acts[blk])
        for blk in range(n_blk):
            cols = slice(blk * LANE, (blk + 1) * LANE)
            coef = gate_ref[pl.ds(t0, SUBLANE), cols] * _gelu(acts[blk])
            coef_ref[pl.ds(t0, SUBLANE), cols] = pltpu.roll(coef, LANE - (PACKED_ROWS - 2), 1)
        return carry

    lax.fori_loop(0, hn_ref.shape[0] // SUBLANE, token_group, 0)


def _peer_out_kernel(start_ref, coef_ref, x1_ref, v_ref, y_ref):
    n_blk = PEER_X // LANE

    def token_group(gi, carry):
        t0 = pl.multiple_of(gi * SUBLANE, SUBLANE)
        coef = coef_ref[pl.ds(t0, SUBLANE), :]
        for j in range(SUBLANE):
            rows = _gather_tiles(v_ref, start_ref, t0 + j)
            pick = [pltpu.roll(jnp.broadcast_to(coef[j:j + 1, blk * LANE:(blk + 1) * LANE], (SUBLANE, LANE)),
                               0, 1, stride=2, stride_axis=0) for blk in range(n_blk)]
            y_ref[t0 + j] = x1_ref[t0 + j] + _dot(jnp.concatenate(pick, axis=1).astype(bf16), rows)
        return carry

    lax.fori_loop(0, x1_ref.shape[0] // SUBLANE, token_group, 0)


def _peer(hn, x1, eidx_t, gate_t, peer_u, peer_v):
    n, d = x1.shape
    tt = PEER_TOKENS
    rows = d // LANE
    eidx = eidx_t.T
    start = (eidx >> 1) * SUBLANE
    col = jnp.tile(jnp.arange(PACKED_ROWS, dtype=i32), PEER_SLOTS)[None, :]
    gate = jnp.where(col == PACKED_ROWS - 2 + jnp.repeat(eidx & 1, PACKED_ROWS, axis=1),
                     jnp.repeat(gate_t.T, PACKED_ROWS, axis=1), 0.0)
    smem = pl.BlockSpec((tt, PEER_SLOTS), lambda i: (i, 0), memory_space=pltpu.SMEM)
    tiles = pl.BlockSpec((tt, rows, LANE), lambda i: (i, 0, 0))
    wide = pl.BlockSpec((tt, PEER_X), lambda i: (i, 0))
    table = _resident((peer_u.shape[0] // 2 * rows, LANE))
    coef = pl.pallas_call(
        _peer_act_kernel,
        grid=(n // tt,),
        in_specs=[smem, tiles, wide, table],
        out_specs=wide,
        out_shape=jax.ShapeDtypeStruct((n, PEER_X), f32),
        compiler_params=_params("parallel"),
        name="peer_act",
    )(start, hn.reshape(n, rows, LANE), gate, _pack_rows(peer_u, reverse_pieces=True))
    y = pl.pallas_call(
        _peer_out_kernel,
        grid=(n // tt,),
        in_specs=[smem, wide, tiles, table],
        out_specs=tiles,
        out_shape=jax.ShapeDtypeStruct((n, rows, LANE), f32),
        compiler_params=_params("parallel"),
        name="peer_out",
    )(start, coef, x1.reshape(n, rows, LANE), _pack_rows(peer_v))
    return y.reshape(n, d)


def kernel(x_prompt, x_sample, cache_k, cache_v, cache_kidx, page_table, state_conv, state_rglru, norm1_g, w_in, b_gates, q_norm_g, k_norm_g, conv_w, conv_b, w_rgate, b_rgate, w_igate, b_igate, lru_lambda, w_o_attn, w_o_rnn, w_out, norm2_g, w_peer_q, peer_k1, peer_k2, peer_u, peer_v):
    bp, sp, d = x_prompt.shape
    db, ts, _ = x_sample.shape
    assert ts == 1, "the sample path handles one new token per sequence"
    d_rnn = conv_w.shape[1]
    past = page_table.shape[1] * PAGE_SIZE
    n_p, n_s = bp * sp, db * ts
    assert n_p % TOKEN_TILE == 0
    n_sp = -(-n_s // TOKEN_TILE) * TOKEN_TILE
    rnn_w = (conv_w, conv_b, w_rgate, b_rgate, w_igate, b_igate, lru_lambda)
    front_w = (norm1_g, w_in, b_gates, q_norm_g, k_norm_g, d_rnn)
    back_w = (w_o_attn, w_o_rnn, w_out, norm2_g, w_peer_q, peer_k1, peer_k2)
    pad = lambda a: jnp.concatenate([a, jnp.zeros((n_sp - n_s,) + a.shape[1:], a.dtype)])

    xp = x_prompt.reshape(n_p, d)
    q, k_p, v_p, qi, ki_p, wi, xr_p, yr, g_a, g_r = _front(xp, jnp.tile(jnp.arange(sp, dtype=f32), bp), *front_w)
    pr = lambda a: a.reshape(bp, sp, a.shape[1])
    attn_p = _prompt_attention(pr(q), pr(k_p), pr(v_p), pr(qi), pr(ki_p), pr(wi[:, :N_IDX_HEADS]))
    rnn_p, h_p = _rglru_prompt(pr(xr_p), pr(yr), *rnn_w)
    x1, hn, eidx_t, gate_t = _back(xp, attn_p.reshape(n_p, -1), rnn_p.reshape(n_p, d_rnn), g_a, g_r, *back_w)
    y_p = _peer(hn, x1, eidx_t, gate_t, peer_u, peer_v)

    xs = pad(x_sample.reshape(n_s, d))
    pos_s = pad(past + jnp.tile(jnp.arange(ts, dtype=f32), db))
    q, k_s, v_s, qi, ki_s, wi, xr_s, yr, g_a, g_r = _front(xs, pos_s, *front_w)
    sm = lambda a: a[:n_s]
    attn_s = _sample_attention(sm(q), sm(k_s), sm(v_s), sm(qi), sm(ki_s), sm(wi[:, :N_IDX_HEADS]),
                               cache_k, cache_v, cache_kidx, page_table)
    rnn_s, h_s = _rglru_sample(sm(xr_s), sm(yr), state_conv, state_rglru, *rnn_w)
    x1, hn, eidx_t, gate_t = _back(xs, pad(attn_s), pad(rnn_s), g_a, g_r, *back_w)
    y_s = _peer(hn, x1, eidx_t, gate_t, peer_u, peer_v)

    conv_p = pr(xr_p)[:, sp - (CONV_W - 1):, :]
    conv_s = jnp.concatenate([state_conv[:, ts:, :], sm(xr_s).reshape(db, ts, d_rnn)], axis=1)
    heads = lambda a, b_, t_: a.reshape(b_, t_, N_KV_HEADS, HEAD_DIM)
    return (y_p.reshape(bp, sp, d), sm(y_s).reshape(db, ts, d),
            heads(k_p, bp, sp), heads(v_p, bp, sp), pr(ki_p),
            conv_p, h_p,
            heads(sm(k_s), db, ts), heads(sm(v_s), db, ts), sm(ki_s).reshape(db, ts, IDX_DIM),
            conv_s, h_s)
```

```python
import functools

import numpy as np
import jax
import jax.numpy as jnp
from jax import lax
from jax.experimental import pallas as pl
from jax.experimental.pallas import tpu as pltpu

f32 = jnp.float32
bf16 = jnp.bfloat16
i32 = jnp.int32

N_HEADS = 16
HEAD_DIM = 64
N_KV_HEADS = 4
KV_GROUP = N_HEADS // N_KV_HEADS
ROT_DIM = HEAD_DIM // 4
ROT_HALF = ROT_DIM // 2
ROPE_THETA = 500000.0
QK_EPS = 1e-6
N_IDX_HEADS = 8
IDX_DIM = 64
TOPK_MAX = 256
PAGE_SIZE = 128
CONV_W = 4
LRU_C = 8.0
PEER_HEADS = 8
PEER_KEY_DIM = 128
N_KEYS = 128
PEER_TOPK = 16
PEER_SLOTS = PEER_HEADS * PEER_TOPK
NORM_EPS = 1e-6

LANE = 128
SUBLANE = 8
PACKED_ROWS = 16
VMEM_LIMIT_BYTES = 56 * 1024 * 1024

INT_MIN = -(2 ** 31)
NEG_BIG = -1e30

TOKEN_TILE = 256
Q_TILE = LANE
KEY_CHUNK = 512
KEY_SUB = 256
SOFTMAX_PARTIALS = 32
V_ROWS = HEAD_DIM + PACKED_ROWS
BOUND_MARGIN = 1.02
MIN_TRUSTED_SUM = 1e-25
RNN_CHUNK = 256
PEER_TOKENS = 64
PAGES_PER_STEP = 8
PEER_X = PEER_SLOTS * PACKED_ROWS


def _params(*sem):
    return pltpu.CompilerParams(dimension_semantics=sem, vmem_limit_bytes=VMEM_LIMIT_BYTES)


def _resident(shape):
    nd = len(shape)
    return pl.BlockSpec(shape, lambda *_: (0,) * nd, pipeline_mode=pl.Buffered(1))


def _gelu(x):
    return 0.5 * x * (1.0 + jnp.tanh(np.sqrt(2.0 / np.pi).astype(np.float32) * (x + 0.044715 * (x * x * x))))


def _dot(a, b):
    return jnp.dot(a, b, preferred_element_type=f32)


def _dot_nt(a, b):
    return lax.dot_general(a, b, (((1,), (1,)), ((), ())), preferred_element_type=f32)


def _front_kernel(x_ref, g1_ref, wa_ref, wb_ref, bg_ref, qg_ref, kg_ref, c_ref, s1_ref, s2_ref, bd_ref,
                  q_ref, k_ref, v_ref, qi_ref, ki_ref, wi_ref, xr_ref, yr_ref, ga_ref, gr_ref,
                  kb_ref, kib_ref, vt_ref, *, d_model, d_rnn):
    x = x_ref[...]
    h = (x * lax.rsqrt(jnp.mean(x * x, axis=-1, keepdims=True) + NORM_EPS) * g1_ref[...]).astype(bf16)
    cos, sin_hi, sin_lo = c_ref[...], s1_ref[...], s2_ref[...]
    bd = bd_ref[...]

    def rope(t):
        return t * cos + pltpu.roll(t, LANE - ROT_HALF, 1) * sin_hi + pltpu.roll(t, ROT_HALF, 1) * sin_lo

    def head_norm(t, g):
        sq = t * t
        hi = sq.astype(bf16)
        lo = (sq - hi.astype(f32)).astype(bf16)
        ss = _dot(hi, bd) + _dot(lo, bd)
        return t * lax.rsqrt(ss * (1.0 / HEAD_DIM) + QK_EPS) * g

    q_w = N_HEADS * HEAD_DIM
    kv_w = N_KV_HEADS * HEAD_DIM
    qi_w = N_IDX_HEADS * IDX_DIM
    pa = _dot(h, wa_ref[...])
    off = 0
    for j in range(q_w // LANE):
        t = pa[:, off + j * LANE: off + (j + 1) * LANE]
        q_ref[:, j * LANE:(j + 1) * LANE] = (rope(head_norm(t, qg_ref[...])) * (HEAD_DIM ** -0.5)).astype(bf16)
    off += q_w
    for j in range(kv_w // LANE):
        t = pa[:, off + j * LANE: off + (j + 1) * LANE]
        kt = rope(head_norm(t, kg_ref[...]))
        k_ref[:, j * LANE:(j + 1) * LANE] = kt
        kb_ref[:, j * LANE:(j + 1) * LANE] = kt.astype(bf16)
    off += kv_w
    vv = pa[:, off:off + kv_w]
    v_ref[...] = vv
    vt = vv.T.astype(bf16)
    ones_row = (lax.broadcasted_iota(i32, (V_ROWS - HEAD_DIM, vv.shape[0]), 0) == 0).astype(bf16)
    for g in range(N_KV_HEADS):
        vt_ref[0, g, 0:HEAD_DIM, :] = vt[g * HEAD_DIM:(g + 1) * HEAD_DIM, :]
        vt_ref[0, g, HEAD_DIM:V_ROWS, :] = ones_row
    off += kv_w
    for j in range(qi_w // LANE):
        t = pa[:, off + j * LANE: off + (j + 1) * LANE]
        qi_ref[:, j * LANE:(j + 1) * LANE] = rope(t).astype(bf16)
    off += qi_w
    kiv = rope(pa[:, off:off + LANE])[:, :IDX_DIM]
    ki_ref[...] = kiv
    kib_ref[...] = kiv.astype(bf16)
    off += LANE
    wi_ref[...] = pa[:, off:off + LANE] * (N_IDX_HEADS ** -0.5)

    xr_ref[...] = _dot(h, wb_ref[:, 0:d_rnn])
    yr_ref[...] = _dot(h, wb_ref[:, d_rnn:2 * d_rnn])
    o = 2 * d_rnn
    ga_ref[...] = jax.nn.sigmoid(_dot(h, wb_ref[:, o:o + d_model]) + bg_ref[:, 0:d_model])
    gr_ref[...] = jax.nn.sigmoid(_dot(h, wb_ref[:, o + d_model:o + 2 * d_model]) + bg_ref[:, d_model:2 * d_model])


def _rope_tables(pos):
    inv = ROPE_THETA ** (-jnp.arange(ROT_HALF, dtype=f32) / ROT_HALF)
    ang = pos.astype(f32)[:, None] * inv[None, :]
    c, s = jnp.cos(ang), jnp.sin(ang)
    n = pos.shape[0]
    z = lambda w: jnp.zeros((n, w), f32)
    cos = jnp.concatenate([c, c, jnp.ones((n, HEAD_DIM - ROT_DIM), f32)], axis=1)
    s_hi = jnp.concatenate([-s, z(HEAD_DIM - ROT_HALF)], axis=1)
    s_lo = jnp.concatenate([z(ROT_HALF), s, z(HEAD_DIM - ROT_DIM)], axis=1)
    rep = LANE // HEAD_DIM
    return jnp.tile(cos, (1, rep)), jnp.tile(s_hi, (1, rep)), jnp.tile(s_lo, (1, rep))


def _front(x, pos, norm1_g, w_in, b_gates, q_norm_g, k_norm_g, d_rnn):
    n, d_model = x.shape
    q_w, kv_w, qi_w = N_HEADS * HEAD_DIM, N_KV_HEADS * HEAD_DIM, N_IDX_HEADS * IDX_DIM
    sizes = [q_w, kv_w, kv_w, qi_w, IDX_DIM, N_IDX_HEADS, d_rnn, d_rnn, d_model, d_model]
    offs = np.cumsum([0] + sizes)
    parts = [w_in[:, offs[i]:offs[i + 1]] for i in range(len(sizes))]
    padl = lambda w: jnp.pad(w, ((0, 0), (0, LANE - w.shape[1])))
    wa = jnp.concatenate(parts[:4] + [padl(parts[4]), padl(parts[5])], axis=1).astype(bf16)
    wb = jnp.concatenate(parts[6:], axis=1).astype(bf16)
    cos, s_hi, s_lo = _rope_tables(pos)
    rep = LANE // HEAD_DIM
    head_of_lane = np.arange(LANE) // HEAD_DIM
    bd = jnp.asarray(head_of_lane[:, None] == head_of_lane[None, :], bf16)
    tm = TOKEN_TILE
    row = lambda w: pl.BlockSpec((tm, w), lambda i: (i, 0))
    outs = [(q_w, bf16), (kv_w, f32), (kv_w, f32), (qi_w, bf16), (IDX_DIM, f32), (LANE, f32),
            (d_rnn, f32), (d_rnn, f32), (d_model, f32), (d_model, f32), (kv_w, bf16), (IDX_DIM, bf16)]
    per_chunk = KEY_CHUNK // tm
    assert n % KEY_CHUNK == 0
    vt_spec = pl.BlockSpec((1, N_KV_HEADS, V_ROWS, tm), lambda i: (i // per_chunk, 0, 0, i % per_chunk))
    vt_shape = jax.ShapeDtypeStruct((n // KEY_CHUNK, N_KV_HEADS, V_ROWS, KEY_CHUNK), bf16)
    return pl.pallas_call(
        functools.partial(_front_kernel, d_model=d_model, d_rnn=d_rnn),
        grid=(n // tm,),
        in_specs=[row(d_model), _resident((1, d_model)), _resident(wa.shape), _resident(wb.shape),
                  _resident((1, 2 * d_model)), _resident((1, LANE)), _resident((1, LANE)),
                  row(LANE), row(LANE), row(LANE), _resident((LANE, LANE))],
        out_specs=[row(w) for w, _ in outs] + [vt_spec],
        out_shape=[jax.ShapeDtypeStruct((n, w), dt) for w, dt in outs] + [vt_shape],
        compiler_params=_params("parallel"),
        name="front",
    )(x, norm1_g.reshape(1, -1), wa, wb, b_gates.reshape(1, -1),
      jnp.tile(q_norm_g, rep).reshape(1, LANE), jnp.tile(k_norm_g, rep).reshape(1, LANE),
      cos, s_hi, s_lo, bd)


def _score_key(score):
    score = jnp.where(score == 0.0, 0.0, score)
    bits = pltpu.bitcast(score, i32)
    return bits ^ ((bits >> 31) & 0x7FFFFFFF)


def _kth_largest_key(count_ge, k, n_keys):
    c0 = count_ge(jnp.zeros(n_keys.shape, i32))
    ans = jnp.where(c0 >= k, 0, INT_MIN).astype(i32)
    cnt = jnp.where(c0 >= k, c0, n_keys)

    def unsettled(st):
        i, _, cnt = st
        return (i < 31) & (jnp.max(jnp.where((cnt == k) | (n_keys < k), 0, 1)) > 0)

    def body(st):
        i, ans, cnt = st
        cand = ans + lax.shift_left(jnp.int32(1), jnp.int32(30) - i)
        c = count_ge(cand)
        return i + 1, jnp.where(c >= k, cand, ans), jnp.where(c >= k, c, cnt)

    _, ans, cnt = lax.while_loop(unsettled, body, (jnp.int32(0), ans, cnt))
    return jnp.maximum(ans, INT_MIN + 1), cnt


def _prompt_attn_kernel(q_ref, qi_ref, wi_ref, ki_ref, k_ref, vt_ref, low_ref, o_ref, keys_ref, bias_ref,
                        m_ref, top_ref, kabs_ref, acc_ref, qz_ref, *, top_k):
    qb = pl.program_id(1)
    tq, ck, sub = Q_TILE, KEY_CHUNK, KEY_SUB
    n_chunks = (qb * tq + tq + ck - 1) // ck
    qpos = qb * tq + lax.broadcasted_iota(i32, (sub, tq), 1)
    krow = lax.broadcasted_iota(i32, (sub, tq), 0)

    qit = qi_ref[0].astype(f32).T.astype(bf16)
    w = wi_ref[0].T[0:N_IDX_HEADS, :] * (IDX_DIM ** -0.5)
    qt = q_ref[0].astype(f32).T.astype(bf16)
    kv_w = N_KV_HEADS * HEAD_DIM
    for g in range(N_KV_HEADS):
        q4 = jnp.concatenate([qt[(g * KV_GROUP + r) * HEAD_DIM:(g * KV_GROUP + r + 1) * HEAD_DIM, :]
                              for r in range(KV_GROUP)], axis=1)
        qz_ref[g] = jnp.zeros((kv_w, KV_GROUP * tq), bf16)
        qz_ref[g, g * HEAD_DIM:(g + 1) * HEAD_DIM, :] = q4

    def score_chunk(c, carry):
        for u in range(ck // sub):
            kc = ki_ref[0, c, u * sub:(u + 1) * sub, :]
            acc = jnp.zeros((sub, tq), f32)
            for h in range(N_IDX_HEADS):
                acc = acc + jnp.maximum(_dot(kc, qit[h * IDX_DIM:(h + 1) * IDX_DIM, :]), 0.0) * w[h:h + 1, :]
            causal = c * ck + u * sub + krow <= qpos
            keys_ref[c, u * sub:(u + 1) * sub, :] = jnp.where(causal, _score_key(acc), INT_MIN)
        return carry

    lax.fori_loop(0, n_chunks, score_chunk, 0)

    def count_ge(cand):
        lanes_of_acc = 8 * SUBLANE
        def body(c, acc):
            m = (keys_ref[c] >= cand).astype(f32)
            return acc + jnp.sum(m.reshape(ck // lanes_of_acc, lanes_of_acc, tq), axis=0)
        acc = lax.fori_loop(0, n_chunks, body, jnp.zeros((lanes_of_acc, tq), f32))
        return jnp.sum(acc, axis=0, keepdims=True)

    n_causal = (qb * tq + 1 + lax.broadcasted_iota(i32, (1, tq), 1)).astype(f32)
    thr, n_ge = _kth_largest_key(count_ge, float(top_k), n_causal)


    @pl.when(jnp.max(n_ge) > top_k)
    def _():
        need = top_k - count_ge(thr + 1)

        def body(c, before):
            kc = keys_ref[c]
            eq = kc == thr
            eqb = eq.astype(bf16)
            rank = _dot(low_ref[...], eqb) + before
            keys_ref[c] = jnp.where(eq & (rank >= need), INT_MIN, kc)
            return before + jnp.sum(eqb.astype(f32), axis=0, keepdims=True)

        lax.fori_loop(0, n_chunks, body, jnp.zeros((1, tq), f32))

    def to_bias(c, carry):
        bias_ref[c] = jnp.where(keys_ref[c] >= thr, 0.0, NEG_BIG)
        return carry

    lax.fori_loop(0, n_chunks, to_bias, 0)

    @pl.when(qb == 0)
    def _():
        def body(c, best):
            a = jnp.abs(k_ref[0, c].astype(f32))
            return jnp.maximum(best, jnp.max(a.reshape(ck // SUBLANE, SUBLANE, kv_w), axis=0))
        best = lax.fori_loop(0, k_ref.shape[1], body, jnp.zeros((SUBLANE, kv_w), f32))
        kabs_ref[...] = jnp.broadcast_to(jnp.max(best, axis=0, keepdims=True), (SUBLANE, kv_w))

    def sweep():
        acc_ref[...] = jnp.zeros(acc_ref.shape, f32)

        def accumulate(c, carry):
            bias = bias_ref[c]
            for g in range(N_KV_HEADS):
                s4 = _dot(k_ref[0, c], qz_ref[g])
                ps = [jnp.exp(s4[:, r * tq:(r + 1) * tq] + bias - top_ref[g * KV_GROUP + r]).astype(bf16)
                      for r in range(KV_GROUP)]
                acc_ref[g] = acc_ref[g] + _dot(vt_ref[c, g], jnp.concatenate(ps, axis=1))
            return carry

        lax.fori_loop(0, n_chunks, accumulate, 0)

    for g in range(N_KV_HEADS):
        bound = _dot(kabs_ref[...].astype(bf16), jnp.abs(qz_ref[g]))[0:1, :] * BOUND_MARGIN
        for r in range(KV_GROUP):
            top_ref[g * KV_GROUP + r] = bound[:, r * tq:(r + 1) * tq]
    sweep()
    smallest = acc_ref[0, HEAD_DIM:HEAD_DIM + 1, :]
    for g in range(1, N_KV_HEADS):
        smallest = jnp.minimum(smallest, acc_ref[g, HEAD_DIM:HEAD_DIM + 1, :])

    @pl.when(jnp.min(smallest) < MIN_TRUSTED_SUM)
    def _():
        part = m_ref.shape[1]
        m_ref[...] = jnp.full(m_ref.shape, NEG_BIG, f32)

        def col_max(c, carry):
            bias = bias_ref[c]
            for g in range(N_KV_HEADS):
                s4 = _dot(k_ref[0, c], qz_ref[g])
                for r in range(KV_GROUP):
                    h = g * KV_GROUP + r
                    s = (s4[:, r * tq:(r + 1) * tq] + bias).reshape(ck // part, part, tq)
                    m_ref[h] = jnp.maximum(m_ref[h], jnp.max(s, axis=0))
            return carry

        lax.fori_loop(0, n_chunks, col_max, 0)
        for h in range(N_HEADS):
            top_ref[h] = jnp.max(m_ref[h], axis=0, keepdims=True)
        sweep()

    heads = []
    for g in range(N_KV_HEADS):
        o4 = acc_ref[g, 0:HEAD_DIM, :] / acc_ref[g, HEAD_DIM:HEAD_DIM + 1, :]
        heads += [o4[:, r * tq:(r + 1) * tq] for r in range(KV_GROUP)]
    o_ref[0] = jnp.concatenate(heads, axis=0).T.astype(o_ref.dtype)


def _prompt_attention(q, qi, wi, ki, k, vt):
    b, s, _ = q.shape
    top_k = min(TOPK_MAX, s // 4)
    tq, ck = Q_TILE, KEY_CHUNK
    nq, nc = s // tq, s // ck
    gq = KV_GROUP * tq
    kv_w = N_KV_HEADS * HEAD_DIM
    low = jnp.asarray(np.arange(ck)[:, None] > np.arange(ck)[None, :], bf16)
    per_query = lambda w: pl.BlockSpec((1, tq, w), lambda i, j: (i, j, 0))
    return pl.pallas_call(
        functools.partial(_prompt_attn_kernel, top_k=top_k),
        grid=(b, nq),
        in_specs=[
            per_query(q.shape[2]), per_query(qi.shape[2]), per_query(wi.shape[2]),
            pl.BlockSpec((1, nc, ck, IDX_DIM), lambda i, j: (i, 0, 0, 0)),
            pl.BlockSpec((1, nc, ck, kv_w), lambda i, j: (i, 0, 0, 0)),
            pl.BlockSpec((nc, N_KV_HEADS, V_ROWS, ck), lambda i, j: (i, 0, 0, 0)),
            _resident((ck, ck)),
        ],
        out_specs=per_query(q.shape[2]),
        out_shape=jax.ShapeDtypeStruct(q.shape, bf16),
        scratch_shapes=[pltpu.VMEM((nc, ck, tq), i32), pltpu.VMEM((nc, ck, tq), f32),
                        pltpu.VMEM((N_HEADS, SOFTMAX_PARTIALS, tq), f32), pltpu.VMEM((N_HEADS, 1, tq), f32),
                        pltpu.VMEM((SUBLANE, kv_w), f32),
                        pltpu.VMEM((N_KV_HEADS, V_ROWS, gq), f32),
                        pltpu.VMEM((N_KV_HEADS, kv_w, gq), bf16)],
        compiler_params=_params("parallel", "arbitrary"),
        name="prompt_attention",
    )(q, qi, wi, ki.reshape(b, nc, ck, IDX_DIM), k.reshape(b, nc, ck, kv_w), vt, low)


def _sample_select_kernel(pt_ref, qi_ref, wi_ref, kin_ref, tri_ref, lt_ref, *rest, top_k, n_pages):
    page_refs, (sel_ref, seln_ref, keys_ref) = rest[:PAGES_PER_STEP], rest[PAGES_PER_STEP:]
    j = pl.program_id(1)
    qi = qi_ref[0]
    w = wi_ref[0]

    for i in range(PAGES_PER_STEP):
        s = _dot_nt(qi, page_refs[i][0].astype(bf16))
        score = jnp.sum(jnp.maximum(s, 0.0) * w, axis=0, keepdims=True)
        keys_ref[pl.ds(j * PAGES_PER_STEP + i, 1), :] = _score_key(score)

    @pl.when(j == pl.num_programs(1) - 1)
    def _():
        kn = kin_ref[0].astype(bf16).astype(f32)
        s_new = jnp.sum(qi.astype(f32) * kn, axis=1, keepdims=True)
        key_new = _score_key(jnp.sum(jnp.maximum(s_new, 0.0) * w, axis=0, keepdims=True))
        keys = keys_ref[...]

        def count_ge(cand):
            c = jnp.sum((keys >= cand).astype(f32), axis=1, keepdims=True)
            return jnp.sum(c, axis=0, keepdims=True) + (key_new >= cand).astype(f32)

        thr, _ = _kth_largest_key(count_ge, float(top_k), jnp.full((1, 1), n_pages * PAGE_SIZE + 1, f32))
        need = top_k - (count_ge(thr + 1))
        eq = keys == thr
        eqb = eq.astype(bf16)
        in_row = _dot(eqb, tri_ref[...])
        per_page = jnp.sum(eqb.astype(f32), axis=1, keepdims=True)
        before = _dot(lt_ref[...], per_page.astype(bf16) * jnp.ones((1, LANE), bf16))[:, 0:1]
        rank = in_row + before
        sel = (keys > thr) | (eq & (rank < need))
        sel_ref[0] = sel.astype(f32)
        n_eq = jnp.sum(per_page, axis=0, keepdims=True)
        sel_new = (key_new > thr) | ((key_new == thr) & (n_eq < need))
        seln_ref[0] = jnp.broadcast_to(sel_new.astype(f32), (1, LANE))


def _sample_attend_kernel(pt_ref, q_ref, sel_ref, seln_ref, kn_ref, vn_ref, gm_ref, *rest, n_pages):
    k_refs = rest[:PAGES_PER_STEP]
    v_refs = rest[PAGES_PER_STEP:2 * PAGES_PER_STEP]
    o_ref, m_ref, l_ref, acc_ref = rest[2 * PAGES_PER_STEP:]
    j = pl.program_id(1)
    q = q_ref[0]

    @pl.when(j == 0)
    def _():
        m_ref[...] = jnp.full(m_ref.shape, NEG_BIG, f32)
        l_ref[...] = jnp.zeros(l_ref.shape, f32)
        acc_ref[...] = jnp.zeros(acc_ref.shape, f32)

    def update(s, vals):
        m = m_ref[...]
        m_new = jnp.maximum(m, jnp.max(s, axis=1, keepdims=True))
        alpha = jnp.exp(m - m_new)
        p = jnp.exp(s - m_new)
        l_ref[...] = alpha * l_ref[...] + jnp.sum(p, axis=1, keepdims=True)
        acc_ref[...] = alpha * acc_ref[...] + vals(p)
        m_ref[...] = m_new

    for i in range(PAGES_PER_STEP):
        sel = sel_ref[0, pl.ds(j * PAGES_PER_STEP + i, 1), :] > 0.5
        s = jnp.where(sel, _dot_nt(q, k_refs[i][0].astype(bf16)), NEG_BIG)
        update(s, lambda p, i=i: _dot(p.astype(bf16), v_refs[i][0].astype(bf16)))

    @pl.when(j == pl.num_programs(1) - 1)
    def _():
        kn = kn_ref[0].astype(bf16).astype(f32)
        vn = vn_ref[0].astype(bf16).astype(f32)
        s = jnp.sum(q.astype(f32) * kn, axis=1, keepdims=True)
        s = jnp.where(seln_ref[0][:, 0:1] > 0.5, s, NEG_BIG)
        update(s, lambda p: p.astype(bf16).astype(f32) * vn)
        full = acc_ref[...] / l_ref[...] * gm_ref[...]
        out = full[:, 0:HEAD_DIM]
        for g in range(1, N_KV_HEADS):
            out = out + full[:, g * HEAD_DIM:(g + 1) * HEAD_DIM]
        o_ref[0] = out.astype(o_ref.dtype)


def _sample_attention(q, k_new, v_new, qi, ki_new, wi, cache_k, cache_v, cache_kidx, page_table):
    db = q.shape[0]
    n_pages = page_table.shape[1]
    past = n_pages * PAGE_SIZE
    top_k = min(TOPK_MAX, (past + 1) // 4)
    pps = PAGES_PER_STEP
    steps = n_pages // pps
    kv_w = N_KV_HEADS * HEAD_DIM
    n_pool = cache_k.shape[0]
    ck2 = cache_k.reshape(n_pool, PAGE_SIZE, kv_w)
    cv2 = cache_v.reshape(n_pool, PAGE_SIZE, kv_w)

    def page_spec(width, i):
        return pl.BlockSpec((1, PAGE_SIZE, width), lambda b, j, pt, i=i: (pt[b, j * pps + i], 0, 0))

    per_seq = lambda *shape: pl.BlockSpec((1,) + shape, lambda b, j, pt: (b,) + (0,) * len(shape))
    const = lambda *shape: pl.BlockSpec(shape, lambda b, j, pt: (0,) * len(shape))
    tri = jnp.asarray(np.arange(PAGE_SIZE)[:, None] < np.arange(PAGE_SIZE)[None, :], bf16)
    lower = jnp.asarray(np.arange(n_pages)[:, None] > np.arange(n_pages)[None, :], bf16)

    sel, sel_new = pl.pallas_call(
        functools.partial(_sample_select_kernel, top_k=top_k, n_pages=n_pages),
        grid_spec=pltpu.PrefetchScalarGridSpec(
            num_scalar_prefetch=1,
            grid=(db, steps),
            in_specs=[per_seq(N_IDX_HEADS, IDX_DIM), per_seq(N_IDX_HEADS, 1), per_seq(1, IDX_DIM),
                      const(PAGE_SIZE, PAGE_SIZE), const(n_pages, n_pages)]
                     + [page_spec(IDX_DIM, i) for i in range(pps)],
            out_specs=[per_seq(n_pages, PAGE_SIZE), per_seq(1, LANE)],
            scratch_shapes=[pltpu.VMEM((n_pages, PAGE_SIZE), i32)],
        ),
        out_shape=[jax.ShapeDtypeStruct((db, n_pages, PAGE_SIZE), f32), jax.ShapeDtypeStruct((db, 1, LANE), f32)],
        compiler_params=_params("parallel", "arbitrary"),
        name="sample_select",
    )(page_table, qi.reshape(db, N_IDX_HEADS, IDX_DIM), (wi * (IDX_DIM ** -0.5)).reshape(db, N_IDX_HEADS, 1),
      ki_new.reshape(db, 1, IDX_DIM), tri, lower, *([cache_kidx] * pps))

    group_mask = np.repeat(np.arange(N_HEADS)[:, None] // KV_GROUP == np.arange(N_KV_HEADS)[None, :], HEAD_DIM, axis=1)
    q_bd = jnp.where(group_mask[None], jnp.tile(q.reshape(db, N_HEADS, HEAD_DIM), (1, 1, N_KV_HEADS)), 0).astype(bf16)
    out = pl.pallas_call(
        functools.partial(_sample_attend_kernel, n_pages=n_pages),
        grid_spec=pltpu.PrefetchScalarGridSpec(
            num_scalar_prefetch=1,
            grid=(db, steps),
            in_specs=[per_seq(N_HEADS, kv_w), per_seq(n_pages, PAGE_SIZE), per_seq(1, LANE),
                      per_seq(1, kv_w), per_seq(1, kv_w), const(N_HEADS, kv_w)]
                     + [page_spec(kv_w, i) for i in range(pps)] * 2,
            out_specs=per_seq(N_HEADS, HEAD_DIM),
            scratch_shapes=[pltpu.VMEM((N_HEADS, 1), f32), pltpu.VMEM((N_HEADS, 1), f32), pltpu.VMEM((N_HEADS, kv_w), f32)],
        ),
        out_shape=jax.ShapeDtypeStruct((db, N_HEADS, HEAD_DIM), bf16),
        compiler_params=_params("parallel", "arbitrary"),
        name="sample_attend",
    )(page_table, q_bd, sel, sel_new, k_new.reshape(db, 1, kv_w), v_new.reshape(db, 1, kv_w),
      jnp.asarray(group_mask, f32), *([ck2] * pps), *([cv2] * pps))
    return out.reshape(db, N_HEADS * HEAD_DIM)


def _rglru_coeffs(xc, wr_ref, br_ref, wi_ref, bi_ref, lam_ref):
    xb = xc.astype(bf16)
    r = jax.nn.sigmoid(_dot(xb, wr_ref[...]) + br_ref[...])
    i = jax.nn.sigmoid(_dot(xb, wi_ref[...]) + bi_ref[...])
    z = -lam_ref[...]
    softplus = jnp.maximum(z, 0.0) + jnp.log1p(jnp.exp(-jnp.abs(z)))
    log_a = -LRU_C * r * softplus
    a = jnp.exp(log_a)
    th = jnp.tanh(log_a)
    b = jnp.sqrt(-2.0 * th / (1.0 - th)) * (i * xc)
    return a, b


def _rglru_seq_kernel(xr_ref, yr_ref, cw_ref, cb_ref, wr_ref, br_ref, wi_ref, bi_ref, lam_ref,
                      o_ref, hl_ref, xprev_ref, h_ref, a_ref, b_ref):
    t = pl.program_id(1)
    tc, d = xr_ref.shape[1], xr_ref.shape[2]

    @pl.when(t == 0)
    def _():
        xprev_ref[...] = jnp.zeros(xprev_ref.shape, f32)
        h_ref[...] = jnp.zeros(h_ref.shape, f32)

    xr = xr_ref[0]
    xcat = jnp.concatenate([xprev_ref[...], xr], axis=0)
    xc = cb_ref[...] + cw_ref[0:1, :] * xcat[SUBLANE - 3:SUBLANE - 3 + tc]
    for j in range(1, CONV_W):
        xc = xc + cw_ref[j:j + 1, :] * xcat[SUBLANE - 3 + j:SUBLANE - 3 + j + tc]
    xprev_ref[...] = xr[tc - SUBLANE:, :]
    a, b = _rglru_coeffs(xc, wr_ref, br_ref, wi_ref, bi_ref, lam_ref)
    a_ref[...] = a
    b_ref[...] = b
    row = lax.broadcasted_iota(i32, (SUBLANE, d), 0)

    def slab(s, h_prev):
        r0 = pl.multiple_of(s * SUBLANE, SUBLANE)
        a8 = a_ref[pl.ds(r0, SUBLANE), :]
        b8 = b_ref[pl.ds(r0, SUBLANE), :]
        for sh in (1, 2, 4):
            keep = row >= sh
            b8 = jnp.where(keep, a8 * pltpu.roll(b8, sh, 0) + b8, b8)
            a8 = jnp.where(keep, a8 * pltpu.roll(a8, sh, 0), a8)
        h8 = a8 * h_prev + b8
        o_ref[0, pl.ds(r0, SUBLANE), :] = (h8 * _gelu(yr_ref[0, pl.ds(r0, SUBLANE), :])).astype(o_ref.dtype)
        return h8[SUBLANE - 1:SUBLANE, :]

    h_last = lax.fori_loop(0, tc // SUBLANE, slab, h_ref[...])
    h_ref[...] = h_last
    hl_ref[0] = h_last


def _block_diag(w):
    n, a, b = w.shape
    eye = jnp.eye(n, dtype=w.dtype)
    return (eye[:, None, :, None] * w[:, :, None, :]).reshape(n * a, n * b).astype(bf16)


def _rglru_prompt(xr, yr, conv_w, conv_b, w_rgate, b_rgate, w_igate, b_igate, lru_lambda):
    b, t, d = xr.shape
    tc = RNN_CHUNK
    vec = lambda v: v.reshape(1, d)
    seq = pl.BlockSpec((1, tc, d), lambda i, j: (i, j, 0))
    out, h_last = pl.pallas_call(
        _rglru_seq_kernel,
        grid=(b, t // tc),
        in_specs=[seq, seq, _resident((CONV_W, d)), _resident((1, d)), _resident((d, d)), _resident((1, d)),
                  _resident((d, d)), _resident((1, d)), _resident((1, d))],
        out_specs=[seq, pl.BlockSpec((1, 1, d), lambda i, j: (i, 0, 0))],
        out_shape=[jax.ShapeDtypeStruct((b, t, d), bf16), jax.ShapeDtypeStruct((b, 1, d), f32)],
        scratch_shapes=[pltpu.VMEM((SUBLANE, d), f32), pltpu.VMEM((1, d), f32),
                        pltpu.VMEM((tc, d), f32), pltpu.VMEM((tc, d), f32)],
        compiler_params=_params("parallel", "arbitrary"),
        name="rglru_prompt",
    )(xr, yr, conv_w, vec(conv_b), _block_diag(w_rgate), vec(b_rgate), _block_diag(w_igate), vec(b_igate),
      vec(lru_lambda))
    return out, h_last.reshape(b, d)


def _rglru_step_kernel(xr_ref, yr_ref, c0_ref, c1_ref, c2_ref, h0_ref, cw_ref, cb_ref, wr_ref, br_ref, wi_ref,
                       bi_ref, lam_ref, o_ref, h_ref):
    xc = cb_ref[...] + cw_ref[0:1, :] * c0_ref[...]
    xc = xc + cw_ref[1:2, :] * c1_ref[...]
    xc = xc + cw_ref[2:3, :] * c2_ref[...]
    xc = xc + cw_ref[3:4, :] * xr_ref[...]
    a, b = _rglru_coeffs(xc, wr_ref, br_ref, wi_ref, bi_ref, lam_ref)
    h = a * h0_ref[...] + b
    h_ref[...] = h
    o_ref[...] = (h * _gelu(yr_ref[...])).astype(o_ref.dtype)


def _rglru_sample(xr, yr, state_conv, h0, conv_w, conv_b, w_rgate, b_rgate, w_igate, b_igate, lru_lambda):
    db, d = xr.shape
    vec = lambda v: v.reshape(1, d)
    return pl.pallas_call(
        _rglru_step_kernel,
        out_shape=[jax.ShapeDtypeStruct((db, d), bf16), jax.ShapeDtypeStruct((db, d), f32)],
        compiler_params=pltpu.CompilerParams(vmem_limit_bytes=VMEM_LIMIT_BYTES),
        name="rglru_sample",
    )(xr, yr, state_conv[:, 0], state_conv[:, 1], state_conv[:, 2], h0, conv_w, vec(conv_b),
      _block_diag(w_rgate), vec(b_rgate), _block_diag(w_igate), vec(b_igate), vec(lru_lambda))


def _top_rows(x, k, payload=None):
    n = x.shape[0]
    row = lax.broadcasted_iota(i32, x.shape, 0)
    vals, picked = [], []
    for _ in range(k):
        m = jnp.max(x, axis=0, keepdims=True)
        r = jnp.min(jnp.where(x == m, row, n), axis=0, keepdims=True)
        hit = row == r
        vals.append(m)
        picked.append(r if payload is None else jnp.sum(jnp.where(hit, payload, 0), axis=0, keepdims=True))
        x = jnp.where(hit, -jnp.inf, x)
    return jnp.concatenate(vals, axis=0), jnp.concatenate(picked, axis=0)


_PAIR_COUNTS = [PEER_TOPK // (r1 + 1) for r1 in range(PEER_TOPK)]


def _back_kernel(x_ref, ao_ref, ro_ref, ga_ref, gr_ref, woa_ref, wor_ref, wout_ref, g2_ref, wqt_ref, k1_ref, k2_ref,
                 x1_ref, hn_ref, eidx_ref, gate_ref):
    mixed = ga_ref[...] * _dot(ao_ref[...], woa_ref[...]) + gr_ref[...] * _dot(ro_ref[...], wor_ref[...])
    x1 = x_ref[...] + _dot(mixed.astype(bf16), wout_ref[...])
    x1_ref[...] = x1
    hn = (x1 * lax.rsqrt(jnp.mean(x1 * x1, axis=-1, keepdims=True) + NORM_EPS) * g2_ref[...]).astype(bf16)
    hn_ref[...] = hn
    qt = _dot_nt(wqt_ref[...], hn).astype(bf16)
    half = PEER_KEY_DIM // 2
    for h in range(PEER_HEADS):
        base = h * PEER_KEY_DIM
        v1, i1 = _top_rows(_dot(k1_ref[...], qt[base:base + half, :]), PEER_TOPK)
        v2, i2 = _top_rows(_dot(k2_ref[...], qt[base + half:base + PEER_KEY_DIM, :]), PEER_TOPK)
        cand, cid = [], []
        for r1, n2 in enumerate(_PAIR_COUNTS):
            cand.append(v1[r1:r1 + 1, :] + v2[0:n2, :])
            cid.append(i1[r1:r1 + 1, :] * N_KEYS + i2[0:n2, :])
        pad = -sum(_PAIR_COUNTS) % SUBLANE
        tm = v1.shape[1]
        cand.append(jnp.full((pad, tm), -jnp.inf, f32))
        cid.append(jnp.zeros((pad, tm), i32))
        sv, eidx = _top_rows(jnp.concatenate(cand, axis=0), PEER_TOPK, jnp.concatenate(cid, axis=0))
        p = jnp.exp(sv - sv[0:1, :])
        eidx_ref[h * PEER_TOPK:(h + 1) * PEER_TOPK, :] = eidx
        gate_ref[h * PEER_TOPK:(h + 1) * PEER_TOPK, :] = p / jnp.sum(p, axis=0, keepdims=True)


def _back(x, attn_o, rnn_o, g_a, g_r, w_o_attn, w_o_rnn, w_out, norm2_g, w_peer_q, peer_k1, peer_k2):
    n, d = x.shape
    tm = TOKEN_TILE
    row = lambda w: pl.BlockSpec((tm, w), lambda i: (i, 0))
    col = pl.BlockSpec((PEER_SLOTS, tm), lambda i: (0, i))
    ws = [w_o_attn.astype(bf16), w_o_rnn.astype(bf16), w_out.astype(bf16), norm2_g.reshape(1, d),
          w_peer_q.T.astype(bf16), peer_k1.astype(bf16), peer_k2.astype(bf16)]
    return pl.pallas_call(
        _back_kernel,
        grid=(n // tm,),
        in_specs=[row(d), row(attn_o.shape[1]), row(rnn_o.shape[1]), row(d), row(d)] + [_resident(w.shape) for w in ws],
        out_specs=[row(d), row(d), col, col],
        out_shape=[jax.ShapeDtypeStruct((n, d), f32), jax.ShapeDtypeStruct((n, d), bf16),
                   jax.ShapeDtypeStruct((PEER_SLOTS, n), i32), jax.ShapeDtypeStruct((PEER_SLOTS, n), f32)],
        compiler_params=_params("parallel"),
        name="back",
    )(x, attn_o, rnn_o, g_a, g_r, *ws)


def _pack_rows(table, reverse_pieces=False):
    e, d = table.shape
    bits = lax.bitcast_convert_type(table.astype(bf16), jnp.uint16).astype(jnp.uint32).reshape(e // 2, 2, d // LANE, LANE)
    if reverse_pieces:
        bits = bits[:, :, ::-1, :]
    return lax.bitcast_convert_type(bits[:, 0] | (bits[:, 1] << 16), i32).reshape(e // 2 * (d // LANE), LANE)


def _gather_tiles(table_ref, row_ref, t):
    starts = row_ref.at[t]
    tiles = [table_ref[pl.ds(pl.multiple_of(starts[s], SUBLANE), SUBLANE), :] for s in range(PEER_SLOTS)]
    return pltpu.bitcast(jnp.concatenate(tiles, axis=0), bf16)


def _peer_act_kernel(start_ref, hn_ref, gate_ref, u_ref, coef_ref):
    row = lax.broadcasted_iota(i32, (SUBLANE, LANE), 0)
    n_blk = PEER_X // LANE

    def token_group(gi, carry):
        t0 = pl.multiple_of(gi * SUBLANE, SUBLANE)
        acts = [jnp.zeros((SUBLANE, LANE), f32) for _ in range(n_blk)]
        for j in range(SUBLANE):
            rows = _gather_tiles(u_ref, start_ref, t0 + j)
            r = _dot_nt(hn_ref[t0 + j], rows)
            for blk in range(n_blk):
                z = pltpu.roll(r[:, blk * LANE:(blk + 1) * LANE], 0, 1, stride=2, stride_axis=0)
                for sh in (4, 2, 1):
                    z = z + pltpu.roll(z, sh, 0)
                acts[blk] = jnp.where(row == j, z, acts[blk])
        for blk in range(n_blk):
            cols = slice(blk * LANE, (blk + 1) * LANE)
            coef = gate_ref[pl.ds(t0, SUBLANE), cols] * _gelu(acts[blk])
            coef_ref[pl.ds(t0, SUBLANE), cols] = pltpu.roll(coef, LANE - (PACKED_ROWS - 2), 1)
        return carry

    lax.fori_loop(0, hn_ref.shape[0] // SUBLANE, token_group, 0)


def _peer_out_kernel(start_ref, coef_ref, x1_ref, v_ref, y_ref):
    n_blk = PEER_X // LANE

    def token_group(gi, carry):
        t0 = pl.multiple_of(gi * SUBLANE, SUBLANE)
        coef = coef_ref[pl.ds(t0, SUBLANE), :]
        for j in range(SUBLANE):
            rows = _gather_tiles(v_ref, start_ref, t0 + j)
            pick = [pltpu.roll(jnp.broadcast_to(coef[j:j + 1, blk * LANE:(blk + 1) * LANE], (SUBLANE, LANE)),
                               0, 1, stride=2, stride_axis=0) for blk in range(n_blk)]
            y_ref[t0 + j] = x1_ref[t0 + j] + _dot(jnp.concatenate(pick, axis=1).astype(bf16), rows)
        return carry

    lax.fori_loop(0, x1_ref.shape[0] // SUBLANE, token_group, 0)


def _peer(hn, x1, eidx_t, gate_t, peer_u, peer_v):
    n, d = x1.shape
    tt = PEER_TOKENS
    rows = d // LANE
    eidx = eidx_t.T
    start = (eidx >> 1) * SUBLANE
    col = jnp.tile(jnp.arange(PACKED_ROWS, dtype=i32), PEER_SLOTS)[None, :]
    gate = jnp.where(col == PACKED_ROWS - 2 + jnp.repeat(eidx & 1, PACKED_ROWS, axis=1),
                     jnp.repeat(gate_t.T, PACKED_ROWS, axis=1), 0.0)
    smem = pl.BlockSpec((tt, PEER_SLOTS), lambda i: (i, 0), memory_space=pltpu.SMEM)
    tiles = pl.BlockSpec((tt, rows, LANE), lambda i: (i, 0, 0))
    wide = pl.BlockSpec((tt, PEER_X), lambda i: (i, 0))
    table = _resident((peer_u.shape[0] // 2 * rows, LANE))
    coef = pl.pallas_call(
        _peer_act_kernel,
        grid=(n // tt,),
        in_specs=[smem, tiles, wide, table],
        out_specs=wide,
        out_shape=jax.ShapeDtypeStruct((n, PEER_X), f32),
        compiler_params=_params("parallel"),
        name="peer_act",
    )(start, hn.reshape(n, rows, LANE), gate, _pack_rows(peer_u, reverse_pieces=True))
    y = pl.pallas_call(
        _peer_out_kernel,
        grid=(n // tt,),
        in_specs=[smem, wide, tiles, table],
        out_specs=tiles,
        out_shape=jax.ShapeDtypeStruct((n, rows, LANE), f32),
        compiler_params=_params("parallel"),
        name="peer_out",
    )(start, coef, x1.reshape(n, rows, LANE), _pack_rows(peer_v))
    return y.reshape(n, d)


def kernel(x_prompt, x_sample, cache_k, cache_v, cache_kidx, page_table, state_conv, state_rglru, norm1_g, w_in, b_gates, q_norm_g, k_norm_g, conv_w, conv_b, w_rgate, b_rgate, w_igate, b_igate, lru_lambda, w_o_attn, w_o_rnn, w_out, norm2_g, w_peer_q, peer_k1, peer_k2, peer_u, peer_v):
    bp, sp, d = x_prompt.shape
    db, ts, _ = x_sample.shape
    assert ts == 1, "the sample path handles one new token per sequence"
    d_rnn = conv_w.shape[1]
    past = page_table.shape[1] * PAGE_SIZE
    n_p, n_s = bp * sp, db * ts
    assert n_p % TOKEN_TILE == 0
    n_sp = -(-n_s // KEY_CHUNK) * KEY_CHUNK
    rnn_w = (conv_w, conv_b, w_rgate, b_rgate, w_igate, b_igate, lru_lambda)
    front_w = (norm1_g, w_in, b_gates, q_norm_g, k_norm_g, d_rnn)
    back_w = (w_o_attn, w_o_rnn, w_out, norm2_g, w_peer_q, peer_k1, peer_k2)
    pad = lambda a: jnp.concatenate([a, jnp.zeros((n_sp - n_s,) + a.shape[1:], a.dtype)])

    xp = x_prompt.reshape(n_p, d)
    q, k_p, v_p, qi, ki_p, wi, xr_p, yr, g_a, g_r, kb, kib, vt = _front(
        xp, jnp.tile(jnp.arange(sp, dtype=f32), bp), *front_w)
    pr = lambda a: a.reshape(bp, sp, a.shape[1])
    attn_p = _prompt_attention(pr(q), pr(qi), pr(wi), pr(kib), pr(kb), vt)
    rnn_p, h_p = _rglru_prompt(pr(xr_p), pr(yr), *rnn_w)
    x1, hn, eidx_t, gate_t = _back(xp, attn_p.reshape(n_p, -1), rnn_p.reshape(n_p, d_rnn), g_a, g_r, *back_w)
    y_p = _peer(hn, x1, eidx_t, gate_t, peer_u, peer_v)

    xs = pad(x_sample.reshape(n_s, d))
    pos_s = pad(past + jnp.tile(jnp.arange(ts, dtype=f32), db))
    q, k_s, v_s, qi, ki_s, wi, xr_s, yr, g_a, g_r, _, _, _ = _front(xs, pos_s, *front_w)
    sm = lambda a: a[:n_s]
    attn_s = _sample_attention(sm(q), sm(k_s), sm(v_s), sm(qi), sm(ki_s), sm(wi[:, :N_IDX_HEADS]),
                               cache_k, cache_v, cache_kidx, page_table)
    rnn_s, h_s = _rglru_sample(sm(xr_s), sm(yr), state_conv, state_rglru, *rnn_w)
    x1, hn, eidx_t, gate_t = _back(xs, pad(attn_s), pad(rnn_s), g_a, g_r, *back_w)
    y_s = _peer(hn, x1, eidx_t, gate_t, peer_u, peer_v)

    conv_p = pr(xr_p)[:, sp - (CONV_W - 1):, :]
    conv_s = jnp.concatenate([state_conv[:, ts:, :], sm(xr_s).reshape(db, ts, d_rnn)], axis=1)
    heads = lambda a, b_, t_: a.reshape(b_, t_, N_KV_HEADS, HEAD_DIM)
    return (y_p.reshape(bp, sp, d), sm(y_s).reshape(db, ts, d),
            heads(k_p, bp, sp), heads(v_p, bp, sp), pr(ki_p),
            conv_p, h_p,
            heads(sm(k_s), db, ts), heads(sm(v_s), db, ts), sm(ki_s).reshape(db, ts, IDX_DIM),
            conv_s, h_s)
```

```python
import functools

import numpy as np
import jax
import jax.numpy as jnp
from jax import lax
from jax.experimental import pallas as pl
from jax.experimental.pallas import tpu as pltpu

f32 = jnp.float32
bf16 = jnp.bfloat16
i32 = jnp.int32

N_HEADS = 16
HEAD_DIM = 64
N_KV_HEADS = 4
KV_GROUP = N_HEADS // N_KV_HEADS
ROT_DIM = HEAD_DIM // 4
ROT_HALF = ROT_DIM // 2
ROPE_THETA = 500000.0
QK_EPS = 1e-6
N_IDX_HEADS = 8
IDX_DIM = 64
TOPK_MAX = 256
PAGE_SIZE = 128
CONV_W = 4
LRU_C = 8.0
PEER_HEADS = 8
PEER_KEY_DIM = 128
N_KEYS = 128
PEER_TOPK = 16
PEER_SLOTS = PEER_HEADS * PEER_TOPK
NORM_EPS = 1e-6

LANE = 128
SUBLANE = 8
PACKED_ROWS = 16
VMEM_LIMIT_BYTES = 56 * 1024 * 1024

INT_MIN = -(2 ** 31)
NEG_BIG = -1e30

TOKEN_TILE = 256
Q_TILE = LANE
KEY_CHUNK = 512
KEY_SUB = 256
SOFTMAX_PARTIALS = 32
V_ROWS = HEAD_DIM + PACKED_ROWS
BOUND_MARGIN = 1.02
MIN_TRUSTED_SUM = 1e-25
RNN_CHUNK = 256
PEER_TOKENS = 64
PAGES_PER_STEP = 8
PEER_X = PEER_SLOTS * PACKED_ROWS


def _params(*sem):
    return pltpu.CompilerParams(dimension_semantics=sem, vmem_limit_bytes=VMEM_LIMIT_BYTES)


def _resident(shape):
    nd = len(shape)
    return pl.BlockSpec(shape, lambda *_: (0,) * nd, pipeline_mode=pl.Buffered(1))


def _gelu(x):
    return 0.5 * x * (1.0 + jnp.tanh(np.sqrt(2.0 / np.pi).astype(np.float32) * (x + 0.044715 * (x * x * x))))


def _dot(a, b):
    return jnp.dot(a, b, preferred_element_type=f32)


def _dot_nt(a, b):
    return lax.dot_general(a, b, (((1,), (1,)), ((), ())), preferred_element_type=f32)


def _front_kernel(x_ref, g1_ref, wa_ref, wb_ref, bg_ref, qg_ref, kg_ref, c_ref, s1_ref, s2_ref, bd_ref,
                  q_ref, k_ref, v_ref, qi_ref, ki_ref, wi_ref, xr_ref, yr_ref, ga_ref, gr_ref,
                  kb_ref, kib_ref, vt_ref, *, d_model, d_rnn):
    x = x_ref[...]
    h = (x * lax.rsqrt(jnp.mean(x * x, axis=-1, keepdims=True) + NORM_EPS) * g1_ref[...]).astype(bf16)
    cos, sin_hi, sin_lo = c_ref[...], s1_ref[...], s2_ref[...]
    bd = bd_ref[...]

    def rope(t):
        return t * cos + pltpu.roll(t, LANE - ROT_HALF, 1) * sin_hi + pltpu.roll(t, ROT_HALF, 1) * sin_lo

    def head_norm(t, g):
        sq = t * t
        hi = sq.astype(bf16)
        lo = (sq - hi.astype(f32)).astype(bf16)
        ss = _dot(hi, bd) + _dot(lo, bd)
        return t * lax.rsqrt(ss * (1.0 / HEAD_DIM) + QK_EPS) * g

    q_w = N_HEADS * HEAD_DIM
    kv_w = N_KV_HEADS * HEAD_DIM
    qi_w = N_IDX_HEADS * IDX_DIM
    pa = _dot(h, wa_ref[...])
    off = 0
    for j in range(q_w // LANE):
        t = pa[:, off + j * LANE: off + (j + 1) * LANE]
        q_ref[:, j * LANE:(j + 1) * LANE] = (rope(head_norm(t, qg_ref[...])) * (HEAD_DIM ** -0.5)).astype(bf16)
    off += q_w
    for j in range(kv_w // LANE):
        t = pa[:, off + j * LANE: off + (j + 1) * LANE]
        kt = rope(head_norm(t, kg_ref[...]))
        k_ref[:, j * LANE:(j + 1) * LANE] = kt
        kb_ref[:, j * LANE:(j + 1) * LANE] = kt.astype(bf16)
    off += kv_w
    vv = pa[:, off:off + kv_w]
    v_ref[...] = vv
    vt = vv.T.astype(bf16)
    ones_row = (lax.broadcasted_iota(i32, (V_ROWS - HEAD_DIM, vv.shape[0]), 0) == 0).astype(bf16)
    for g in range(N_KV_HEADS):
        vt_ref[0, g, 0:HEAD_DIM, :] = vt[g * HEAD_DIM:(g + 1) * HEAD_DIM, :]
        vt_ref[0, g, HEAD_DIM:V_ROWS, :] = ones_row
    off += kv_w
    for j in range(qi_w // LANE):
        t = pa[:, off + j * LANE: off + (j + 1) * LANE]
        qi_ref[:, j * LANE:(j + 1) * LANE] = rope(t).astype(bf16)
    off += qi_w
    kiv = rope(pa[:, off:off + LANE])[:, :IDX_DIM]
    ki_ref[...] = kiv
    kib_ref[...] = kiv.astype(bf16)
    off += LANE
    wi_ref[...] = pa[:, off:off + LANE] * (N_IDX_HEADS ** -0.5)

    xr_ref[...] = _dot(h, wb_ref[:, 0:d_rnn])
    yr_ref[...] = _dot(h, wb_ref[:, d_rnn:2 * d_rnn])
    o = 2 * d_rnn
    ga_ref[...] = jax.nn.sigmoid(_dot(h, wb_ref[:, o:o + d_model]) + bg_ref[:, 0:d_model])
    gr_ref[...] = jax.nn.sigmoid(_dot(h, wb_ref[:, o + d_model:o + 2 * d_model]) + bg_ref[:, d_model:2 * d_model])


def _rope_tables(pos):
    inv = ROPE_THETA ** (-jnp.arange(ROT_HALF, dtype=f32) / ROT_HALF)
    ang = pos.astype(f32)[:, None] * inv[None, :]
    c, s = jnp.cos(ang), jnp.sin(ang)
    n = pos.shape[0]
    z = lambda w: jnp.zeros((n, w), f32)
    cos = jnp.concatenate([c, c, jnp.ones((n, HEAD_DIM - ROT_DIM), f32)], axis=1)
    s_hi = jnp.concatenate([-s, z(HEAD_DIM - ROT_HALF)], axis=1)
    s_lo = jnp.concatenate([z(ROT_HALF), s, z(HEAD_DIM - ROT_DIM)], axis=1)
    rep = LANE // HEAD_DIM
    return jnp.tile(cos, (1, rep)), jnp.tile(s_hi, (1, rep)), jnp.tile(s_lo, (1, rep))


def _front(x, pos, norm1_g, w_in, b_gates, q_norm_g, k_norm_g, d_rnn):
    n, d_model = x.shape
    q_w, kv_w, qi_w = N_HEADS * HEAD_DIM, N_KV_HEADS * HEAD_DIM, N_IDX_HEADS * IDX_DIM
    sizes = [q_w, kv_w, kv_w, qi_w, IDX_DIM, N_IDX_HEADS, d_rnn, d_rnn, d_model, d_model]
    offs = np.cumsum([0] + sizes)
    parts = [w_in[:, offs[i]:offs[i + 1]] for i in range(len(sizes))]
    padl = lambda w: jnp.pad(w, ((0, 0), (0, LANE - w.shape[1])))
    wa = jnp.concatenate(parts[:4] + [padl(parts[4]), padl(parts[5])], axis=1).astype(bf16)
    wb = jnp.concatenate(parts[6:], axis=1).astype(bf16)
    cos, s_hi, s_lo = _rope_tables(pos)
    rep = LANE // HEAD_DIM
    head_of_lane = np.arange(LANE) // HEAD_DIM
    bd = jnp.asarray(head_of_lane[:, None] == head_of_lane[None, :], bf16)
    tm = TOKEN_TILE
    row = lambda w: pl.BlockSpec((tm, w), lambda i: (i, 0))
    outs = [(q_w, bf16), (kv_w, f32), (kv_w, f32), (qi_w, bf16), (IDX_DIM, f32), (LANE, f32),
            (d_rnn, f32), (d_rnn, f32), (d_model, f32), (d_model, f32), (kv_w, bf16), (IDX_DIM, bf16)]
    per_chunk = KEY_CHUNK // tm
    assert n % KEY_CHUNK == 0
    vt_spec = pl.BlockSpec((1, N_KV_HEADS, V_ROWS, tm), lambda i: (i // per_chunk, 0, 0, i % per_chunk))
    vt_shape = jax.ShapeDtypeStruct((n // KEY_CHUNK, N_KV_HEADS, V_ROWS, KEY_CHUNK), bf16)
    return pl.pallas_call(
        functools.partial(_front_kernel, d_model=d_model, d_rnn=d_rnn),
        grid=(n // tm,),
        in_specs=[row(d_model), _resident((1, d_model)), _resident(wa.shape), _resident(wb.shape),
                  _resident((1, 2 * d_model)), _resident((1, LANE)), _resident((1, LANE)),
                  row(LANE), row(LANE), row(LANE), _resident((LANE, LANE))],
        out_specs=[row(w) for w, _ in outs] + [vt_spec],
        out_shape=[jax.ShapeDtypeStruct((n, w), dt) for w, dt in outs] + [vt_shape],
        compiler_params=_params("parallel"),
        name="front",
    )(x, norm1_g.reshape(1, -1), wa, wb, b_gates.reshape(1, -1),
      jnp.tile(q_norm_g, rep).reshape(1, LANE), jnp.tile(k_norm_g, rep).reshape(1, LANE),
      cos, s_hi, s_lo, bd)


def _score_key(score):
    score = jnp.where(score == 0.0, 0.0, score)
    bits = pltpu.bitcast(score, i32)
    return bits ^ ((bits >> 31) & 0x7FFFFFFF)


def _kth_largest_key(count_ge, k, n_keys):
    c0 = count_ge(jnp.zeros(n_keys.shape, i32))
    ans = jnp.where(c0 >= k, 0, INT_MIN).astype(i32)
    cnt = jnp.where(c0 >= k, c0, n_keys)

    def unsettled(st):
        i, _, cnt = st
        return (i < 31) & (jnp.max(jnp.where((cnt == k) | (n_keys < k), 0, 1)) > 0)

    def body(st):
        i, ans, cnt = st
        cand = ans + lax.shift_left(jnp.int32(1), jnp.int32(30) - i)
        c = count_ge(cand)
        return i + 1, jnp.where(c >= k, cand, ans), jnp.where(c >= k, c, cnt)

    _, ans, cnt = lax.while_loop(unsettled, body, (jnp.int32(0), ans, cnt))
    return jnp.maximum(ans, INT_MIN + 1), cnt


def _prompt_attn_kernel(q_ref, qi_ref, wi_ref, ki_ref, k_ref, vt_ref, low_ref, o_ref, keys_ref, bias_ref,
                        m_ref, top_ref, kabs_ref, acc_ref, qz_ref, *, top_k):
    qb = pl.program_id(1)
    tq, ck, sub = Q_TILE, KEY_CHUNK, KEY_SUB
    n_chunks = (qb * tq + tq + ck - 1) // ck
    qpos = qb * tq + lax.broadcasted_iota(i32, (sub, tq), 1)
    krow = lax.broadcasted_iota(i32, (sub, tq), 0)

    qit = qi_ref[0].astype(f32).T.astype(bf16)
    w = wi_ref[0].T[0:N_IDX_HEADS, :] * (IDX_DIM ** -0.5)
    qt = q_ref[0].astype(f32).T.astype(bf16)
    kv_w = N_KV_HEADS * HEAD_DIM
    for g in range(N_KV_HEADS):
        q4 = jnp.concatenate([qt[(g * KV_GROUP + r) * HEAD_DIM:(g * KV_GROUP + r + 1) * HEAD_DIM, :]
                              for r in range(KV_GROUP)], axis=1)
        qz_ref[g] = jnp.zeros((kv_w, KV_GROUP * tq), bf16)
        qz_ref[g, g * HEAD_DIM:(g + 1) * HEAD_DIM, :] = q4

    def score_chunk(c, carry):
        for u in range(ck // sub):
            kc = ki_ref[0, c, u * sub:(u + 1) * sub, :]
            acc = jnp.zeros((sub, tq), f32)
            for h in range(N_IDX_HEADS):
                acc = acc + jnp.maximum(_dot(kc, qit[h * IDX_DIM:(h + 1) * IDX_DIM, :]), 0.0) * w[h:h + 1, :]
            causal = c * ck + u * sub + krow <= qpos
            keys_ref[c, u * sub:(u + 1) * sub, :] = jnp.where(causal, _score_key(acc), INT_MIN)
        return carry

    lax.fori_loop(0, n_chunks, score_chunk, 0)

    def count_ge(cand):
        lanes_of_acc = 8 * SUBLANE
        def body(c, acc):
            m = (keys_ref[c] >= cand).astype(f32)
            return acc + jnp.sum(m.reshape(ck // lanes_of_acc, lanes_of_acc, tq), axis=0)
        acc = lax.fori_loop(0, n_chunks, body, jnp.zeros((lanes_of_acc, tq), f32))
        return jnp.sum(acc, axis=0, keepdims=True)

    n_causal = (qb * tq + 1 + lax.broadcasted_iota(i32, (1, tq), 1)).astype(f32)
    thr, n_ge = _kth_largest_key(count_ge, float(top_k), n_causal)


    @pl.when(jnp.max(n_ge) > top_k)
    def _():
        need = top_k - count_ge(thr + 1)

        def body(c, before):
            kc = keys_ref[c]
            eq = kc == thr
            eqb = eq.astype(bf16)
            rank = _dot(low_ref[...], eqb) + before
            keys_ref[c] = jnp.where(eq & (rank >= need), INT_MIN, kc)
            return before + jnp.sum(eqb.astype(f32), axis=0, keepdims=True)

        lax.fori_loop(0, n_chunks, body, jnp.zeros((1, tq), f32))

    def to_bias(c, carry):
        bias_ref[c] = jnp.where(keys_ref[c] >= thr, 0.0, NEG_BIG)
        return carry

    lax.fori_loop(0, n_chunks, to_bias, 0)

    @pl.when(qb == 0)
    def _():
        def body(c, best):
            a = jnp.abs(k_ref[0, c].astype(f32))
            return jnp.maximum(best, jnp.max(a.reshape(ck // SUBLANE, SUBLANE, kv_w), axis=0))
        best = lax.fori_loop(0, k_ref.shape[1], body, jnp.zeros((SUBLANE, kv_w), f32))
        kabs_ref[...] = jnp.broadcast_to(jnp.max(best, axis=0, keepdims=True), (SUBLANE, kv_w))

    def sweep():
        acc_ref[...] = jnp.zeros(acc_ref.shape, f32)

        def accumulate(c, carry):
            bias = bias_ref[c]
            for g in range(N_KV_HEADS):
                s4 = _dot(k_ref[0, c], qz_ref[g])
                ps = [jnp.exp(s4[:, r * tq:(r + 1) * tq] + bias - top_ref[g * KV_GROUP + r]).astype(bf16)
                      for r in range(KV_GROUP)]
                acc_ref[g] = acc_ref[g] + _dot(vt_ref[c, g], jnp.concatenate(ps, axis=1))
            return carry

        lax.fori_loop(0, n_chunks, accumulate, 0)

    for g in range(N_KV_HEADS):
        bound = _dot(kabs_ref[...].astype(bf16), jnp.abs(qz_ref[g]))[0:1, :] * BOUND_MARGIN
        for r in range(KV_GROUP):
            top_ref[g * KV_GROUP + r] = bound[:, r * tq:(r + 1) * tq]
    sweep()
    smallest = acc_ref[0, HEAD_DIM:HEAD_DIM + 1, :]
    for g in range(1, N_KV_HEADS):
        smallest = jnp.minimum(smallest, acc_ref[g, HEAD_DIM:HEAD_DIM + 1, :])

    @pl.when(jnp.min(smallest) < MIN_TRUSTED_SUM)
    def _():
        part = m_ref.shape[1]
        m_ref[...] = jnp.full(m_ref.shape, NEG_BIG, f32)

        def col_max(c, carry):
            bias = bias_ref[c]
            for g in range(N_KV_HEADS):
                s4 = _dot(k_ref[0, c], qz_ref[g])
                for r in range(KV_GROUP):
                    h = g * KV_GROUP + r
                    s = (s4[:, r * tq:(r + 1) * tq] + bias).reshape(ck // part, part, tq)
                    m_ref[h] = jnp.maximum(m_ref[h], jnp.max(s, axis=0))
            return carry

        lax.fori_loop(0, n_chunks, col_max, 0)
        for h in range(N_HEADS):
            top_ref[h] = jnp.max(m_ref[h], axis=0, keepdims=True)
        sweep()

    heads = []
    for g in range(N_KV_HEADS):
        o4 = acc_ref[g, 0:HEAD_DIM, :] / acc_ref[g, HEAD_DIM:HEAD_DIM + 1, :]
        heads += [o4[:, r * tq:(r + 1) * tq] for r in range(KV_GROUP)]
    o_ref[0] = jnp.concatenate(heads, axis=0).T.astype(o_ref.dtype)


def _prompt_attention(q, qi, wi, ki, k, vt):
    b, s, _ = q.shape
    top_k = min(TOPK_MAX, s // 4)
    tq, ck = Q_TILE, KEY_CHUNK
    nq, nc = s // tq, s // ck
    gq = KV_GROUP * tq
    kv_w = N_KV_HEADS * HEAD_DIM
    low = jnp.asarray(np.arange(ck)[:, None] > np.arange(ck)[None, :], bf16)
    per_query = lambda w: pl.BlockSpec((1, tq, w), lambda i, j: (i, j, 0))
    return pl.pallas_call(
        functools.partial(_prompt_attn_kernel, top_k=top_k),
        grid=(b, nq),
        in_specs=[
            per_query(q.shape[2]), per_query(qi.shape[2]), per_query(wi.shape[2]),
            pl.BlockSpec((1, nc, ck, IDX_DIM), lambda i, j: (i, 0, 0, 0)),
            pl.BlockSpec((1, nc, ck, kv_w), lambda i, j: (i, 0, 0, 0)),
            pl.BlockSpec((nc, N_KV_HEADS, V_ROWS, ck), lambda i, j: (i, 0, 0, 0)),
            _resident((ck, ck)),
        ],
        out_specs=per_query(q.shape[2]),
        out_shape=jax.ShapeDtypeStruct(q.shape, bf16),
        scratch_shapes=[pltpu.VMEM((nc, ck, tq), i32), pltpu.VMEM((nc, ck, tq), f32),
                        pltpu.VMEM((N_HEADS, SOFTMAX_PARTIALS, tq), f32), pltpu.VMEM((N_HEADS, 1, tq), f32),
                        pltpu.VMEM((SUBLANE, kv_w), f32),
                        pltpu.VMEM((N_KV_HEADS, V_ROWS, gq), f32),
                        pltpu.VMEM((N_KV_HEADS, kv_w, gq), bf16)],
        compiler_params=_params("parallel", "arbitrary"),
        name="prompt_attention",
    )(q, qi, wi, ki.reshape(b, nc, ck, IDX_DIM), k.reshape(b, nc, ck, kv_w), vt, low)


def _sample_select_kernel(pt_ref, qi_ref, wi_ref, kin_ref, tri_ref, lt_ref, *rest, top_k, n_pages):
    page_refs, (sel_ref, seln_ref, keys_ref) = rest[:PAGES_PER_STEP], rest[PAGES_PER_STEP:]
    j = pl.program_id(1)
    qi = qi_ref[0]
    w = wi_ref[0]

    for i in range(PAGES_PER_STEP):
        s = _dot_nt(qi, page_refs[i][0].astype(bf16))
        score = jnp.sum(jnp.maximum(s, 0.0) * w, axis=0, keepdims=True)
        keys_ref[pl.ds(j * PAGES_PER_STEP + i, 1), :] = _score_key(score)

    @pl.when(j == pl.num_programs(1) - 1)
    def _():
        kn = kin_ref[0].astype(bf16).astype(f32)
        s_new = jnp.sum(qi.astype(f32) * kn, axis=1, keepdims=True)
        key_new = _score_key(jnp.sum(jnp.maximum(s_new, 0.0) * w, axis=0, keepdims=True))
        keys = keys_ref[...]

        def count_ge(cand):
            c = jnp.sum((keys >= cand).astype(f32), axis=1, keepdims=True)
            return jnp.sum(c, axis=0, keepdims=True) + (key_new >= cand).astype(f32)

        thr, _ = _kth_largest_key(count_ge, float(top_k), jnp.full((1, 1), n_pages * PAGE_SIZE + 1, f32))
        need = top_k - (count_ge(thr + 1))
        eq = keys == thr
        eqb = eq.astype(bf16)
        in_row = _dot(eqb, tri_ref[...])
        per_page = jnp.sum(eqb.astype(f32), axis=1, keepdims=True)
        before = _dot(lt_ref[...], per_page.astype(bf16) * jnp.ones((1, LANE), bf16))[:, 0:1]
        rank = in_row + before
        sel = (keys > thr) | (eq & (rank < need))
        sel_ref[0] = sel.astype(f32)
        n_eq = jnp.sum(per_page, axis=0, keepdims=True)
        sel_new = (key_new > thr) | ((key_new == thr) & (n_eq < need))
        seln_ref[0] = jnp.broadcast_to(sel_new.astype(f32), (1, LANE))


def _sample_attend_kernel(pt_ref, q_ref, sel_ref, seln_ref, kn_ref, vn_ref, gm_ref, *rest, n_pages):
    k_refs = rest[:PAGES_PER_STEP]
    v_refs = rest[PAGES_PER_STEP:2 * PAGES_PER_STEP]
    o_ref, m_ref, l_ref, acc_ref = rest[2 * PAGES_PER_STEP:]
    j = pl.program_id(1)
    q = q_ref[0]

    @pl.when(j == 0)
    def _():
        m_ref[...] = jnp.full(m_ref.shape, NEG_BIG, f32)
        l_ref[...] = jnp.zeros(l_ref.shape, f32)
        acc_ref[...] = jnp.zeros(acc_ref.shape, f32)

    def update(s, vals):
        m = m_ref[...]
        m_new = jnp.maximum(m, jnp.max(s, axis=1, keepdims=True))
        alpha = jnp.exp(m - m_new)
        p = jnp.exp(s - m_new)
        l_ref[...] = alpha * l_ref[...] + jnp.sum(p, axis=1, keepdims=True)
        acc_ref[...] = alpha * acc_ref[...] + vals(p)
        m_ref[...] = m_new

    for i in range(PAGES_PER_STEP):
        sel = sel_ref[0, pl.ds(j * PAGES_PER_STEP + i, 1), :] > 0.5
        s = jnp.where(sel, _dot_nt(q, k_refs[i][0].astype(bf16)), NEG_BIG)
        update(s, lambda p, i=i: _dot(p.astype(bf16), v_refs[i][0].astype(bf16)))

    @pl.when(j == pl.num_programs(1) - 1)
    def _():
        kn = kn_ref[0].astype(bf16).astype(f32)
        vn = vn_ref[0].astype(bf16).astype(f32)
        s = jnp.sum(q.astype(f32) * kn, axis=1, keepdims=True)
        s = jnp.where(seln_ref[0][:, 0:1] > 0.5, s, NEG_BIG)
        update(s, lambda p: p.astype(bf16).astype(f32) * vn)
        full = acc_ref[...] / l_ref[...] * gm_ref[...]
        out = full[:, 0:HEAD_DIM]
        for g in range(1, N_KV_HEADS):
            out = out + full[:, g * HEAD_DIM:(g + 1) * HEAD_DIM]
        o_ref[0] = out.astype(o_ref.dtype)


def _sample_attention(q, k_new, v_new, qi, ki_new, wi, cache_k, cache_v, cache_kidx, page_table):
    db = q.shape[0]
    n_pages = page_table.shape[1]
    past = n_pages * PAGE_SIZE
    top_k = min(TOPK_MAX, (past + 1) // 4)
    pps = PAGES_PER_STEP
    steps = n_pages // pps
    kv_w = N_KV_HEADS * HEAD_DIM
    n_pool = cache_k.shape[0]
    ck2 = cache_k.reshape(n_pool, PAGE_SIZE, kv_w)
    cv2 = cache_v.reshape(n_pool, PAGE_SIZE, kv_w)

    def page_spec(width, i):
        return pl.BlockSpec((1, PAGE_SIZE, width), lambda b, j, pt, i=i: (pt[b, j * pps + i], 0, 0))

    per_seq = lambda *shape: pl.BlockSpec((1,) + shape, lambda b, j, pt: (b,) + (0,) * len(shape))
    const = lambda *shape: pl.BlockSpec(shape, lambda b, j, pt: (0,) * len(shape))
    tri = jnp.asarray(np.arange(PAGE_SIZE)[:, None] < np.arange(PAGE_SIZE)[None, :], bf16)
    lower = jnp.asarray(np.arange(n_pages)[:, None] > np.arange(n_pages)[None, :], bf16)

    sel, sel_new = pl.pallas_call(
        functools.partial(_sample_select_kernel, top_k=top_k, n_pages=n_pages),
        grid_spec=pltpu.PrefetchScalarGridSpec(
            num_scalar_prefetch=1,
            grid=(db, steps),
            in_specs=[per_seq(N_IDX_HEADS, IDX_DIM), per_seq(N_IDX_HEADS, 1), per_seq(1, IDX_DIM),
                      const(PAGE_SIZE, PAGE_SIZE), const(n_pages, n_pages)]
                     + [page_spec(IDX_DIM, i) for i in range(pps)],
            out_specs=[per_seq(n_pages, PAGE_SIZE), per_seq(1, LANE)],
            scratch_shapes=[pltpu.VMEM((n_pages, PAGE_SIZE), i32)],
        ),
        out_shape=[jax.ShapeDtypeStruct((db, n_pages, PAGE_SIZE), f32), jax.ShapeDtypeStruct((db, 1, LANE), f32)],
        compiler_params=_params("parallel", "arbitrary"),
        name="sample_select",
    )(page_table, qi.reshape(db, N_IDX_HEADS, IDX_DIM), (wi * (IDX_DIM ** -0.5)).reshape(db, N_IDX_HEADS, 1),
      ki_new.reshape(db, 1, IDX_DIM), tri, lower, *([cache_kidx] * pps))

    group_mask = np.repeat(np.arange(N_HEADS)[:, None] // KV_GROUP == np.arange(N_KV_HEADS)[None, :], HEAD_DIM, axis=1)
    q_bd = jnp.where(group_mask[None], jnp.tile(q.reshape(db, N_HEADS, HEAD_DIM), (1, 1, N_KV_HEADS)), 0).astype(bf16)
    out = pl.pallas_call(
        functools.partial(_sample_attend_kernel, n_pages=n_pages),
        grid_spec=pltpu.PrefetchScalarGridSpec(
            num_scalar_prefetch=1,
            grid=(db, steps),
            in_specs=[per_seq(N_HEADS, kv_w), per_seq(n_pages, PAGE_SIZE), per_seq(1, LANE),
                      per_seq(1, kv_w), per_seq(1, kv_w), const(N_HEADS, kv_w)]
                     + [page_spec(kv_w, i) for i in range(pps)] * 2,
            out_specs=per_seq(N_HEADS, HEAD_DIM),
            scratch_shapes=[pltpu.VMEM((N_HEADS, 1), f32), pltpu.VMEM((N_HEADS, 1), f32), pltpu.VMEM((N_HEADS, kv_w), f32)],
        ),
        out_shape=jax.ShapeDtypeStruct((db, N_HEADS, HEAD_DIM), bf16),
        compiler_params=_params("parallel", "arbitrary"),
        name="sample_attend",
    )(page_table, q_bd, sel, sel_new, k_new.reshape(db, 1, kv_w), v_new.reshape(db, 1, kv_w),
      jnp.asarray(group_mask, f32), *([ck2] * pps), *([cv2] * pps))
    return out.reshape(db, N_HEADS * HEAD_DIM)


def _rglru_coeffs(xc, wr_ref, br_ref, wi_ref, bi_ref, lam_ref):
    xb = xc.astype(bf16)
    r = jax.nn.sigmoid(_dot(xb, wr_ref[...]) + br_ref[...])
    i = jax.nn.sigmoid(_dot(xb, wi_ref[...]) + bi_ref[...])
    z = -lam_ref[...]
    softplus = jnp.maximum(z, 0.0) + jnp.log1p(jnp.exp(-jnp.abs(z)))
    log_a = -LRU_C * r * softplus
    a = jnp.exp(log_a)
    th = jnp.tanh(log_a)
    b = jnp.sqrt(-2.0 * th / (1.0 - th)) * (i * xc)
    return a, b


def _rglru_seq_kernel(xr_ref, yr_ref, cw_ref, cb_ref, wr_ref, br_ref, wi_ref, bi_ref, lam_ref,
                      o_ref, hl_ref, xprev_ref, h_ref, a_ref, b_ref):
    t = pl.program_id(1)
    tc, d = xr_ref.shape[1], xr_ref.shape[2]

    @pl.when(t == 0)
    def _():
        xprev_ref[...] = jnp.zeros(xprev_ref.shape, f32)
        h_ref[...] = jnp.zeros(h_ref.shape, f32)

    xr = xr_ref[0]
    xcat = jnp.concatenate([xprev_ref[...], xr], axis=0)
    xc = cb_ref[...] + cw_ref[0:1, :] * xcat[SUBLANE - 3:SUBLANE - 3 + tc]
    for j in range(1, CONV_W):
        xc = xc + cw_ref[j:j + 1, :] * xcat[SUBLANE - 3 + j:SUBLANE - 3 + j + tc]
    xprev_ref[...] = xr[tc - SUBLANE:, :]
    a, b = _rglru_coeffs(xc, wr_ref, br_ref, wi_ref, bi_ref, lam_ref)
    a_ref[...] = a
    b_ref[...] = b
    row = lax.broadcasted_iota(i32, (SUBLANE, d), 0)

    def slab(s, h_prev):
        r0 = pl.multiple_of(s * SUBLANE, SUBLANE)
        a8 = a_ref[pl.ds(r0, SUBLANE), :]
        b8 = b_ref[pl.ds(r0, SUBLANE), :]
        for sh in (1, 2, 4):
            keep = row >= sh
            b8 = jnp.where(keep, a8 * pltpu.roll(b8, sh, 0) + b8, b8)
            a8 = jnp.where(keep, a8 * pltpu.roll(a8, sh, 0), a8)
        h8 = a8 * h_prev + b8
        o_ref[0, pl.ds(r0, SUBLANE), :] = (h8 * _gelu(yr_ref[0, pl.ds(r0, SUBLANE), :])).astype(o_ref.dtype)
        return h8[SUBLANE - 1:SUBLANE, :]

    h_last = lax.fori_loop(0, tc // SUBLANE, slab, h_ref[...])
    h_ref[...] = h_last
    hl_ref[0] = h_last


def _block_diag(w):
    n, a, b = w.shape
    eye = jnp.eye(n, dtype=w.dtype)
    return (eye[:, None, :, None] * w[:, :, None, :]).reshape(n * a, n * b).astype(bf16)


def _rglru_prompt(xr, yr, conv_w, conv_b, w_rgate, b_rgate, w_igate, b_igate, lru_lambda):
    b, t, d = xr.shape
    tc = RNN_CHUNK
    vec = lambda v: v.reshape(1, d)
    seq = pl.BlockSpec((1, tc, d), lambda i, j: (i, j, 0))
    out, h_last = pl.pallas_call(
        _rglru_seq_kernel,
        grid=(b, t // tc),
        in_specs=[seq, seq, _resident((CONV_W, d)), _resident((1, d)), _resident((d, d)), _resident((1, d)),
                  _resident((d, d)), _resident((1, d)), _resident((1, d))],
        out_specs=[seq, pl.BlockSpec((1, 1, d), lambda i, j: (i, 0, 0))],
        out_shape=[jax.ShapeDtypeStruct((b, t, d), bf16), jax.ShapeDtypeStruct((b, 1, d), f32)],
        scratch_shapes=[pltpu.VMEM((SUBLANE, d), f32), pltpu.VMEM((1, d), f32),
                        pltpu.VMEM((tc, d), f32), pltpu.VMEM((tc, d), f32)],
        compiler_params=_params("parallel", "arbitrary"),
        name="rglru_prompt",
    )(xr, yr, conv_w, vec(conv_b), _block_diag(w_rgate), vec(b_rgate), _block_diag(w_igate), vec(b_igate),
      vec(lru_lambda))
    return out, h_last.reshape(b, d)


def _rglru_step_kernel(xr_ref, yr_ref, c0_ref, c1_ref, c2_ref, h0_ref, cw_ref, cb_ref, wr_ref, br_ref, wi_ref,
                       bi_ref, lam_ref, o_ref, h_ref):
    xc = cb_ref[...] + cw_ref[0:1, :] * c0_ref[...]
    xc = xc + cw_ref[1:2, :] * c1_ref[...]
    xc = xc + cw_ref[2:3, :] * c2_ref[...]
    xc = xc + cw_ref[3:4, :] * xr_ref[...]
    a, b = _rglru_coeffs(xc, wr_ref, br_ref, wi_ref, bi_ref, lam_ref)
    h = a * h0_ref[...] + b
    h_ref[...] = h
    o_ref[...] = (h * _gelu(yr_ref[...])).astype(o_ref.dtype)


def _rglru_sample(xr, yr, state_conv, h0, conv_w, conv_b, w_rgate, b_rgate, w_igate, b_igate, lru_lambda):
    db, d = xr.shape
    vec = lambda v: v.reshape(1, d)
    return pl.pallas_call(
        _rglru_step_kernel,
        out_shape=[jax.ShapeDtypeStruct((db, d), bf16), jax.ShapeDtypeStruct((db, d), f32)],
        compiler_params=pltpu.CompilerParams(vmem_limit_bytes=VMEM_LIMIT_BYTES),
        name="rglru_sample",
    )(xr, yr, state_conv[:, 0], state_conv[:, 1], state_conv[:, 2], h0, conv_w, vec(conv_b),
      _block_diag(w_rgate), vec(b_rgate), _block_diag(w_igate), vec(b_igate), vec(lru_lambda))


def _top_rows(x, k, payload=None):
    n = x.shape[0]
    row = lax.broadcasted_iota(i32, x.shape, 0)
    vals, picked = [], []
    for _ in range(k):
        m = jnp.max(x, axis=0, keepdims=True)
        r = jnp.min(jnp.where(x == m, row, n), axis=0, keepdims=True)
        hit = row == r
        vals.append(m)
        picked.append(r if payload is None else jnp.sum(jnp.where(hit, payload, 0), axis=0, keepdims=True))
        x = jnp.where(hit, -jnp.inf, x)
    return jnp.concatenate(vals, axis=0), jnp.concatenate(picked, axis=0)


_PAIR_COUNTS = [PEER_TOPK // (r1 + 1) for r1 in range(PEER_TOPK)]


def _back_kernel(x_ref, ao_ref, ro_ref, ga_ref, gr_ref, woa_ref, wor_ref, wout_ref, g2_ref, wqt_ref, k1_ref, k2_ref,
                 x1_ref, hn_ref, eidx_ref, gate_ref):
    mixed = ga_ref[...] * _dot(ao_ref[...], woa_ref[...]) + gr_ref[...] * _dot(ro_ref[...], wor_ref[...])
    x1 = x_ref[...] + _dot(mixed.astype(bf16), wout_ref[...])
    x1_ref[...] = x1
    hn = (x1 * lax.rsqrt(jnp.mean(x1 * x1, axis=-1, keepdims=True) + NORM_EPS) * g2_ref[...]).astype(bf16)
    hn_ref[...] = hn.astype(f32)
    qt = _dot_nt(wqt_ref[...], hn).astype(bf16)
    half = PEER_KEY_DIM // 2
    for h in range(PEER_HEADS):
        base = h * PEER_KEY_DIM
        v1, i1 = _top_rows(_dot(k1_ref[...], qt[base:base + half, :]), PEER_TOPK)
        v2, i2 = _top_rows(_dot(k2_ref[...], qt[base + half:base + PEER_KEY_DIM, :]), PEER_TOPK)
        cand, cid = [], []
        for r1, n2 in enumerate(_PAIR_COUNTS):
            cand.append(v1[r1:r1 + 1, :] + v2[0:n2, :])
            cid.append(i1[r1:r1 + 1, :] * N_KEYS + i2[0:n2, :])
        pad = -sum(_PAIR_COUNTS) % SUBLANE
        tm = v1.shape[1]
        cand.append(jnp.full((pad, tm), -jnp.inf, f32))
        cid.append(jnp.zeros((pad, tm), i32))
        sv, eidx = _top_rows(jnp.concatenate(cand, axis=0), PEER_TOPK, jnp.concatenate(cid, axis=0))
        p = jnp.exp(sv - sv[0:1, :])
        eidx_ref[h * PEER_TOPK:(h + 1) * PEER_TOPK, :] = eidx
        gate_ref[h * PEER_TOPK:(h + 1) * PEER_TOPK, :] = p / jnp.sum(p, axis=0, keepdims=True)


def _back(x, attn_o, rnn_o, g_a, g_r, w_o_attn, w_o_rnn, w_out, norm2_g, w_peer_q, peer_k1, peer_k2):
    n, d = x.shape
    tm = TOKEN_TILE
    row = lambda w: pl.BlockSpec((tm, w), lambda i: (i, 0))
    col = pl.BlockSpec((PEER_SLOTS, tm), lambda i: (0, i))
    ws = [w_o_attn.astype(bf16), w_o_rnn.astype(bf16), w_out.astype(bf16), norm2_g.reshape(1, d),
          w_peer_q.T.astype(bf16), peer_k1.astype(bf16), peer_k2.astype(bf16)]
    return pl.pallas_call(
        _back_kernel,
        grid=(n // tm,),
        in_specs=[row(d), row(attn_o.shape[1]), row(rnn_o.shape[1]), row(d), row(d)] + [_resident(w.shape) for w in ws],
        out_specs=[row(d), row(d), col, col],
        out_shape=[jax.ShapeDtypeStruct((n, d), f32), jax.ShapeDtypeStruct((n, d), f32),
                   jax.ShapeDtypeStruct((PEER_SLOTS, n), i32), jax.ShapeDtypeStruct((PEER_SLOTS, n), f32)],
        compiler_params=_params("parallel"),
        name="back",
    )(x, attn_o, rnn_o, g_a, g_r, *ws)


def _pack_rows(table, reverse_pieces=False):
    e, d = table.shape
    bits = lax.bitcast_convert_type(table.astype(bf16), jnp.uint16).astype(jnp.uint32).reshape(e // 2, 2, d // LANE, LANE)
    if reverse_pieces:
        bits = bits[:, :, ::-1, :]
    return lax.bitcast_convert_type(bits[:, 0] | (bits[:, 1] << 16), i32).reshape(e // 2 * (d // LANE), LANE)


def _gather_tiles(table_ref, row_ref, t):
    starts = row_ref.at[t]
    tiles = [table_ref[pl.ds(pl.multiple_of(starts[s], SUBLANE), SUBLANE), :] for s in range(PEER_SLOTS)]
    return pltpu.bitcast(jnp.concatenate(tiles, axis=0), bf16)


def _token_tiles(x8):
    tiles = x8.reshape(SUBLANE, x8.shape[1] // LANE, LANE)
    return [tiles[j] for j in range(SUBLANE)]


def _peer_act_kernel(start_ref, hn_ref, gate_ref, u_ref, coef_ref):
    row = lax.broadcasted_iota(i32, (SUBLANE, LANE), 0)
    n_blk = PEER_X // LANE

    def token_group(gi, carry):
        t0 = pl.multiple_of(gi * SUBLANE, SUBLANE)
        acts = [jnp.zeros((SUBLANE, LANE), f32) for _ in range(n_blk)]
        h_tiles = _token_tiles(hn_ref[pl.ds(t0, SUBLANE), :])
        for j in range(SUBLANE):
            rows = _gather_tiles(u_ref, start_ref, t0 + j)
            r = _dot_nt(h_tiles[j].astype(bf16), rows)
            for blk in range(n_blk):
                z = pltpu.roll(r[:, blk * LANE:(blk + 1) * LANE], 0, 1, stride=2, stride_axis=0)
                for sh in (4, 2, 1):
                    z = z + pltpu.roll(z, sh, 0)
                acts[blk] = jnp.where(row == j, z, acts[blk])
        for blk in range(n_blk):
            cols = slice(blk * LANE, (blk + 1) * LANE)
            coef = gate_ref[pl.ds(t0, SUBLANE), cols] * _gelu(acts[blk])
            coef_ref[pl.ds(t0, SUBLANE), cols] = pltpu.roll(coef, LANE - (PACKED_ROWS - 2), 1)
        return carry

    lax.fori_loop(0, hn_ref.shape[0] // SUBLANE, token_group, 0)


def _peer_out_kernel(start_ref, coef_ref, x1_ref, v_ref, y_ref):
    n_blk = PEER_X // LANE

    def token_group(gi, carry):
        t0 = pl.multiple_of(gi * SUBLANE, SUBLANE)
        coef = coef_ref[pl.ds(t0, SUBLANE), :]
        outs = []
        for j in range(SUBLANE):
            rows = _gather_tiles(v_ref, start_ref, t0 + j)
            pick = [pltpu.roll(jnp.broadcast_to(coef[j:j + 1, blk * LANE:(blk + 1) * LANE], (SUBLANE, LANE)),
                               0, 1, stride=2, stride_axis=0) for blk in range(n_blk)]
            outs.append(_dot(jnp.concatenate(pick, axis=1).astype(bf16), rows))
        x1 = x1_ref[pl.ds(t0, SUBLANE), :]
        y_ref[pl.ds(t0, SUBLANE), :] = x1 + jnp.stack(outs, axis=0).reshape(x1.shape)
        return carry

    lax.fori_loop(0, x1_ref.shape[0] // SUBLANE, token_group, 0)


def _peer(hn, x1, eidx_t, gate_t, peer_u, peer_v):
    n, d = x1.shape
    tt = PEER_TOKENS
    rows = d // LANE
    eidx = eidx_t.T
    start = (eidx >> 1) * SUBLANE
    col = jnp.tile(jnp.arange(PACKED_ROWS, dtype=i32), PEER_SLOTS)[None, :]
    gate = jnp.where(col == PACKED_ROWS - 2 + jnp.repeat(eidx & 1, PACKED_ROWS, axis=1),
                     jnp.repeat(gate_t.T, PACKED_ROWS, axis=1), 0.0)
    smem = pl.BlockSpec((tt, PEER_SLOTS), lambda i: (i, 0), memory_space=pltpu.SMEM)
    tokens = pl.BlockSpec((tt, d), lambda i: (i, 0))
    wide = pl.BlockSpec((tt, PEER_X), lambda i: (i, 0))
    table = _resident((peer_u.shape[0] // 2 * rows, LANE))
    coef = pl.pallas_call(
        _peer_act_kernel,
        grid=(n // tt,),
        in_specs=[smem, tokens, wide, table],
        out_specs=wide,
        out_shape=jax.ShapeDtypeStruct((n, PEER_X), f32),
        compiler_params=_params("parallel"),
        name="peer_act",
    )(start, hn, gate, _pack_rows(peer_u, reverse_pieces=True))
    y = pl.pallas_call(
        _peer_out_kernel,
        grid=(n // tt,),
        in_specs=[smem, wide, tokens, table],
        out_specs=tokens,
        out_shape=jax.ShapeDtypeStruct((n, d), f32),
        compiler_params=_params("parallel"),
        name="peer_out",
    )(start, coef, x1, _pack_rows(peer_v))
    return y


def kernel(x_prompt, x_sample, cache_k, cache_v, cache_kidx, page_table, state_conv, state_rglru, norm1_g, w_in, b_gates, q_norm_g, k_norm_g, conv_w, conv_b, w_rgate, b_rgate, w_igate, b_igate, lru_lambda, w_o_attn, w_o_rnn, w_out, norm2_g, w_peer_q, peer_k1, peer_k2, peer_u, peer_v):
    bp, sp, d = x_prompt.shape
    db, ts, _ = x_sample.shape
    assert ts == 1, "the sample path handles one new token per sequence"
    d_rnn = conv_w.shape[1]
    past = page_table.shape[1] * PAGE_SIZE
    n_p, n_s = bp * sp, db * ts
    assert n_p % TOKEN_TILE == 0
    n_sp = -(-n_s // KEY_CHUNK) * KEY_CHUNK
    rnn_w = (conv_w, conv_b, w_rgate, b_rgate, w_igate, b_igate, lru_lambda)
    front_w = (norm1_g, w_in, b_gates, q_norm_g, k_norm_g, d_rnn)
    back_w = (w_o_attn, w_o_rnn, w_out, norm2_g, w_peer_q, peer_k1, peer_k2)
    pad = lambda a: jnp.concatenate([a, jnp.zeros((n_sp - n_s,) + a.shape[1:], a.dtype)])

    xp = x_prompt.reshape(n_p, d)
    q, k_p, v_p, qi, ki_p, wi, xr_p, yr, g_a, g_r, kb, kib, vt = _front(
        xp, jnp.tile(jnp.arange(sp, dtype=f32), bp), *front_w)
    pr = lambda a: a.reshape(bp, sp, a.shape[1])
    attn_p = _prompt_attention(pr(q), pr(qi), pr(wi), pr(kib), pr(kb), vt)
    rnn_p, h_p = _rglru_prompt(pr(xr_p), pr(yr), *rnn_w)
    x1, hn, eidx_t, gate_t = _back(xp, attn_p.reshape(n_p, -1), rnn_p.reshape(n_p, d_rnn), g_a, g_r, *back_w)
    y_p = _peer(hn, x1, eidx_t, gate_t, peer_u, peer_v)

    xs = pad(x_sample.reshape(n_s, d))
    pos_s = pad(past + jnp.tile(jnp.arange(ts, dtype=f32), db))
    q, k_s, v_s, qi, ki_s, wi, xr_s, yr, g_a, g_r, _, _, _ = _front(xs, pos_s, *front_w)
    sm = lambda a: a[:n_s]
    attn_s = _sample_attention(sm(q), sm(k_s), sm(v_s), sm(qi), sm(ki_s), sm(wi[:, :N_IDX_HEADS]),
                               cache_k, cache_v, cache_kidx, page_table)
    rnn_s, h_s = _rglru_sample(sm(xr_s), sm(yr), state_conv, state_rglru, *rnn_w)
    x1, hn, eidx_t, gate_t = _back(xs, pad(attn_s), pad(rnn_s), g_a, g_r, *back_w)
    y_s = _peer(hn, x1, eidx_t, gate_t, peer_u, peer_v)

    conv_p = pr(xr_p)[:, sp - (CONV_W - 1):, :]
    conv_s = jnp.concatenate([state_conv[:, ts:, :], sm(xr_s).reshape(db, ts, d_rnn)], axis=1)
    heads = lambda a, b_, t_: a.reshape(b_, t_, N_KV_HEADS, HEAD_DIM)
    return (y_p.reshape(bp, sp, d), sm(y_s).reshape(db, ts, d),
            heads(k_p, bp, sp), heads(v_p, bp, sp), pr(ki_p),
            conv_p, h_p,
            heads(sm(k_s), db, ts), heads(sm(v_s), db, ts), sm(ki_s).reshape(db, ts, IDX_DIM),
            conv_s, h_s)
```

```python
import functools

import numpy as np
import jax
import jax.numpy as jnp
from jax import lax
from jax.experimental import pallas as pl
from jax.experimental.pallas import tpu as pltpu

f32 = jnp.float32
bf16 = jnp.bfloat16
i32 = jnp.int32

N_HEADS = 16
HEAD_DIM = 64
N_KV_HEADS = 4
KV_GROUP = N_HEADS // N_KV_HEADS
ROT_DIM = HEAD_DIM // 4
ROT_HALF = ROT_DIM // 2
ROPE_THETA = 500000.0
QK_EPS = 1e-6
N_IDX_HEADS = 8
IDX_DIM = 64
TOPK_MAX = 256
PAGE_SIZE = 128
CONV_W = 4
LRU_C = 8.0
PEER_HEADS = 8
PEER_KEY_DIM = 128
N_KEYS = 128
PEER_TOPK = 16
PEER_SLOTS = PEER_HEADS * PEER_TOPK
NORM_EPS = 1e-6

LANE = 128
SUBLANE = 8
PACKED_ROWS = 16
VMEM_LIMIT_BYTES = 56 * 1024 * 1024

INT_MIN = -(2 ** 31)
NEG_BIG = -1e30

TOKEN_TILE = 256
Q_TILE = LANE
KEY_CHUNK = 512
KEY_SUB = 256
SOFTMAX_PARTIALS = 32
V_ROWS = HEAD_DIM + PACKED_ROWS
BOUND_MARGIN = 1.02
MIN_TRUSTED_SUM = 1e-25
RNN_CHUNK = 256
PEER_TOKENS = 64
PAGES_PER_STEP = 8
PEER_X = PEER_SLOTS * PACKED_ROWS


def _params(*sem):
    return pltpu.CompilerParams(dimension_semantics=sem, vmem_limit_bytes=VMEM_LIMIT_BYTES)


def _resident(shape):
    nd = len(shape)
    return pl.BlockSpec(shape, lambda *_: (0,) * nd, pipeline_mode=pl.Buffered(1))


def _gelu(x):
    return 0.5 * x * (1.0 + jnp.tanh(np.sqrt(2.0 / np.pi).astype(np.float32) * (x + 0.044715 * (x * x * x))))


def _dot(a, b):
    return jnp.dot(a, b, preferred_element_type=f32)


def _dot_nt(a, b):
    return lax.dot_general(a, b, (((1,), (1,)), ((), ())), preferred_element_type=f32)


def _front_kernel(x_ref, g1_ref, wa_ref, wb_ref, bg_ref, qg_ref, kg_ref, c_ref, s1_ref, s2_ref, bd_ref,
                  q_ref, k_ref, v_ref, qi_ref, ki_ref, wi_ref, xr_ref, yr_ref, ga_ref, gr_ref,
                  kb_ref, kib_ref, vt_ref, *, d_model, d_rnn):
    x = x_ref[...]
    h = (x * lax.rsqrt(jnp.mean(x * x, axis=-1, keepdims=True) + NORM_EPS) * g1_ref[...]).astype(bf16)
    cos, sin_hi, sin_lo = c_ref[...], s1_ref[...], s2_ref[...]
    bd = bd_ref[...]

    def rope(t):
        return t * cos + pltpu.roll(t, LANE - ROT_HALF, 1) * sin_hi + pltpu.roll(t, ROT_HALF, 1) * sin_lo

    def head_norm(t, g):
        sq = t * t
        hi = sq.astype(bf16)
        lo = (sq - hi.astype(f32)).astype(bf16)
        ss = _dot(hi, bd) + _dot(lo, bd)
        return t * lax.rsqrt(ss * (1.0 / HEAD_DIM) + QK_EPS) * g

    q_w = N_HEADS * HEAD_DIM
    kv_w = N_KV_HEADS * HEAD_DIM
    qi_w = N_IDX_HEADS * IDX_DIM
    pa = _dot(h, wa_ref[...])
    off = 0
    for j in range(q_w // LANE):
        t = pa[:, off + j * LANE: off + (j + 1) * LANE]
        q_ref[:, j * LANE:(j + 1) * LANE] = (rope(head_norm(t, qg_ref[...])) * (HEAD_DIM ** -0.5)).astype(bf16)
    off += q_w
    for j in range(kv_w // LANE):
        t = pa[:, off + j * LANE: off + (j + 1) * LANE]
        kt = rope(head_norm(t, kg_ref[...]))
        for u in range(LANE // HEAD_DIM):
            k_ref[:, j * (LANE // HEAD_DIM) + u, :] = kt[:, u * HEAD_DIM:(u + 1) * HEAD_DIM]
        kb_ref[:, j * LANE:(j + 1) * LANE] = kt.astype(bf16)
    off += kv_w
    vv = pa[:, off:off + kv_w]
    for g in range(N_KV_HEADS):
        v_ref[:, g, :] = vv[:, g * HEAD_DIM:(g + 1) * HEAD_DIM]
    vt = vv.T.astype(bf16)
    ones_row = (lax.broadcasted_iota(i32, (V_ROWS - HEAD_DIM, vv.shape[0]), 0) == 0).astype(bf16)
    for g in range(N_KV_HEADS):
        vt_ref[0, g, 0:HEAD_DIM, :] = vt[g * HEAD_DIM:(g + 1) * HEAD_DIM, :]
        vt_ref[0, g, HEAD_DIM:V_ROWS, :] = ones_row
    off += kv_w
    for j in range(qi_w // LANE):
        t = pa[:, off + j * LANE: off + (j + 1) * LANE]
        qi_ref[:, j * LANE:(j + 1) * LANE] = rope(t).astype(bf16)
    off += qi_w
    kiv = rope(pa[:, off:off + LANE])[:, :IDX_DIM]
    ki_ref[...] = kiv
    kib_ref[...] = kiv.astype(bf16)
    off += LANE
    wi_ref[...] = pa[:, off:off + LANE] * (N_IDX_HEADS ** -0.5)

    xr_ref[...] = _dot(h, wb_ref[:, 0:d_rnn])
    yr_ref[...] = _dot(h, wb_ref[:, d_rnn:2 * d_rnn])
    o = 2 * d_rnn
    ga_ref[...] = jax.nn.sigmoid(_dot(h, wb_ref[:, o:o + d_model]) + bg_ref[:, 0:d_model])
    gr_ref[...] = jax.nn.sigmoid(_dot(h, wb_ref[:, o + d_model:o + 2 * d_model]) + bg_ref[:, d_model:2 * d_model])


def _rope_tables(pos):
    inv = ROPE_THETA ** (-jnp.arange(ROT_HALF, dtype=f32) / ROT_HALF)
    ang = pos.astype(f32)[:, None] * inv[None, :]
    c, s = jnp.cos(ang), jnp.sin(ang)
    n = pos.shape[0]
    z = lambda w: jnp.zeros((n, w), f32)
    cos = jnp.concatenate([c, c, jnp.ones((n, HEAD_DIM - ROT_DIM), f32)], axis=1)
    s_hi = jnp.concatenate([-s, z(HEAD_DIM - ROT_HALF)], axis=1)
    s_lo = jnp.concatenate([z(ROT_HALF), s, z(HEAD_DIM - ROT_DIM)], axis=1)
    rep = LANE // HEAD_DIM
    return jnp.tile(cos, (1, rep)), jnp.tile(s_hi, (1, rep)), jnp.tile(s_lo, (1, rep))


def _front(x, pos, norm1_g, w_in, b_gates, q_norm_g, k_norm_g, d_rnn):
    n, d_model = x.shape
    q_w, kv_w, qi_w = N_HEADS * HEAD_DIM, N_KV_HEADS * HEAD_DIM, N_IDX_HEADS * IDX_DIM
    sizes = [q_w, kv_w, kv_w, qi_w, IDX_DIM, N_IDX_HEADS, d_rnn, d_rnn, d_model, d_model]
    offs = np.cumsum([0] + sizes)
    parts = [w_in[:, offs[i]:offs[i + 1]] for i in range(len(sizes))]
    padl = lambda w: jnp.pad(w, ((0, 0), (0, LANE - w.shape[1])))
    wa = jnp.concatenate(parts[:4] + [padl(parts[4]), padl(parts[5])], axis=1).astype(bf16)
    wb = jnp.concatenate(parts[6:], axis=1).astype(bf16)
    cos, s_hi, s_lo = _rope_tables(pos)
    rep = LANE // HEAD_DIM
    head_of_lane = np.arange(LANE) // HEAD_DIM
    bd = jnp.asarray(head_of_lane[:, None] == head_of_lane[None, :], bf16)
    tm = TOKEN_TILE
    row = lambda w: pl.BlockSpec((tm, w), lambda i: (i, 0))
    outs = [(q_w, bf16), (kv_w, f32), (kv_w, f32), (qi_w, bf16), (IDX_DIM, f32), (LANE, f32),
            (d_rnn, f32), (d_rnn, f32), (d_model, f32), (d_model, f32), (kv_w, bf16), (IDX_DIM, bf16)]
    per_chunk = KEY_CHUNK // tm
    assert n % KEY_CHUNK == 0
    per_head = pl.BlockSpec((tm, N_KV_HEADS, HEAD_DIM), lambda i: (i, 0, 0))
    vt_spec = pl.BlockSpec((1, N_KV_HEADS, V_ROWS, tm), lambda i: (i // per_chunk, 0, 0, i % per_chunk))
    vt_shape = jax.ShapeDtypeStruct((n // KEY_CHUNK, N_KV_HEADS, V_ROWS, KEY_CHUNK), bf16)
    return pl.pallas_call(
        functools.partial(_front_kernel, d_model=d_model, d_rnn=d_rnn),
        grid=(n // tm,),
        in_specs=[row(d_model), _resident((1, d_model)), _resident(wa.shape), _resident(wb.shape),
                  _resident((1, 2 * d_model)), _resident((1, LANE)), _resident((1, LANE)),
                  row(LANE), row(LANE), row(LANE), _resident((LANE, LANE))],
        out_specs=[per_head if i in (1, 2) else row(w) for i, (w, _) in enumerate(outs)] + [vt_spec],
        out_shape=[jax.ShapeDtypeStruct((n, N_KV_HEADS, HEAD_DIM) if i in (1, 2) else (n, w), dt)
                   for i, (w, dt) in enumerate(outs)] + [vt_shape],
        compiler_params=_params("parallel"),
        name="front",
    )(x, norm1_g.reshape(1, -1), wa, wb, b_gates.reshape(1, -1),
      jnp.tile(q_norm_g, rep).reshape(1, LANE), jnp.tile(k_norm_g, rep).reshape(1, LANE),
      cos, s_hi, s_lo, bd)


def _score_key(score):
    score = jnp.where(score == 0.0, 0.0, score)
    bits = pltpu.bitcast(score, i32)
    return bits ^ ((bits >> 31) & 0x7FFFFFFF)


def _kth_largest_key(count_ge, k, n_keys):
    c0 = count_ge(jnp.zeros(n_keys.shape, i32))
    ans = jnp.where(c0 >= k, 0, INT_MIN).astype(i32)
    cnt = jnp.where(c0 >= k, c0, n_keys)

    def unsettled(st):
        i, _, cnt = st
        return (i < 31) & (jnp.max(jnp.where((cnt == k) | (n_keys < k), 0, 1)) > 0)

    def body(st):
        i, ans, cnt = st
        cand = ans + lax.shift_left(jnp.int32(1), jnp.int32(30) - i)
        c = count_ge(cand)
        return i + 1, jnp.where(c >= k, cand, ans), jnp.where(c >= k, c, cnt)

    _, ans, cnt = lax.while_loop(unsettled, body, (jnp.int32(0), ans, cnt))
    return jnp.maximum(ans, INT_MIN + 1), cnt


def _prompt_attn_kernel(q_ref, qi_ref, wi_ref, ki_ref, k_ref, vt_ref, low_ref, o_ref, keys_ref, bias_ref,
                        m_ref, top_ref, kabs_ref, acc_ref, qz_ref, *, top_k):
    qb = pl.program_id(1)
    tq, ck, sub = Q_TILE, KEY_CHUNK, KEY_SUB
    n_chunks = (qb * tq + tq + ck - 1) // ck
    qpos = qb * tq + lax.broadcasted_iota(i32, (sub, tq), 1)
    krow = lax.broadcasted_iota(i32, (sub, tq), 0)

    qit = qi_ref[0].astype(f32).T.astype(bf16)
    w = wi_ref[0].T[0:N_IDX_HEADS, :] * (IDX_DIM ** -0.5)
    qt = q_ref[0].astype(f32).T.astype(bf16)
    kv_w = N_KV_HEADS * HEAD_DIM
    for g in range(N_KV_HEADS):
        q4 = jnp.concatenate([qt[(g * KV_GROUP + r) * HEAD_DIM:(g * KV_GROUP + r + 1) * HEAD_DIM, :]
                              for r in range(KV_GROUP)], axis=1)
        qz_ref[g] = jnp.zeros((kv_w, KV_GROUP * tq), bf16)
        qz_ref[g, g * HEAD_DIM:(g + 1) * HEAD_DIM, :] = q4

    def score_chunk(c, carry):
        for u in range(ck // sub):
            kc = ki_ref[0, c, u * sub:(u + 1) * sub, :]
            acc = jnp.zeros((sub, tq), f32)
            for h in range(N_IDX_HEADS):
                acc = acc + jnp.maximum(_dot(kc, qit[h * IDX_DIM:(h + 1) * IDX_DIM, :]), 0.0) * w[h:h + 1, :]
            causal = c * ck + u * sub + krow <= qpos
            keys_ref[c, u * sub:(u + 1) * sub, :] = jnp.where(causal, _score_key(acc), INT_MIN)
        return carry

    lax.fori_loop(0, n_chunks, score_chunk, 0)

    def count_ge(cand):
        lanes_of_acc = 8 * SUBLANE
        def body(c, acc):
            m = (keys_ref[c] >= cand).astype(f32)
            return acc + jnp.sum(m.reshape(ck // lanes_of_acc, lanes_of_acc, tq), axis=0)
        acc = lax.fori_loop(0, n_chunks, body, jnp.zeros((lanes_of_acc, tq), f32))
        return jnp.sum(acc, axis=0, keepdims=True)

    n_causal = (qb * tq + 1 + lax.broadcasted_iota(i32, (1, tq), 1)).astype(f32)
    thr, n_ge = _kth_largest_key(count_ge, float(top_k), n_causal)


    @pl.when(jnp.max(n_ge) > top_k)
    def _():
        need = top_k - count_ge(thr + 1)

        def body(c, before):
            kc = keys_ref[c]
            eq = kc == thr
            eqb = eq.astype(bf16)
            rank = _dot(low_ref[...], eqb) + before
            keys_ref[c] = jnp.where(eq & (rank >= need), INT_MIN, kc)
            return before + jnp.sum(eqb.astype(f32), axis=0, keepdims=True)

        lax.fori_loop(0, n_chunks, body, jnp.zeros((1, tq), f32))

    def to_bias(c, carry):
        bias_ref[c] = jnp.where(keys_ref[c] >= thr, 0.0, NEG_BIG)
        return carry

    lax.fori_loop(0, n_chunks, to_bias, 0)

    @pl.when(qb == 0)
    def _():
        def body(c, best):
            a = jnp.abs(k_ref[0, c].astype(f32))
            return jnp.maximum(best, jnp.max(a.reshape(ck // SUBLANE, SUBLANE, kv_w), axis=0))
        best = lax.fori_loop(0, k_ref.shape[1], body, jnp.zeros((SUBLANE, kv_w), f32))
        kabs_ref[...] = jnp.broadcast_to(jnp.max(best, axis=0, keepdims=True), (SUBLANE, kv_w))

    def sweep():
        acc_ref[...] = jnp.zeros(acc_ref.shape, f32)

        def accumulate(c, carry):
            bias = bias_ref[c]
            for g in range(N_KV_HEADS):
                s4 = _dot(k_ref[0, c], qz_ref[g])
                ps = [jnp.exp(s4[:, r * tq:(r + 1) * tq] + bias - top_ref[g * KV_GROUP + r]).astype(bf16)
                      for r in range(KV_GROUP)]
                acc_ref[g] = acc_ref[g] + _dot(vt_ref[c, g], jnp.concatenate(ps, axis=1))
            return carry

        lax.fori_loop(0, n_chunks, accumulate, 0)

    for g in range(N_KV_HEADS):
        bound = _dot(kabs_ref[...].astype(bf16), jnp.abs(qz_ref[g]))[0:1, :] * BOUND_MARGIN
        for r in range(KV_GROUP):
            top_ref[g * KV_GROUP + r] = bound[:, r * tq:(r + 1) * tq]
    sweep()
    smallest = acc_ref[0, HEAD_DIM:HEAD_DIM + 1, :]
    for g in range(1, N_KV_HEADS):
        smallest = jnp.minimum(smallest, acc_ref[g, HEAD_DIM:HEAD_DIM + 1, :])

    @pl.when(jnp.min(smallest) < MIN_TRUSTED_SUM)
    def _():
        part = m_ref.shape[1]
        m_ref[...] = jnp.full(m_ref.shape, NEG_BIG, f32)

        def col_max(c, carry):
            bias = bias_ref[c]
            for g in range(N_KV_HEADS):
                s4 = _dot(k_ref[0, c], qz_ref[g])
                for r in range(KV_GROUP):
                    h = g * KV_GROUP + r
                    s = (s4[:, r * tq:(r + 1) * tq] + bias).reshape(ck // part, part, tq)
                    m_ref[h] = jnp.maximum(m_ref[h], jnp.max(s, axis=0))
            return carry

        lax.fori_loop(0, n_chunks, col_max, 0)
        for h in range(N_HEADS):
            top_ref[h] = jnp.max(m_ref[h], axis=0, keepdims=True)
        sweep()

    heads = []
    for g in range(N_KV_HEADS):
        o4 = acc_ref[g, 0:HEAD_DIM, :] / acc_ref[g, HEAD_DIM:HEAD_DIM + 1, :]
        heads += [o4[:, r * tq:(r + 1) * tq] for r in range(KV_GROUP)]
    o_ref[0] = jnp.concatenate(heads, axis=0).T.astype(o_ref.dtype)


def _prompt_attention(q, qi, wi, ki, k, vt):
    b, s, _ = q.shape
    top_k = min(TOPK_MAX, s // 4)
    tq, ck = Q_TILE, KEY_CHUNK
    nq, nc = s // tq, s // ck
    gq = KV_GROUP * tq
    kv_w = N_KV_HEADS * HEAD_DIM
    low = jnp.asarray(np.arange(ck)[:, None] > np.arange(ck)[None, :], bf16)
    per_query = lambda w: pl.BlockSpec((1, tq, w), lambda i, j: (i, j, 0))
    return pl.pallas_call(
        functools.partial(_prompt_attn_kernel, top_k=top_k),
        grid=(b, nq),
        in_specs=[
            per_query(q.shape[2]), per_query(qi.shape[2]), per_query(wi.shape[2]),
            pl.BlockSpec((1, nc, ck, IDX_DIM), lambda i, j: (i, 0, 0, 0)),
            pl.BlockSpec((1, nc, ck, kv_w), lambda i, j: (i, 0, 0, 0)),
            pl.BlockSpec((nc, N_KV_HEADS, V_ROWS, ck), lambda i, j: (i, 0, 0, 0)),
            _resident((ck, ck)),
        ],
        out_specs=per_query(q.shape[2]),
        out_shape=jax.ShapeDtypeStruct(q.shape, bf16),
        scratch_shapes=[pltpu.VMEM((nc, ck, tq), i32), pltpu.VMEM((nc, ck, tq), f32),
                        pltpu.VMEM((N_HEADS, SOFTMAX_PARTIALS, tq), f32), pltpu.VMEM((N_HEADS, 1, tq), f32),
                        pltpu.VMEM((SUBLANE, kv_w), f32),
                        pltpu.VMEM((N_KV_HEADS, V_ROWS, gq), f32),
                        pltpu.VMEM((N_KV_HEADS, kv_w, gq), bf16)],
        compiler_params=_params("parallel", "arbitrary"),
        name="prompt_attention",
    )(q, qi, wi, ki.reshape(b, nc, ck, IDX_DIM), k.reshape(b, nc, ck, kv_w), vt, low)


def _sample_select_kernel(pt_ref, qi_ref, wi_ref, kin_ref, tri_ref, lt_ref, *rest, top_k, n_pages):
    page_refs, (sel_ref, seln_ref, keys_ref) = rest[:PAGES_PER_STEP], rest[PAGES_PER_STEP:]
    j = pl.program_id(1)
    qi = qi_ref[0]
    w = wi_ref[0]

    for i in range(PAGES_PER_STEP):
        s = _dot_nt(qi, page_refs[i][0].astype(bf16))
        score = jnp.sum(jnp.maximum(s, 0.0) * w, axis=0, keepdims=True)
        keys_ref[pl.ds(j * PAGES_PER_STEP + i, 1), :] = _score_key(score)

    @pl.when(j == pl.num_programs(1) - 1)
    def _():
        kn = kin_ref[0].astype(bf16).astype(f32)
        s_new = jnp.sum(qi.astype(f32) * kn, axis=1, keepdims=True)
        key_new = _score_key(jnp.sum(jnp.maximum(s_new, 0.0) * w, axis=0, keepdims=True))
        keys = keys_ref[...]

        def count_ge(cand):
            c = jnp.sum((keys >= cand).astype(f32), axis=1, keepdims=True)
            return jnp.sum(c, axis=0, keepdims=True) + (key_new >= cand).astype(f32)

        thr, _ = _kth_largest_key(count_ge, float(top_k), jnp.full((1, 1), n_pages * PAGE_SIZE + 1, f32))
        need = top_k - (count_ge(thr + 1))
        eq = keys == thr
        eqb = eq.astype(bf16)
        in_row = _dot(eqb, tri_ref[...])
        per_page = jnp.sum(eqb.astype(f32), axis=1, keepdims=True)
        before = _dot(lt_ref[...], per_page.astype(bf16) * jnp.ones((1, LANE), bf16))[:, 0:1]
        rank = in_row + before
        sel = (keys > thr) | (eq & (rank < need))
        sel_ref[0] = sel.astype(f32)
        n_eq = jnp.sum(per_page, axis=0, keepdims=True)
        sel_new = (key_new > thr) | ((key_new == thr) & (n_eq < need))
        seln_ref[0] = jnp.broadcast_to(sel_new.astype(f32), (1, LANE))


def _sample_attend_kernel(pt_ref, q_ref, sel_ref, seln_ref, kn_ref, vn_ref, gm_ref, *rest, n_pages):
    k_refs = rest[:PAGES_PER_STEP]
    v_refs = rest[PAGES_PER_STEP:2 * PAGES_PER_STEP]
    o_ref, m_ref, l_ref, acc_ref = rest[2 * PAGES_PER_STEP:]
    j = pl.program_id(1)
    q = q_ref[0]

    @pl.when(j == 0)
    def _():
        m_ref[...] = jnp.full(m_ref.shape, NEG_BIG, f32)
        l_ref[...] = jnp.zeros(l_ref.shape, f32)
        acc_ref[...] = jnp.zeros(acc_ref.shape, f32)

    def update(s, vals):
        m = m_ref[...]
        m_new = jnp.maximum(m, jnp.max(s, axis=1, keepdims=True))
        alpha = jnp.exp(m - m_new)
        p = jnp.exp(s - m_new)
        l_ref[...] = alpha * l_ref[...] + jnp.sum(p, axis=1, keepdims=True)
        acc_ref[...] = alpha * acc_ref[...] + vals(p)
        m_ref[...] = m_new

    for i in range(PAGES_PER_STEP):
        sel = sel_ref[0, pl.ds(j * PAGES_PER_STEP + i, 1), :] > 0.5
        s = jnp.where(sel, _dot_nt(q, k_refs[i][0].astype(bf16)), NEG_BIG)
        update(s, lambda p, i=i: _dot(p.astype(bf16), v_refs[i][0].astype(bf16)))

    @pl.when(j == pl.num_programs(1) - 1)
    def _():
        kn = kn_ref[0].astype(bf16).astype(f32)
        vn = vn_ref[0].astype(bf16).astype(f32)
        s = jnp.sum(q.astype(f32) * kn, axis=1, keepdims=True)
        s = jnp.where(seln_ref[0][:, 0:1] > 0.5, s, NEG_BIG)
        update(s, lambda p: p.astype(bf16).astype(f32) * vn)
        full = acc_ref[...] / l_ref[...] * gm_ref[...]
        out = full[:, 0:HEAD_DIM]
        for g in range(1, N_KV_HEADS):
            out = out + full[:, g * HEAD_DIM:(g + 1) * HEAD_DIM]
        o_ref[0] = out.astype(o_ref.dtype)


def _sample_attention(q, k_new, v_new, qi, ki_new, wi, cache_k, cache_v, cache_kidx, page_table):
    db = q.shape[0]
    n_pages = page_table.shape[1]
    past = n_pages * PAGE_SIZE
    top_k = min(TOPK_MAX, (past + 1) // 4)
    pps = PAGES_PER_STEP
    steps = n_pages // pps
    kv_w = N_KV_HEADS * HEAD_DIM
    n_pool = cache_k.shape[0]
    ck2 = cache_k.reshape(n_pool, PAGE_SIZE, kv_w)
    cv2 = cache_v.reshape(n_pool, PAGE_SIZE, kv_w)

    def page_spec(width, i):
        return pl.BlockSpec((1, PAGE_SIZE, width), lambda b, j, pt, i=i: (pt[b, j * pps + i], 0, 0))

    per_seq = lambda *shape: pl.BlockSpec((1,) + shape, lambda b, j, pt: (b,) + (0,) * len(shape))
    const = lambda *shape: pl.BlockSpec(shape, lambda b, j, pt: (0,) * len(shape))
    tri = jnp.asarray(np.arange(PAGE_SIZE)[:, None] < np.arange(PAGE_SIZE)[None, :], bf16)
    lower = jnp.asarray(np.arange(n_pages)[:, None] > np.arange(n_pages)[None, :], bf16)

    sel, sel_new = pl.pallas_call(
        functools.partial(_sample_select_kernel, top_k=top_k, n_pages=n_pages),
        grid_spec=pltpu.PrefetchScalarGridSpec(
            num_scalar_prefetch=1,
            grid=(db, steps),
            in_specs=[per_seq(N_IDX_HEADS, IDX_DIM), per_seq(N_IDX_HEADS, 1), per_seq(1, IDX_DIM),
                      const(PAGE_SIZE, PAGE_SIZE), const(n_pages, n_pages)]
                     + [page_spec(IDX_DIM, i) for i in range(pps)],
            out_specs=[per_seq(n_pages, PAGE_SIZE), per_seq(1, LANE)],
            scratch_shapes=[pltpu.VMEM((n_pages, PAGE_SIZE), i32)],
        ),
        out_shape=[jax.ShapeDtypeStruct((db, n_pages, PAGE_SIZE), f32), jax.ShapeDtypeStruct((db, 1, LANE), f32)],
        compiler_params=_params("parallel", "arbitrary"),
        name="sample_select",
    )(page_table, qi.reshape(db, N_IDX_HEADS, IDX_DIM), (wi * (IDX_DIM ** -0.5)).reshape(db, N_IDX_HEADS, 1),
      ki_new.reshape(db, 1, IDX_DIM), tri, lower, *([cache_kidx] * pps))

    group_mask = np.repeat(np.arange(N_HEADS)[:, None] // KV_GROUP == np.arange(N_KV_HEADS)[None, :], HEAD_DIM, axis=1)
    q_bd = jnp.where(group_mask[None], jnp.tile(q.reshape(db, N_HEADS, HEAD_DIM), (1, 1, N_KV_HEADS)), 0).astype(bf16)
    out = pl.pallas_call(
        functools.partial(_sample_attend_kernel, n_pages=n_pages),
        grid_spec=pltpu.PrefetchScalarGridSpec(
            num_scalar_prefetch=1,
            grid=(db, steps),
            in_specs=[per_seq(N_HEADS, kv_w), per_seq(n_pages, PAGE_SIZE), per_seq(1, LANE),
                      per_seq(1, kv_w), per_seq(1, kv_w), const(N_HEADS, kv_w)]
                     + [page_spec(kv_w, i) for i in range(pps)] * 2,
            out_specs=per_seq(N_HEADS, HEAD_DIM),
            scratch_shapes=[pltpu.VMEM((N_HEADS, 1), f32), pltpu.VMEM((N_HEADS, 1), f32), pltpu.VMEM((N_HEADS, kv_w), f32)],
        ),
        out_shape=jax.ShapeDtypeStruct((db, N_HEADS, HEAD_DIM), bf16),
        compiler_params=_params("parallel", "arbitrary"),
        name="sample_attend",
    )(page_table, q_bd, sel, sel_new, k_new.reshape(db, 1, kv_w), v_new.reshape(db, 1, kv_w),
      jnp.asarray(group_mask, f32), *([ck2] * pps), *([cv2] * pps))
    return out.reshape(db, N_HEADS * HEAD_DIM)


def _rglru_coeffs(xc, wr_ref, br_ref, wi_ref, bi_ref, lam_ref):
    xb = xc.astype(bf16)
    r = jax.nn.sigmoid(_dot(xb, wr_ref[...]) + br_ref[...])
    i = jax.nn.sigmoid(_dot(xb, wi_ref[...]) + bi_ref[...])
    z = -lam_ref[...]
    softplus = jnp.maximum(z, 0.0) + jnp.log1p(jnp.exp(-jnp.abs(z)))
    log_a = -LRU_C * r * softplus
    a = jnp.exp(log_a)
    th = jnp.tanh(log_a)
    b = jnp.sqrt(-2.0 * th / (1.0 - th)) * (i * xc)
    return a, b


def _rglru_seq_kernel(xr_ref, yr_ref, cw_ref, cb_ref, wr_ref, br_ref, wi_ref, bi_ref, lam_ref,
                      o_ref, hl_ref, xprev_ref, h_ref, a_ref, b_ref):
    t = pl.program_id(1)
    tc, d = xr_ref.shape[1], xr_ref.shape[2]

    @pl.when(t == 0)
    def _():
        xprev_ref[...] = jnp.zeros(xprev_ref.shape, f32)
        h_ref[...] = jnp.zeros(h_ref.shape, f32)

    xr = xr_ref[0]
    xcat = jnp.concatenate([xprev_ref[...], xr], axis=0)
    xc = cb_ref[...] + cw_ref[0:1, :] * xcat[SUBLANE - 3:SUBLANE - 3 + tc]
    for j in range(1, CONV_W):
        xc = xc + cw_ref[j:j + 1, :] * xcat[SUBLANE - 3 + j:SUBLANE - 3 + j + tc]
    xprev_ref[...] = xr[tc - SUBLANE:, :]
    a, b = _rglru_coeffs(xc, wr_ref, br_ref, wi_ref, bi_ref, lam_ref)
    a_ref[...] = a
    b_ref[...] = b
    row = lax.broadcasted_iota(i32, (SUBLANE, d), 0)

    def slab(s, h_prev):
        r0 = pl.multiple_of(s * SUBLANE, SUBLANE)
        a8 = a_ref[pl.ds(r0, SUBLANE), :]
        b8 = b_ref[pl.ds(r0, SUBLANE), :]
        for sh in (1, 2, 4):
            keep = row >= sh
            b8 = jnp.where(keep, a8 * pltpu.roll(b8, sh, 0) + b8, b8)
            a8 = jnp.where(keep, a8 * pltpu.roll(a8, sh, 0), a8)
        h8 = a8 * h_prev + b8
        o_ref[0, pl.ds(r0, SUBLANE), :] = (h8 * _gelu(yr_ref[0, pl.ds(r0, SUBLANE), :])).astype(o_ref.dtype)
        return h8[SUBLANE - 1:SUBLANE, :]

    h_last = lax.fori_loop(0, tc // SUBLANE, slab, h_ref[...])
    h_ref[...] = h_last
    hl_ref[0] = h_last


def _block_diag(w):
    n, a, b = w.shape
    eye = jnp.eye(n, dtype=w.dtype)
    return (eye[:, None, :, None] * w[:, :, None, :]).reshape(n * a, n * b).astype(bf16)


def _rglru_prompt(xr, yr, conv_w, conv_b, w_rgate, b_rgate, w_igate, b_igate, lru_lambda):
    b, t, d = xr.shape
    tc = RNN_CHUNK
    vec = lambda v: v.reshape(1, d)
    seq = pl.BlockSpec((1, tc, d), lambda i, j: (i, j, 0))
    out, h_last = pl.pallas_call(
        _rglru_seq_kernel,
        grid=(b, t // tc),
        in_specs=[seq, seq, _resident((CONV_W, d)), _resident((1, d)), _resident((d, d)), _resident((1, d)),
                  _resident((d, d)), _resident((1, d)), _resident((1, d))],
        out_specs=[seq, pl.BlockSpec((1, 1, d), lambda i, j: (i, 0, 0))],
        out_shape=[jax.ShapeDtypeStruct((b, t, d), bf16), jax.ShapeDtypeStruct((b, 1, d), f32)],
        scratch_shapes=[pltpu.VMEM((SUBLANE, d), f32), pltpu.VMEM((1, d), f32),
                        pltpu.VMEM((tc, d), f32), pltpu.VMEM((tc, d), f32)],
        compiler_params=_params("parallel", "arbitrary"),
        name="rglru_prompt",
    )(xr, yr, conv_w, vec(conv_b), _block_diag(w_rgate), vec(b_rgate), _block_diag(w_igate), vec(b_igate),
      vec(lru_lambda))
    return out, h_last.reshape(b, d)


def _rglru_step_kernel(xr_ref, yr_ref, c0_ref, c1_ref, c2_ref, h0_ref, cw_ref, cb_ref, wr_ref, br_ref, wi_ref,
                       bi_ref, lam_ref, o_ref, h_ref):
    xc = cb_ref[...] + cw_ref[0:1, :] * c0_ref[...]
    xc = xc + cw_ref[1:2, :] * c1_ref[...]
    xc = xc + cw_ref[2:3, :] * c2_ref[...]
    xc = xc + cw_ref[3:4, :] * xr_ref[...]
    a, b = _rglru_coeffs(xc, wr_ref, br_ref, wi_ref, bi_ref, lam_ref)
    h = a * h0_ref[...] + b
    h_ref[...] = h
    o_ref[...] = (h * _gelu(yr_ref[...])).astype(o_ref.dtype)


def _rglru_sample(xr, yr, state_conv, h0, conv_w, conv_b, w_rgate, b_rgate, w_igate, b_igate, lru_lambda):
    db, d = xr.shape
    vec = lambda v: v.reshape(1, d)
    return pl.pallas_call(
        _rglru_step_kernel,
        out_shape=[jax.ShapeDtypeStruct((db, d), bf16), jax.ShapeDtypeStruct((db, d), f32)],
        compiler_params=pltpu.CompilerParams(vmem_limit_bytes=VMEM_LIMIT_BYTES),
        name="rglru_sample",
    )(xr, yr, state_conv[:, 0], state_conv[:, 1], state_conv[:, 2], h0, conv_w, vec(conv_b),
      _block_diag(w_rgate), vec(b_rgate), _block_diag(w_igate), vec(b_igate), vec(lru_lambda))


def _top_rows(x, k, payload=None):
    n = x.shape[0]
    row = lax.broadcasted_iota(i32, x.shape, 0)
    vals, picked = [], []
    for _ in range(k):
        m = jnp.max(x, axis=0, keepdims=True)
        r = jnp.min(jnp.where(x == m, row, n), axis=0, keepdims=True)
        hit = row == r
        vals.append(m)
        picked.append(r if payload is None else jnp.sum(jnp.where(hit, payload, 0), axis=0, keepdims=True))
        x = jnp.where(hit, -jnp.inf, x)
    return jnp.concatenate(vals, axis=0), jnp.concatenate(picked, axis=0)


_PAIR_COUNTS = [PEER_TOPK // (r1 + 1) for r1 in range(PEER_TOPK)]


def _back_kernel(x_ref, ao_ref, ro_ref, ga_ref, gr_ref, woa_ref, wor_ref, wout_ref, g2_ref, wqt_ref, k1_ref, k2_ref,
                 x1_ref, hn_ref, eidx_ref, gate_ref):
    mixed = ga_ref[...] * _dot(ao_ref[...], woa_ref[...]) + gr_ref[...] * _dot(ro_ref[...], wor_ref[...])
    x1 = x_ref[...] + _dot(mixed.astype(bf16), wout_ref[...])
    x1_ref[...] = x1
    hn = (x1 * lax.rsqrt(jnp.mean(x1 * x1, axis=-1, keepdims=True) + NORM_EPS) * g2_ref[...]).astype(bf16)
    hn_ref[...] = hn.astype(f32)
    qt = _dot_nt(wqt_ref[...], hn).astype(bf16)
    half = PEER_KEY_DIM // 2
    for h in range(PEER_HEADS):
        base = h * PEER_KEY_DIM
        v1, i1 = _top_rows(_dot(k1_ref[...], qt[base:base + half, :]), PEER_TOPK)
        v2, i2 = _top_rows(_dot(k2_ref[...], qt[base + half:base + PEER_KEY_DIM, :]), PEER_TOPK)
        cand, cid = [], []
        for r1, n2 in enumerate(_PAIR_COUNTS):
            cand.append(v1[r1:r1 + 1, :] + v2[0:n2, :])
            cid.append(i1[r1:r1 + 1, :] * N_KEYS + i2[0:n2, :])
        pad = -sum(_PAIR_COUNTS) % SUBLANE
        tm = v1.shape[1]
        cand.append(jnp.full((pad, tm), -jnp.inf, f32))
        cid.append(jnp.zeros((pad, tm), i32))
        sv, eidx = _top_rows(jnp.concatenate(cand, axis=0), PEER_TOPK, jnp.concatenate(cid, axis=0))
        p = jnp.exp(sv - sv[0:1, :])
        eidx_ref[h * PEER_TOPK:(h + 1) * PEER_TOPK, :] = eidx
        gate_ref[h * PEER_TOPK:(h + 1) * PEER_TOPK, :] = p / jnp.sum(p, axis=0, keepdims=True)


def _back(x, attn_o, rnn_o, g_a, g_r, w_o_attn, w_o_rnn, w_out, norm2_g, w_peer_q, peer_k1, peer_k2):
    n, d = x.shape
    tm = TOKEN_TILE
    row = lambda w: pl.BlockSpec((tm, w), lambda i: (i, 0))
    col = pl.BlockSpec((PEER_SLOTS, tm), lambda i: (0, i))
    ws = [w_o_attn.astype(bf16), w_o_rnn.astype(bf16), w_out.astype(bf16), norm2_g.reshape(1, d),
          w_peer_q.T.astype(bf16), peer_k1.astype(bf16), peer_k2.astype(bf16)]
    return pl.pallas_call(
        _back_kernel,
        grid=(n // tm,),
        in_specs=[row(d), row(attn_o.shape[1]), row(rnn_o.shape[1]), row(d), row(d)] + [_resident(w.shape) for w in ws],
        out_specs=[row(d), row(d), col, col],
        out_shape=[jax.ShapeDtypeStruct((n, d), f32), jax.ShapeDtypeStruct((n, d), f32),
                   jax.ShapeDtypeStruct((PEER_SLOTS, n), i32), jax.ShapeDtypeStruct((PEER_SLOTS, n), f32)],
        compiler_params=_params("parallel"),
        name="back",
    )(x, attn_o, rnn_o, g_a, g_r, *ws)


def _pack_rows(table, reverse_pieces=False):
    e, d = table.shape
    bits = lax.bitcast_convert_type(table.astype(bf16), jnp.uint16).astype(jnp.uint32).reshape(e // 2, 2, d // LANE, LANE)
    if reverse_pieces:
        bits = bits[:, :, ::-1, :]
    return lax.bitcast_convert_type(bits[:, 0] | (bits[:, 1] << 16), i32).reshape(e // 2 * (d // LANE), LANE)


def _gather_tiles(table_ref, row_ref, t):
    starts = row_ref.at[t]
    tiles = [table_ref[pl.ds(pl.multiple_of(starts[s], SUBLANE), SUBLANE), :] for s in range(PEER_SLOTS)]
    return pltpu.bitcast(jnp.concatenate(tiles, axis=0), bf16)


def _token_tiles(x8):
    tiles = x8.reshape(SUBLANE, x8.shape[1] // LANE, LANE)
    return [tiles[j] for j in range(SUBLANE)]


def _peer_act_kernel(start_ref, hn_ref, gate_ref, u_ref, coef_ref):
    row = lax.broadcasted_iota(i32, (SUBLANE, LANE), 0)
    n_blk = PEER_X // LANE

    def token_group(gi, carry):
        t0 = pl.multiple_of(gi * SUBLANE, SUBLANE)
        acts = [jnp.zeros((SUBLANE, LANE), f32) for _ in range(n_blk)]
        h_tiles = _token_tiles(hn_ref[pl.ds(t0, SUBLANE), :])
        for j in range(SUBLANE):
            rows = _gather_tiles(u_ref, start_ref, t0 + j)
            r = _dot_nt(h_tiles[j].astype(bf16), rows)
            for blk in range(n_blk):
                z = pltpu.roll(r[:, blk * LANE:(blk + 1) * LANE], 0, 1, stride=2, stride_axis=0)
                for sh in (4, 2, 1):
                    z = z + pltpu.roll(z, sh, 0)
                acts[blk] = jnp.where(row == j, z, acts[blk])
        for blk in range(n_blk):
            cols = slice(blk * LANE, (blk + 1) * LANE)
            coef = gate_ref[pl.ds(t0, SUBLANE), cols] * _gelu(acts[blk])
            coef_ref[pl.ds(t0, SUBLANE), cols] = pltpu.roll(coef, LANE - (PACKED_ROWS - 2), 1)
        return carry

    lax.fori_loop(0, hn_ref.shape[0] // SUBLANE, token_group, 0)


def _peer_out_kernel(start_ref, coef_ref, x1_ref, v_ref, y_ref):
    n_blk = PEER_X // LANE

    def token_group(gi, carry):
        t0 = pl.multiple_of(gi * SUBLANE, SUBLANE)
        coef = coef_ref[pl.ds(t0, SUBLANE), :]
        outs = []
        for j in range(SUBLANE):
            rows = _gather_tiles(v_ref, start_ref, t0 + j)
            pick = [pltpu.roll(jnp.broadcast_to(coef[j:j + 1, blk * LANE:(blk + 1) * LANE], (SUBLANE, LANE)),
                               0, 1, stride=2, stride_axis=0) for blk in range(n_blk)]
            outs.append(_dot(jnp.concatenate(pick, axis=1).astype(bf16), rows))
        x1 = x1_ref[pl.ds(t0, SUBLANE), :]
        y_ref[pl.ds(t0, SUBLANE), :] = x1 + jnp.stack(outs, axis=0).reshape(x1.shape)
        return carry

    lax.fori_loop(0, x1_ref.shape[0] // SUBLANE, token_group, 0)


def _peer(hn, x1, eidx_t, gate_t, peer_u, peer_v):
    n, d = x1.shape
    tt = PEER_TOKENS
    rows = d // LANE
    eidx = eidx_t.T
    start = (eidx >> 1) * SUBLANE
    col = jnp.tile(jnp.arange(PACKED_ROWS, dtype=i32), PEER_SLOTS)[None, :]
    gate = jnp.where(col == PACKED_ROWS - 2 + jnp.repeat(eidx & 1, PACKED_ROWS, axis=1),
                     jnp.repeat(gate_t.T, PACKED_ROWS, axis=1), 0.0)
    smem = pl.BlockSpec((tt, PEER_SLOTS), lambda i: (i, 0), memory_space=pltpu.SMEM)
    tokens = pl.BlockSpec((tt, d), lambda i: (i, 0))
    wide = pl.BlockSpec((tt, PEER_X), lambda i: (i, 0))
    table = _resident((peer_u.shape[0] // 2 * rows, LANE))
    coef = pl.pallas_call(
        _peer_act_kernel,
        grid=(n // tt,),
        in_specs=[smem, tokens, wide, table],
        out_specs=wide,
        out_shape=jax.ShapeDtypeStruct((n, PEER_X), f32),
        compiler_params=_params("parallel"),
        name="peer_act",
    )(start, hn, gate, _pack_rows(peer_u, reverse_pieces=True))
    y = pl.pallas_call(
        _peer_out_kernel,
        grid=(n // tt,),
        in_specs=[smem, wide, tokens, table],
        out_specs=tokens,
        out_shape=jax.ShapeDtypeStruct((n, d), f32),
        compiler_params=_params("parallel"),
        name="peer_out",
    )(start, coef, x1, _pack_rows(peer_v))
    return y


def kernel(x_prompt, x_sample, cache_k, cache_v, cache_kidx, page_table, state_conv, state_rglru, norm1_g, w_in, b_gates, q_norm_g, k_norm_g, conv_w, conv_b, w_rgate, b_rgate, w_igate, b_igate, lru_lambda, w_o_attn, w_o_rnn, w_out, norm2_g, w_peer_q, peer_k1, peer_k2, peer_u, peer_v):
    bp, sp, d = x_prompt.shape
    db, ts, _ = x_sample.shape
    assert ts == 1, "the sample path handles one new token per sequence"
    d_rnn = conv_w.shape[1]
    past = page_table.shape[1] * PAGE_SIZE
    n_p, n_s = bp * sp, db * ts
    assert n_p % TOKEN_TILE == 0
    n_sp = -(-n_s // KEY_CHUNK) * KEY_CHUNK
    rnn_w = (conv_w, conv_b, w_rgate, b_rgate, w_igate, b_igate, lru_lambda)
    front_w = (norm1_g, w_in, b_gates, q_norm_g, k_norm_g, d_rnn)
    back_w = (w_o_attn, w_o_rnn, w_out, norm2_g, w_peer_q, peer_k1, peer_k2)
    pad = lambda a: jnp.concatenate([a, jnp.zeros((n_sp - n_s,) + a.shape[1:], a.dtype)])
    heads = lambda a, b_, t_: a.reshape(b_, t_, N_KV_HEADS, HEAD_DIM)
    flat = lambda a: a.reshape(a.shape[0], N_KV_HEADS * HEAD_DIM)

    xp = x_prompt.reshape(n_p, d)
    q, k_p, v_p, qi, ki_p, wi, xr_p, yr, g_a, g_r, kb, kib, vt = _front(
        xp, jnp.tile(jnp.arange(sp, dtype=f32), bp), *front_w)
    pr = lambda a: a.reshape(bp, sp, a.shape[1])
    attn_p = _prompt_attention(pr(q), pr(qi), pr(wi), pr(kib), pr(kb), vt)
    rnn_p, h_p = _rglru_prompt(pr(xr_p), pr(yr), *rnn_w)
    x1, hn, eidx_t, gate_t = _back(xp, attn_p.reshape(n_p, -1), rnn_p.reshape(n_p, d_rnn), g_a, g_r, *back_w)
    y_p = _peer(hn, x1, eidx_t, gate_t, peer_u, peer_v)

    xs = pad(x_sample.reshape(n_s, d))
    pos_s = pad(past + jnp.tile(jnp.arange(ts, dtype=f32), db))
    q, k_s, v_s, qi, ki_s, wi, xr_s, yr, g_a, g_r, _, _, _ = _front(xs, pos_s, *front_w)
    sm = lambda a: a[:n_s]
    attn_s = _sample_attention(sm(q), flat(sm(k_s)), flat(sm(v_s)), sm(qi), sm(ki_s), sm(wi[:, :N_IDX_HEADS]),
                               cache_k, cache_v, cache_kidx, page_table)
    rnn_s, h_s = _rglru_sample(sm(xr_s), sm(yr), state_conv, state_rglru, *rnn_w)
    x1, hn, eidx_t, gate_t = _back(xs, pad(attn_s), pad(rnn_s), g_a, g_r, *back_w)
    y_s = _peer(hn, x1, eidx_t, gate_t, peer_u, peer_v)

    conv_p = pr(xr_p)[:, sp - (CONV_W - 1):, :]
    conv_s = jnp.concatenate([state_conv[:, ts:, :], sm(xr_s).reshape(db, ts, d_rnn)], axis=1)
    return (y_p.reshape(bp, sp, d), sm(y_s).reshape(db, ts, d),
            heads(k_p, bp, sp), heads(v_p, bp, sp), pr(ki_p),
            conv_p, h_p,
            heads(sm(k_s), db, ts), heads(sm(v_s), db, ts), sm(ki_s).reshape(db, ts, IDX_DIM),
            conv_s, h_s)
```

```python
import functools

import numpy as np
import jax
import jax.numpy as jnp
from jax import lax
from jax.experimental import pallas as pl
from jax.experimental.pallas import tpu as pltpu

f32 = jnp.float32
bf16 = jnp.bfloat16
i32 = jnp.int32

N_HEADS = 16
HEAD_DIM = 64
N_KV_HEADS = 4
KV_GROUP = N_HEADS // N_KV_HEADS
ROT_DIM = HEAD_DIM // 4
ROT_HALF = ROT_DIM // 2
ROPE_THETA = 500000.0
QK_EPS = 1e-6
N_IDX_HEADS = 8
IDX_DIM = 64
TOPK_MAX = 256
PAGE_SIZE = 128
CONV_W = 4
LRU_C = 8.0
PEER_HEADS = 8
PEER_KEY_DIM = 128
N_KEYS = 128
PEER_TOPK = 16
PEER_SLOTS = PEER_HEADS * PEER_TOPK
NORM_EPS = 1e-6

LANE = 128
SUBLANE = 8
PACKED_ROWS = 16
VMEM_LIMIT_BYTES = 56 * 1024 * 1024

INT_MIN = -(2 ** 31)
NEG_BIG = -1e30

TOKEN_TILE = 256
Q_TILE = LANE
KEY_CHUNK = 512
KEY_SUB = 256
SOFTMAX_PARTIALS = 32
V_ROWS = HEAD_DIM + PACKED_ROWS
BOUND_MARGIN = 1.02
MIN_TRUSTED_SUM = 1e-25
RNN_CHUNK = 256
PEER_TOKENS = 64
PAGES_PER_STEP = 8
PEER_X = PEER_SLOTS * PACKED_ROWS


def _params(*sem):
    return pltpu.CompilerParams(dimension_semantics=sem, vmem_limit_bytes=VMEM_LIMIT_BYTES)


def _resident(shape):
    nd = len(shape)
    return pl.BlockSpec(shape, lambda *_: (0,) * nd, pipeline_mode=pl.Buffered(1))


def _gelu(x):
    return 0.5 * x * (1.0 + jnp.tanh(np.sqrt(2.0 / np.pi).astype(np.float32) * (x + 0.044715 * (x * x * x))))


def _dot(a, b):
    return jnp.dot(a, b, preferred_element_type=f32)


def _dot_nt(a, b):
    return lax.dot_general(a, b, (((1,), (1,)), ((), ())), preferred_element_type=f32)


def _front_kernel(x_ref, g1_ref, wa_ref, wb_ref, bg_ref, qg_ref, kg_ref, c_ref, s1_ref, s2_ref, bd_ref,
                  q_ref, k_ref, v_ref, qi_ref, ki_ref, wi_ref, xr_ref, yr_ref, ga_ref, gr_ref,
                  kb_ref, kib_ref, vt_ref, *, d_model, d_rnn):
    x = x_ref[...]
    h = (x * lax.rsqrt(jnp.mean(x * x, axis=-1, keepdims=True) + NORM_EPS) * g1_ref[...]).astype(bf16)
    cos, sin_hi, sin_lo = c_ref[...], s1_ref[...], s2_ref[...]
    bd = bd_ref[...]

    def rope(t):
        return t * cos + pltpu.roll(t, LANE - ROT_HALF, 1) * sin_hi + pltpu.roll(t, ROT_HALF, 1) * sin_lo

    def head_norm(t, g):
        sq = t * t
        hi = sq.astype(bf16)
        lo = (sq - hi.astype(f32)).astype(bf16)
        ss = _dot(hi, bd) + _dot(lo, bd)
        return t * lax.rsqrt(ss * (1.0 / HEAD_DIM) + QK_EPS) * g

    q_w = N_HEADS * HEAD_DIM
    kv_w = N_KV_HEADS * HEAD_DIM
    qi_w = N_IDX_HEADS * IDX_DIM
    pa = _dot(h, wa_ref[...])
    off = 0
    for j in range(q_w // LANE):
        t = pa[:, off + j * LANE: off + (j + 1) * LANE]
        q_ref[:, j * LANE:(j + 1) * LANE] = (rope(head_norm(t, qg_ref[...])) * (HEAD_DIM ** -0.5)).astype(bf16)
    off += q_w
    for j in range(kv_w // LANE):
        t = pa[:, off + j * LANE: off + (j + 1) * LANE]
        kt = rope(head_norm(t, kg_ref[...]))
        ktt = kt.T
        for u in range(LANE // HEAD_DIM):
            k_ref[0, j * (LANE // HEAD_DIM) + u] = ktt[u * HEAD_DIM:(u + 1) * HEAD_DIM, :]
        kb_ref[:, j * LANE:(j + 1) * LANE] = kt.astype(bf16)
    off += kv_w
    vv = pa[:, off:off + kv_w]
    vvt = vv.T
    for g in range(N_KV_HEADS):
        v_ref[0, g] = vvt[g * HEAD_DIM:(g + 1) * HEAD_DIM, :]
    vt = vvt.astype(bf16)
    ones_row = (lax.broadcasted_iota(i32, (V_ROWS - HEAD_DIM, vv.shape[0]), 0) == 0).astype(bf16)
    for g in range(N_KV_HEADS):
        vt_ref[0, g, 0:HEAD_DIM, :] = vt[g * HEAD_DIM:(g + 1) * HEAD_DIM, :]
        vt_ref[0, g, HEAD_DIM:V_ROWS, :] = ones_row
    off += kv_w
    for j in range(qi_w // LANE):
        t = pa[:, off + j * LANE: off + (j + 1) * LANE]
        qi_ref[:, j * LANE:(j + 1) * LANE] = rope(t).astype(bf16)
    off += qi_w
    ki_wide = rope(pa[:, off:off + LANE])
    ki_ref[0] = ki_wide.T[0:IDX_DIM, :]
    kib_ref[...] = ki_wide[:, :IDX_DIM].astype(bf16)
    off += LANE
    wi_ref[...] = pa[:, off:off + LANE] * (N_IDX_HEADS ** -0.5)

    xr_ref[...] = _dot(h, wb_ref[:, 0:d_rnn])
    yr_ref[...] = _dot(h, wb_ref[:, d_rnn:2 * d_rnn])
    o = 2 * d_rnn
    ga_ref[...] = jax.nn.sigmoid(_dot(h, wb_ref[:, o:o + d_model]) + bg_ref[:, 0:d_model])
    gr_ref[...] = jax.nn.sigmoid(_dot(h, wb_ref[:, o + d_model:o + 2 * d_model]) + bg_ref[:, d_model:2 * d_model])


def _rope_tables(pos):
    inv = ROPE_THETA ** (-jnp.arange(ROT_HALF, dtype=f32) / ROT_HALF)
    ang = pos.astype(f32)[:, None] * inv[None, :]
    c, s = jnp.cos(ang), jnp.sin(ang)
    n = pos.shape[0]
    z = lambda w: jnp.zeros((n, w), f32)
    cos = jnp.concatenate([c, c, jnp.ones((n, HEAD_DIM - ROT_DIM), f32)], axis=1)
    s_hi = jnp.concatenate([-s, z(HEAD_DIM - ROT_HALF)], axis=1)
    s_lo = jnp.concatenate([z(ROT_HALF), s, z(HEAD_DIM - ROT_DIM)], axis=1)
    rep = LANE // HEAD_DIM
    return jnp.tile(cos, (1, rep)), jnp.tile(s_hi, (1, rep)), jnp.tile(s_lo, (1, rep))


def _front(x, pos, seq, norm1_g, w_in, b_gates, q_norm_g, k_norm_g, d_rnn):
    n, d_model = x.shape
    q_w, kv_w, qi_w = N_HEADS * HEAD_DIM, N_KV_HEADS * HEAD_DIM, N_IDX_HEADS * IDX_DIM
    sizes = [q_w, kv_w, kv_w, qi_w, IDX_DIM, N_IDX_HEADS, d_rnn, d_rnn, d_model, d_model]
    offs = np.cumsum([0] + sizes)
    parts = [w_in[:, offs[i]:offs[i + 1]] for i in range(len(sizes))]
    padl = lambda w: jnp.pad(w, ((0, 0), (0, LANE - w.shape[1])))
    wa = jnp.concatenate(parts[:4] + [padl(parts[4]), padl(parts[5])], axis=1).astype(bf16)
    wb = jnp.concatenate(parts[6:], axis=1).astype(bf16)
    cos, s_hi, s_lo = _rope_tables(pos)
    rep = LANE // HEAD_DIM
    head_of_lane = np.arange(LANE) // HEAD_DIM
    bd = jnp.asarray(head_of_lane[:, None] == head_of_lane[None, :], bf16)
    tm = TOKEN_TILE
    row = lambda w: pl.BlockSpec((tm, w), lambda i: (i, 0))
    outs = [(q_w, bf16), (kv_w, f32), (kv_w, f32), (qi_w, bf16), (IDX_DIM, f32), (LANE, f32),
            (d_rnn, f32), (d_rnn, f32), (d_model, f32), (d_model, f32), (kv_w, bf16), (IDX_DIM, bf16)]
    per_chunk = KEY_CHUNK // tm
    assert n % KEY_CHUNK == 0
    tps = seq // tm
    assert n % seq == 0 and seq % tm == 0
    per_head = pl.BlockSpec((1, N_KV_HEADS, HEAD_DIM, tm), lambda i: (i // tps, 0, 0, i % tps))
    per_idx = pl.BlockSpec((1, IDX_DIM, tm), lambda i: (i // tps, 0, i % tps))
    special = {1: (per_head, (n // seq, N_KV_HEADS, HEAD_DIM, seq)), 2: (per_head, (n // seq, N_KV_HEADS, HEAD_DIM, seq)),
               4: (per_idx, (n // seq, IDX_DIM, seq))}
    vt_spec = pl.BlockSpec((1, N_KV_HEADS, V_ROWS, tm), lambda i: (i // per_chunk, 0, 0, i % per_chunk))
    vt_shape = jax.ShapeDtypeStruct((n // KEY_CHUNK, N_KV_HEADS, V_ROWS, KEY_CHUNK), bf16)
    return pl.pallas_call(
        functools.partial(_front_kernel, d_model=d_model, d_rnn=d_rnn),
        grid=(n // tm,),
        in_specs=[row(d_model), _resident((1, d_model)), _resident(wa.shape), _resident(wb.shape),
                  _resident((1, 2 * d_model)), _resident((1, LANE)), _resident((1, LANE)),
                  row(LANE), row(LANE), row(LANE), _resident((LANE, LANE))],
        out_specs=[special[i][0] if i in special else row(w) for i, (w, _) in enumerate(outs)] + [vt_spec],
        out_shape=[jax.ShapeDtypeStruct(special[i][1] if i in special else (n, w), dt)
                   for i, (w, dt) in enumerate(outs)] + [vt_shape],
        compiler_params=_params("parallel"),
        name="front",
    )(x, norm1_g.reshape(1, -1), wa, wb, b_gates.reshape(1, -1),
      jnp.tile(q_norm_g, rep).reshape(1, LANE), jnp.tile(k_norm_g, rep).reshape(1, LANE),
      cos, s_hi, s_lo, bd)


def _score_key(score):
    score = jnp.where(score == 0.0, 0.0, score)
    bits = pltpu.bitcast(score, i32)
    return bits ^ ((bits >> 31) & 0x7FFFFFFF)


def _kth_largest_key(count_ge, k, n_keys):
    c0 = count_ge(jnp.zeros(n_keys.shape, i32))
    ans = jnp.where(c0 >= k, 0, INT_MIN).astype(i32)
    cnt = jnp.where(c0 >= k, c0, n_keys)

    def unsettled(st):
        i, _, cnt = st
        return (i < 31) & (jnp.max(jnp.where((cnt == k) | (n_keys < k), 0, 1)) > 0)

    def body(st):
        i, ans, cnt = st
        cand = ans + lax.shift_left(jnp.int32(1), jnp.int32(30) - i)
        c = count_ge(cand)
        return i + 1, jnp.where(c >= k, cand, ans), jnp.where(c >= k, c, cnt)

    _, ans, cnt = lax.while_loop(unsettled, body, (jnp.int32(0), ans, cnt))
    return jnp.maximum(ans, INT_MIN + 1), cnt


def _prompt_attn_kernel(q_ref, qi_ref, wi_ref, ki_ref, k_ref, vt_ref, low_ref, o_ref, keys_ref, bias_ref,
                        m_ref, top_ref, kabs_ref, acc_ref, qz_ref, *, top_k):
    qb = pl.program_id(1)
    tq, ck, sub = Q_TILE, KEY_CHUNK, KEY_SUB
    n_chunks = (qb * tq + tq + ck - 1) // ck
    qpos = qb * tq + lax.broadcasted_iota(i32, (sub, tq), 1)
    krow = lax.broadcasted_iota(i32, (sub, tq), 0)

    qit = qi_ref[0].astype(f32).T.astype(bf16)
    w = wi_ref[0].T[0:N_IDX_HEADS, :] * (IDX_DIM ** -0.5)
    qt = q_ref[0].astype(f32).T.astype(bf16)
    kv_w = N_KV_HEADS * HEAD_DIM
    for g in range(N_KV_HEADS):
        q4 = jnp.concatenate([qt[(g * KV_GROUP + r) * HEAD_DIM:(g * KV_GROUP + r + 1) * HEAD_DIM, :]
                              for r in range(KV_GROUP)], axis=1)
        qz_ref[g] = jnp.zeros((kv_w, KV_GROUP * tq), bf16)
        qz_ref[g, g * HEAD_DIM:(g + 1) * HEAD_DIM, :] = q4

    def score_chunk(c, carry):
        for u in range(ck // sub):
            kc = ki_ref[0, c, u * sub:(u + 1) * sub, :]
            acc = jnp.zeros((sub, tq), f32)
            for h in range(N_IDX_HEADS):
                acc = acc + jnp.maximum(_dot(kc, qit[h * IDX_DIM:(h + 1) * IDX_DIM, :]), 0.0) * w[h:h + 1, :]
            causal = c * ck + u * sub + krow <= qpos
            keys_ref[c, u * sub:(u + 1) * sub, :] = jnp.where(causal, _score_key(acc), INT_MIN)
        return carry

    lax.fori_loop(0, n_chunks, score_chunk, 0)

    def count_ge(cand):
        lanes_of_acc = 8 * SUBLANE
        def body(c, acc):
            m = (keys_ref[c] >= cand).astype(f32)
            return acc + jnp.sum(m.reshape(ck // lanes_of_acc, lanes_of_acc, tq), axis=0)
        acc = lax.fori_loop(0, n_chunks, body, jnp.zeros((lanes_of_acc, tq), f32))
        return jnp.sum(acc, axis=0, keepdims=True)

    n_causal = (qb * tq + 1 + lax.broadcasted_iota(i32, (1, tq), 1)).astype(f32)
    thr, n_ge = _kth_largest_key(count_ge, float(top_k), n_causal)


    @pl.when(jnp.max(n_ge) > top_k)
    def _():
        need = top_k - count_ge(thr + 1)

        def body(c, before):
            kc = keys_ref[c]
            eq = kc == thr
            eqb = eq.astype(bf16)
            rank = _dot(low_ref[...], eqb) + before
            keys_ref[c] = jnp.where(eq & (rank >= need), INT_MIN, kc)
            return before + jnp.sum(eqb.astype(f32), axis=0, keepdims=True)

        lax.fori_loop(0, n_chunks, body, jnp.zeros((1, tq), f32))

    def to_bias(c, carry):
        bias_ref[c] = jnp.where(keys_ref[c] >= thr, 0.0, NEG_BIG)
        return carry

    lax.fori_loop(0, n_chunks, to_bias, 0)

    @pl.when(qb == 0)
    def _():
        def body(c, best):
            a = jnp.abs(k_ref[0, c].astype(f32))
            return jnp.maximum(best, jnp.max(a.reshape(ck // SUBLANE, SUBLANE, kv_w), axis=0))
        best = lax.fori_loop(0, k_ref.shape[1], body, jnp.zeros((SUBLANE, kv_w), f32))
        kabs_ref[...] = jnp.broadcast_to(jnp.max(best, axis=0, keepdims=True), (SUBLANE, kv_w))

    def sweep():
        acc_ref[...] = jnp.zeros(acc_ref.shape, f32)

        def accumulate(c, carry):
            bias = bias_ref[c]
            for g in range(N_KV_HEADS):
                s4 = _dot(k_ref[0, c], qz_ref[g])
                ps = [jnp.exp(s4[:, r * tq:(r + 1) * tq] + bias - top_ref[g * KV_GROUP + r]).astype(bf16)
                      for r in range(KV_GROUP)]
                acc_ref[g] = acc_ref[g] + _dot(vt_ref[c, g], jnp.concatenate(ps, axis=1))
            return carry

        lax.fori_loop(0, n_chunks, accumulate, 0)

    for g in range(N_KV_HEADS):
        bound = _dot(kabs_ref[...].astype(bf16), jnp.abs(qz_ref[g]))[0:1, :] * BOUND_MARGIN
        for r in range(KV_GROUP):
            top_ref[g * KV_GROUP + r] = bound[:, r * tq:(r + 1) * tq]
    sweep()
    smallest = acc_ref[0, HEAD_DIM:HEAD_DIM + 1, :]
    for g in range(1, N_KV_HEADS):
        smallest = jnp.minimum(smallest, acc_ref[g, HEAD_DIM:HEAD_DIM + 1, :])

    @pl.when(jnp.min(smallest) < MIN_TRUSTED_SUM)
    def _():
        part = m_ref.shape[1]
        m_ref[...] = jnp.full(m_ref.shape, NEG_BIG, f32)

        def col_max(c, carry):
            bias = bias_ref[c]
            for g in range(N_KV_HEADS):
                s4 = _dot(k_ref[0, c], qz_ref[g])
                for r in range(KV_GROUP):
                    h = g * KV_GROUP + r
                    s = (s4[:, r * tq:(r + 1) * tq] + bias).reshape(ck // part, part, tq)
                    m_ref[h] = jnp.maximum(m_ref[h], jnp.max(s, axis=0))
            return carry

        lax.fori_loop(0, n_chunks, col_max, 0)
        for h in range(N_HEADS):
            top_ref[h] = jnp.max(m_ref[h], axis=0, keepdims=True)
        sweep()

    heads = []
    for g in range(N_KV_HEADS):
        o4 = acc_ref[g, 0:HEAD_DIM, :] / acc_ref[g, HEAD_DIM:HEAD_DIM + 1, :]
        heads += [o4[:, r * tq:(r + 1) * tq] for r in range(KV_GROUP)]
    o_ref[0] = jnp.concatenate(heads, axis=0).T.astype(o_ref.dtype)


def _prompt_attention(q, qi, wi, ki, k, vt):
    b, s, _ = q.shape
    top_k = min(TOPK_MAX, s // 4)
    tq, ck = Q_TILE, KEY_CHUNK
    nq, nc = s // tq, s // ck
    gq = KV_GROUP * tq
    kv_w = N_KV_HEADS * HEAD_DIM
    low = jnp.asarray(np.arange(ck)[:, None] > np.arange(ck)[None, :], bf16)
    per_query = lambda w: pl.BlockSpec((1, tq, w), lambda i, j: (i, j, 0))
    return pl.pallas_call(
        functools.partial(_prompt_attn_kernel, top_k=top_k),
        grid=(b, nq),
        in_specs=[
            per_query(q.shape[2]), per_query(qi.shape[2]), per_query(wi.shape[2]),
            pl.BlockSpec((1, nc, ck, IDX_DIM), lambda i, j: (i, 0, 0, 0)),
            pl.BlockSpec((1, nc, ck, kv_w), lambda i, j: (i, 0, 0, 0)),
            pl.BlockSpec((nc, N_KV_HEADS, V_ROWS, ck), lambda i, j: (i, 0, 0, 0)),
            _resident((ck, ck)),
        ],
        out_specs=per_query(q.shape[2]),
        out_shape=jax.ShapeDtypeStruct(q.shape, bf16),
        scratch_shapes=[pltpu.VMEM((nc, ck, tq), i32), pltpu.VMEM((nc, ck, tq), f32),
                        pltpu.VMEM((N_HEADS, SOFTMAX_PARTIALS, tq), f32), pltpu.VMEM((N_HEADS, 1, tq), f32),
                        pltpu.VMEM((SUBLANE, kv_w), f32),
                        pltpu.VMEM((N_KV_HEADS, V_ROWS, gq), f32),
                        pltpu.VMEM((N_KV_HEADS, kv_w, gq), bf16)],
        compiler_params=_params("parallel", "arbitrary"),
        name="prompt_attention",
    )(q, qi, wi, ki.reshape(b, nc, ck, IDX_DIM), k.reshape(b, nc, ck, kv_w), vt, low)


def _sample_select_kernel(pt_ref, qi_ref, wi_ref, kin_ref, tri_ref, lt_ref, *rest, top_k, n_pages):
    page_refs, (sel_ref, seln_ref, keys_ref) = rest[:PAGES_PER_STEP], rest[PAGES_PER_STEP:]
    j = pl.program_id(1)
    qi = qi_ref[0]
    w = wi_ref[0]

    for i in range(PAGES_PER_STEP):
        s = _dot(qi, page_refs[i][0].astype(bf16))
        score = jnp.sum(jnp.maximum(s, 0.0) * w, axis=0, keepdims=True)
        keys_ref[pl.ds(j * PAGES_PER_STEP + i, 1), :] = _score_key(score)

    @pl.when(j == pl.num_programs(1) - 1)
    def _():
        kn = kin_ref[0].astype(bf16).astype(f32)
        s_new = jnp.sum(qi.astype(f32) * kn, axis=1, keepdims=True)
        key_new = _score_key(jnp.sum(jnp.maximum(s_new, 0.0) * w, axis=0, keepdims=True))
        keys = keys_ref[...]

        def count_ge(cand):
            c = jnp.sum((keys >= cand).astype(f32), axis=1, keepdims=True)
            return jnp.sum(c, axis=0, keepdims=True) + (key_new >= cand).astype(f32)

        thr, _ = _kth_largest_key(count_ge, float(top_k), jnp.full((1, 1), n_pages * PAGE_SIZE + 1, f32))
        need = top_k - (count_ge(thr + 1))
        eq = keys == thr
        eqb = eq.astype(bf16)
        in_row = _dot(eqb, tri_ref[...])
        per_page = jnp.sum(eqb.astype(f32), axis=1, keepdims=True)
        before = _dot(lt_ref[...], per_page.astype(bf16) * jnp.ones((1, LANE), bf16))[:, 0:1]
        rank = in_row + before
        sel = (keys > thr) | (eq & (rank < need))
        sel_ref[0] = sel.astype(f32)
        n_eq = jnp.sum(per_page, axis=0, keepdims=True)
        sel_new = (key_new > thr) | ((key_new == thr) & (n_eq < need))
        seln_ref[0] = jnp.broadcast_to(sel_new.astype(f32), (1, LANE))


def _sample_attend_kernel(pt_ref, q_ref, sel_ref, seln_ref, kn_ref, vn_ref, gm_ref, *rest, n_pages):
    k_refs = rest[:PAGES_PER_STEP]
    v_refs = rest[PAGES_PER_STEP:2 * PAGES_PER_STEP]
    o_ref, m_ref, l_ref, acc_ref = rest[2 * PAGES_PER_STEP:]
    j = pl.program_id(1)
    q = q_ref[0]

    @pl.when(j == 0)
    def _():
        m_ref[...] = jnp.full(m_ref.shape, NEG_BIG, f32)
        l_ref[...] = jnp.zeros(l_ref.shape, f32)
        acc_ref[...] = jnp.zeros(acc_ref.shape, f32)

    def update(s, vals):
        m = m_ref[...]
        m_new = jnp.maximum(m, jnp.max(s, axis=1, keepdims=True))
        alpha = jnp.exp(m - m_new)
        p = jnp.exp(s - m_new)
        l_ref[...] = alpha * l_ref[...] + jnp.sum(p, axis=1, keepdims=True)
        acc_ref[...] = alpha * acc_ref[...] + vals(p)
        m_ref[...] = m_new

    for i in range(PAGES_PER_STEP):
        sel = sel_ref[0, pl.ds(j * PAGES_PER_STEP + i, 1), :] > 0.5
        s = jnp.where(sel, _dot(q, k_refs[i][0].astype(bf16)), NEG_BIG)
        update(s, lambda p, i=i: _dot_nt(p.astype(bf16), v_refs[i][0].astype(bf16)))

    @pl.when(j == pl.num_programs(1) - 1)
    def _():
        kn = kn_ref[0].astype(bf16).astype(f32)
        vn = vn_ref[0].astype(bf16).astype(f32)
        s = jnp.sum(q.astype(f32) * kn, axis=1, keepdims=True)
        s = jnp.where(seln_ref[0][:, 0:1] > 0.5, s, NEG_BIG)
        update(s, lambda p: p.astype(bf16).astype(f32) * vn)
        full = acc_ref[...] / l_ref[...] * gm_ref[...]
        out = full[:, 0:HEAD_DIM]
        for g in range(1, N_KV_HEADS):
            out = out + full[:, g * HEAD_DIM:(g + 1) * HEAD_DIM]
        o_ref[0] = out.astype(o_ref.dtype)


def _sample_attention(q, k_new, v_new, qi, ki_new, wi, cache_k, cache_v, cache_kidx, page_table):
    db = q.shape[0]
    n_pages = page_table.shape[1]
    past = n_pages * PAGE_SIZE
    top_k = min(TOPK_MAX, (past + 1) // 4)
    pps = PAGES_PER_STEP
    steps = n_pages // pps
    kv_w = N_KV_HEADS * HEAD_DIM
    n_pool = cache_k.shape[0]
    ck2 = cache_k.transpose(0, 2, 3, 1).reshape(n_pool, kv_w, PAGE_SIZE)
    cv2 = cache_v.transpose(0, 2, 3, 1).reshape(n_pool, kv_w, PAGE_SIZE)
    cki = cache_kidx.transpose(0, 2, 1)

    def page_spec(width, i):
        return pl.BlockSpec((1, width, PAGE_SIZE), lambda b, j, pt, i=i: (pt[b, j * pps + i], 0, 0))

    per_seq = lambda *shape: pl.BlockSpec((1,) + shape, lambda b, j, pt: (b,) + (0,) * len(shape))
    const = lambda *shape: pl.BlockSpec(shape, lambda b, j, pt: (0,) * len(shape))
    tri = jnp.asarray(np.arange(PAGE_SIZE)[:, None] < np.arange(PAGE_SIZE)[None, :], bf16)
    lower = jnp.asarray(np.arange(n_pages)[:, None] > np.arange(n_pages)[None, :], bf16)

    sel, sel_new = pl.pallas_call(
        functools.partial(_sample_select_kernel, top_k=top_k, n_pages=n_pages),
        grid_spec=pltpu.PrefetchScalarGridSpec(
            num_scalar_prefetch=1,
            grid=(db, steps),
            in_specs=[per_seq(N_IDX_HEADS, IDX_DIM), per_seq(N_IDX_HEADS, 1), per_seq(1, IDX_DIM),
                      const(PAGE_SIZE, PAGE_SIZE), const(n_pages, n_pages)]
                     + [page_spec(IDX_DIM, i) for i in range(pps)],
            out_specs=[per_seq(n_pages, PAGE_SIZE), per_seq(1, LANE)],
            scratch_shapes=[pltpu.VMEM((n_pages, PAGE_SIZE), i32)],
        ),
        out_shape=[jax.ShapeDtypeStruct((db, n_pages, PAGE_SIZE), f32), jax.ShapeDtypeStruct((db, 1, LANE), f32)],
        compiler_params=_params("parallel", "arbitrary"),
        name="sample_select",
    )(page_table, qi.reshape(db, N_IDX_HEADS, IDX_DIM), (wi * (IDX_DIM ** -0.5)).reshape(db, N_IDX_HEADS, 1),
      ki_new.reshape(db, 1, IDX_DIM), tri, lower, *([cki] * pps))

    group_mask = np.repeat(np.arange(N_HEADS)[:, None] // KV_GROUP == np.arange(N_KV_HEADS)[None, :], HEAD_DIM, axis=1)
    q_bd = jnp.where(group_mask[None], jnp.tile(q.reshape(db, N_HEADS, HEAD_DIM), (1, 1, N_KV_HEADS)), 0).astype(bf16)
    out = pl.pallas_call(
        functools.partial(_sample_attend_kernel, n_pages=n_pages),
        grid_spec=pltpu.PrefetchScalarGridSpec(
            num_scalar_prefetch=1,
            grid=(db, steps),
            in_specs=[per_seq(N_HEADS, kv_w), per_seq(n_pages, PAGE_SIZE), per_seq(1, LANE),
                      per_seq(1, kv_w), per_seq(1, kv_w), const(N_HEADS, kv_w)]
                     + [page_spec(kv_w, i) for i in range(pps)] * 2,
            out_specs=per_seq(N_HEADS, HEAD_DIM),
            scratch_shapes=[pltpu.VMEM((N_HEADS, 1), f32), pltpu.VMEM((N_HEADS, 1), f32), pltpu.VMEM((N_HEADS, kv_w), f32)],
        ),
        out_shape=jax.ShapeDtypeStruct((db, N_HEADS, HEAD_DIM), bf16),
        compiler_params=_params("parallel", "arbitrary"),
        name="sample_attend",
    )(page_table, q_bd, sel, sel_new, k_new.reshape(db, 1, kv_w), v_new.reshape(db, 1, kv_w),
      jnp.asarray(group_mask, f32), *([ck2] * pps), *([cv2] * pps))
    return out.reshape(db, N_HEADS * HEAD_DIM)


def _rglru_coeffs(xc, wr_ref, br_ref, wi_ref, bi_ref, lam_ref):
    xb = xc.astype(bf16)
    r = jax.nn.sigmoid(_dot(xb, wr_ref[...]) + br_ref[...])
    i = jax.nn.sigmoid(_dot(xb, wi_ref[...]) + bi_ref[...])
    z = -lam_ref[...]
    softplus = jnp.maximum(z, 0.0) + jnp.log1p(jnp.exp(-jnp.abs(z)))
    log_a = -LRU_C * r * softplus
    a = jnp.exp(log_a)
    th = jnp.tanh(log_a)
    b = jnp.sqrt(-2.0 * th / (1.0 - th)) * (i * xc)
    return a, b


def _rglru_seq_kernel(xr_ref, yr_ref, cw_ref, cb_ref, wr_ref, br_ref, wi_ref, bi_ref, lam_ref,
                      o_ref, hl_ref, xprev_ref, h_ref, a_ref, b_ref):
    t = pl.program_id(1)
    tc, d = xr_ref.shape[1], xr_ref.shape[2]

    @pl.when(t == 0)
    def _():
        xprev_ref[...] = jnp.zeros(xprev_ref.shape, f32)
        h_ref[...] = jnp.zeros(h_ref.shape, f32)

    xr = xr_ref[0]
    xcat = jnp.concatenate([xprev_ref[...], xr], axis=0)
    xc = cb_ref[...] + cw_ref[0:1, :] * xcat[SUBLANE - 3:SUBLANE - 3 + tc]
    for j in range(1, CONV_W):
        xc = xc + cw_ref[j:j + 1, :] * xcat[SUBLANE - 3 + j:SUBLANE - 3 + j + tc]
    xprev_ref[...] = xr[tc - SUBLANE:, :]
    a, b = _rglru_coeffs(xc, wr_ref, br_ref, wi_ref, bi_ref, lam_ref)
    a_ref[...] = a
    b_ref[...] = b
    row = lax.broadcasted_iota(i32, (SUBLANE, d), 0)

    def slab(s, h_prev):
        r0 = pl.multiple_of(s * SUBLANE, SUBLANE)
        a8 = a_ref[pl.ds(r0, SUBLANE), :]
        b8 = b_ref[pl.ds(r0, SUBLANE), :]
        for sh in (1, 2, 4):
            keep = row >= sh
            b8 = jnp.where(keep, a8 * pltpu.roll(b8, sh, 0) + b8, b8)
            a8 = jnp.where(keep, a8 * pltpu.roll(a8, sh, 0), a8)
        h8 = a8 * h_prev + b8
        o_ref[0, pl.ds(r0, SUBLANE), :] = (h8 * _gelu(yr_ref[0, pl.ds(r0, SUBLANE), :])).astype(o_ref.dtype)
        return h8[SUBLANE - 1:SUBLANE, :]

    h_last = lax.fori_loop(0, tc // SUBLANE, slab, h_ref[...])
    h_ref[...] = h_last
    hl_ref[0] = h_last


def _block_diag(w):
    n, a, b = w.shape
    eye = jnp.eye(n, dtype=w.dtype)
    return (eye[:, None, :, None] * w[:, :, None, :]).reshape(n * a, n * b).astype(bf16)


def _rglru_prompt(xr, yr, conv_w, conv_b, w_rgate, b_rgate, w_igate, b_igate, lru_lambda):
    b, t, d = xr.shape
    tc = RNN_CHUNK
    vec = lambda v: v.reshape(1, d)
    seq = pl.BlockSpec((1, tc, d), lambda i, j: (i, j, 0))
    out, h_last = pl.pallas_call(
        _rglru_seq_kernel,
        grid=(b, t // tc),
        in_specs=[seq, seq, _resident((CONV_W, d)), _resident((1, d)), _resident((d, d)), _resident((1, d)),
                  _resident((d, d)), _resident((1, d)), _resident((1, d))],
        out_specs=[seq, pl.BlockSpec((1, 1, d), lambda i, j: (i, 0, 0))],
        out_shape=[jax.ShapeDtypeStruct((b, t, d), bf16), jax.ShapeDtypeStruct((b, 1, d), f32)],
        scratch_shapes=[pltpu.VMEM((SUBLANE, d), f32), pltpu.VMEM((1, d), f32),
                        pltpu.VMEM((tc, d), f32), pltpu.VMEM((tc, d), f32)],
        compiler_params=_params("parallel", "arbitrary"),
        name="rglru_prompt",
    )(xr, yr, conv_w, vec(conv_b), _block_diag(w_rgate), vec(b_rgate), _block_diag(w_igate), vec(b_igate),
      vec(lru_lambda))
    return out, h_last.reshape(b, d)


def _rglru_step_kernel(xr_ref, yr_ref, c0_ref, c1_ref, c2_ref, h0_ref, cw_ref, cb_ref, wr_ref, br_ref, wi_ref,
                       bi_ref, lam_ref, o_ref, h_ref):
    xc = cb_ref[...] + cw_ref[0:1, :] * c0_ref[...]
    xc = xc + cw_ref[1:2, :] * c1_ref[...]
    xc = xc + cw_ref[2:3, :] * c2_ref[...]
    xc = xc + cw_ref[3:4, :] * xr_ref[...]
    a, b = _rglru_coeffs(xc, wr_ref, br_ref, wi_ref, bi_ref, lam_ref)
    h = a * h0_ref[...] + b
    h_ref[...] = h
    o_ref[...] = (h * _gelu(yr_ref[...])).astype(o_ref.dtype)


def _rglru_sample(xr, yr, state_conv, h0, conv_w, conv_b, w_rgate, b_rgate, w_igate, b_igate, lru_lambda):
    db, d = xr.shape
    vec = lambda v: v.reshape(1, d)
    return pl.pallas_call(
        _rglru_step_kernel,
        out_shape=[jax.ShapeDtypeStruct((db, d), bf16), jax.ShapeDtypeStruct((db, d), f32)],
        compiler_params=pltpu.CompilerParams(vmem_limit_bytes=VMEM_LIMIT_BYTES),
        name="rglru_sample",
    )(xr, yr, state_conv[:, 0], state_conv[:, 1], state_conv[:, 2], h0, conv_w, vec(conv_b),
      _block_diag(w_rgate), vec(b_rgate), _block_diag(w_igate), vec(b_igate), vec(lru_lambda))


def _top_rows(x, k, payload=None):
    n = x.shape[0]
    row = lax.broadcasted_iota(i32, x.shape, 0)
    vals, picked = [], []
    for _ in range(k):
        m = jnp.max(x, axis=0, keepdims=True)
        r = jnp.min(jnp.where(x == m, row, n), axis=0, keepdims=True)
        hit = row == r
        vals.append(m)
        picked.append(r if payload is None else jnp.sum(jnp.where(hit, payload, 0), axis=0, keepdims=True))
        x = jnp.where(hit, -jnp.inf, x)
    return jnp.concatenate(vals, axis=0), jnp.concatenate(picked, axis=0)


_PAIR_COUNTS = [PEER_TOPK // (r1 + 1) for r1 in range(PEER_TOPK)]


def _back_kernel(x_ref, ao_ref, ro_ref, ga_ref, gr_ref, woa_ref, wor_ref, wout_ref, g2_ref, wqt_ref, k1_ref, k2_ref,
                 x1_ref, hn_ref, eidx_ref, gate_ref):
    mixed = ga_ref[...] * _dot(ao_ref[...], woa_ref[...]) + gr_ref[...] * _dot(ro_ref[...], wor_ref[...])
    x1 = x_ref[...] + _dot(mixed.astype(bf16), wout_ref[...])
    x1_ref[...] = x1
    hn = (x1 * lax.rsqrt(jnp.mean(x1 * x1, axis=-1, keepdims=True) + NORM_EPS) * g2_ref[...]).astype(bf16)
    hn_ref[...] = hn.astype(f32)
    qt = _dot_nt(wqt_ref[...], hn).astype(bf16)
    half = PEER_KEY_DIM // 2
    for h in range(PEER_HEADS):
        base = h * PEER_KEY_DIM
        v1, i1 = _top_rows(_dot(k1_ref[...], qt[base:base + half, :]), PEER_TOPK)
        v2, i2 = _top_rows(_dot(k2_ref[...], qt[base + half:base + PEER_KEY_DIM, :]), PEER_TOPK)
        cand, cid = [], []
        for r1, n2 in enumerate(_PAIR_COUNTS):
            cand.append(v1[r1:r1 + 1, :] + v2[0:n2, :])
            cid.append(i1[r1:r1 + 1, :] * N_KEYS + i2[0:n2, :])
        pad = -sum(_PAIR_COUNTS) % SUBLANE
        tm = v1.shape[1]
        cand.append(jnp.full((pad, tm), -jnp.inf, f32))
        cid.append(jnp.zeros((pad, tm), i32))
        sv, eidx = _top_rows(jnp.concatenate(cand, axis=0), PEER_TOPK, jnp.concatenate(cid, axis=0))
        p = jnp.exp(sv - sv[0:1, :])
        eidx_ref[h * PEER_TOPK:(h + 1) * PEER_TOPK, :] = eidx
        gate_ref[h * PEER_TOPK:(h + 1) * PEER_TOPK, :] = p / jnp.sum(p, axis=0, keepdims=True)


def _back(x, attn_o, rnn_o, g_a, g_r, w_o_attn, w_o_rnn, w_out, norm2_g, w_peer_q, peer_k1, peer_k2):
    n, d = x.shape
    tm = TOKEN_TILE
    row = lambda w: pl.BlockSpec((tm, w), lambda i: (i, 0))
    col = pl.BlockSpec((PEER_SLOTS, tm), lambda i: (0, i))
    ws = [w_o_attn.astype(bf16), w_o_rnn.astype(bf16), w_out.astype(bf16), norm2_g.reshape(1, d),
          w_peer_q.T.astype(bf16), peer_k1.astype(bf16), peer_k2.astype(bf16)]
    return pl.pallas_call(
        _back_kernel,
        grid=(n // tm,),
        in_specs=[row(d), row(attn_o.shape[1]), row(rnn_o.shape[1]), row(d), row(d)] + [_resident(w.shape) for w in ws],
        out_specs=[row(d), row(d), col, col],
        out_shape=[jax.ShapeDtypeStruct((n, d), f32), jax.ShapeDtypeStruct((n, d), f32),
                   jax.ShapeDtypeStruct((PEER_SLOTS, n), i32), jax.ShapeDtypeStruct((PEER_SLOTS, n), f32)],
        compiler_params=_params("parallel"),
        name="back",
    )(x, attn_o, rnn_o, g_a, g_r, *ws)


def _pack_rows(table, reverse_pieces=False):
    e, d = table.shape
    bits = lax.bitcast_convert_type(table.astype(bf16), jnp.uint16).astype(jnp.uint32).reshape(e // 2, 2, d // LANE, LANE)
    if reverse_pieces:
        bits = bits[:, :, ::-1, :]
    return lax.bitcast_convert_type(bits[:, 0] | (bits[:, 1] << 16), i32).reshape(e // 2 * (d // LANE), LANE)


def _gather_tiles(table_ref, row_ref, t):
    starts = row_ref.at[t]
    tiles = [table_ref[pl.ds(pl.multiple_of(starts[s], SUBLANE), SUBLANE), :] for s in range(PEER_SLOTS)]
    return pltpu.bitcast(jnp.concatenate(tiles, axis=0), bf16)


def _token_tiles(x8):
    tiles = x8.reshape(SUBLANE, x8.shape[1] // LANE, LANE)
    return [tiles[j] for j in range(SUBLANE)]


def _peer_act_kernel(start_ref, hn_ref, gate_ref, u_ref, coef_ref):
    row = lax.broadcasted_iota(i32, (SUBLANE, LANE), 0)
    n_blk = PEER_X // LANE

    def token_group(gi, carry):
        t0 = pl.multiple_of(gi * SUBLANE, SUBLANE)
        acts = [jnp.zeros((SUBLANE, LANE), f32) for _ in range(n_blk)]
        h_tiles = _token_tiles(hn_ref[pl.ds(t0, SUBLANE), :])
        for j in range(SUBLANE):
            rows = _gather_tiles(u_ref, start_ref, t0 + j)
            r = _dot_nt(h_tiles[j].astype(bf16), rows)
            for blk in range(n_blk):
                z = pltpu.roll(r[:, blk * LANE:(blk + 1) * LANE], 0, 1, stride=2, stride_axis=0)
                for sh in (4, 2, 1):
                    z = z + pltpu.roll(z, sh, 0)
                acts[blk] = jnp.where(row == j, z, acts[blk])
        for blk in range(n_blk):
            cols = slice(blk * LANE, (blk + 1) * LANE)
            coef = gate_ref[pl.ds(t0, SUBLANE), cols] * _gelu(acts[blk])
            coef_ref[pl.ds(t0, SUBLANE), cols] = pltpu.roll(coef, LANE - (PACKED_ROWS - 2), 1)
        return carry

    lax.fori_loop(0, hn_ref.shape[0] // SUBLANE, token_group, 0)


def _peer_out_kernel(start_ref, coef_ref, x1_ref, v_ref, y_ref):
    n_blk = PEER_X // LANE

    def token_group(gi, carry):
        t0 = pl.multiple_of(gi * SUBLANE, SUBLANE)
        coef = coef_ref[pl.ds(t0, SUBLANE), :]
        outs = []
        for j in range(SUBLANE):
            rows = _gather_tiles(v_ref, start_ref, t0 + j)
            pick = [pltpu.roll(jnp.broadcast_to(coef[j:j + 1, blk * LANE:(blk + 1) * LANE], (SUBLANE, LANE)),
                               0, 1, stride=2, stride_axis=0) for blk in range(n_blk)]
            outs.append(_dot(jnp.concatenate(pick, axis=1).astype(bf16), rows))
        x1 = x1_ref[pl.ds(t0, SUBLANE), :]
        y_ref[pl.ds(t0, SUBLANE), :] = x1 + jnp.stack(outs, axis=0).reshape(x1.shape)
        return carry

    lax.fori_loop(0, x1_ref.shape[0] // SUBLANE, token_group, 0)


def _peer(hn, x1, eidx_t, gate_t, peer_u, peer_v):
    n, d = x1.shape
    tt = PEER_TOKENS
    rows = d // LANE
    eidx = eidx_t.T
    start = (eidx >> 1) * SUBLANE
    col = jnp.tile(jnp.arange(PACKED_ROWS, dtype=i32), PEER_SLOTS)[None, :]
    gate = jnp.where(col == PACKED_ROWS - 2 + jnp.repeat(eidx & 1, PACKED_ROWS, axis=1),
                     jnp.repeat(gate_t.T, PACKED_ROWS, axis=1), 0.0)
    smem = pl.BlockSpec((tt, PEER_SLOTS), lambda i: (i, 0), memory_space=pltpu.SMEM)
    tokens = pl.BlockSpec((tt, d), lambda i: (i, 0))
    wide = pl.BlockSpec((tt, PEER_X), lambda i: (i, 0))
    table = _resident((peer_u.shape[0] // 2 * rows, LANE))
    coef = pl.pallas_call(
        _peer_act_kernel,
        grid=(n // tt,),
        in_specs=[smem, tokens, wide, table],
        out_specs=wide,
        out_shape=jax.ShapeDtypeStruct((n, PEER_X), f32),
        compiler_params=_params("parallel"),
        name="peer_act",
    )(start, hn, gate, _pack_rows(peer_u, reverse_pieces=True))
    y = pl.pallas_call(
        _peer_out_kernel,
        grid=(n // tt,),
        in_specs=[smem, wide, tokens, table],
        out_specs=tokens,
        out_shape=jax.ShapeDtypeStruct((n, d), f32),
        compiler_params=_params("parallel"),
        name="peer_out",
    )(start, coef, x1, _pack_rows(peer_v))
    return y


def kernel(x_prompt, x_sample, cache_k, cache_v, cache_kidx, page_table, state_conv, state_rglru, norm1_g, w_in, b_gates, q_norm_g, k_norm_g, conv_w, conv_b, w_rgate, b_rgate, w_igate, b_igate, lru_lambda, w_o_attn, w_o_rnn, w_out, norm2_g, w_peer_q, peer_k1, peer_k2, peer_u, peer_v):
    bp, sp, d = x_prompt.shape
    db, ts, _ = x_sample.shape
    assert ts == 1, "the sample path handles one new token per sequence"
    d_rnn = conv_w.shape[1]
    past = page_table.shape[1] * PAGE_SIZE
    n_p, n_s = bp * sp, db * ts
    assert n_p % TOKEN_TILE == 0
    n_sp = -(-n_s // KEY_CHUNK) * KEY_CHUNK
    rnn_w = (conv_w, conv_b, w_rgate, b_rgate, w_igate, b_igate, lru_lambda)
    front_w = (norm1_g, w_in, b_gates, q_norm_g, k_norm_g, d_rnn)
    back_w = (w_o_attn, w_o_rnn, w_out, norm2_g, w_peer_q, peer_k1, peer_k2)
    pad = lambda a: jnp.concatenate([a, jnp.zeros((n_sp - n_s,) + a.shape[1:], a.dtype)])
    heads = lambda a, b_, t_: a.reshape(b_, t_, N_KV_HEADS, HEAD_DIM)
    flat = lambda a: a.reshape(a.shape[0], N_KV_HEADS * HEAD_DIM)

    xp = x_prompt.reshape(n_p, d)
    q, k_p, v_p, qi, ki_p, wi, xr_p, yr, g_a, g_r, kb, kib, vt = _front(
        xp, jnp.tile(jnp.arange(sp, dtype=f32), bp), sp, *front_w)
    pr = lambda a: a.reshape(bp, sp, a.shape[1])
    attn_p = _prompt_attention(pr(q), pr(qi), pr(wi), pr(kib), pr(kb), vt)
    rnn_p, h_p = _rglru_prompt(pr(xr_p), pr(yr), *rnn_w)
    x1, hn, eidx_t, gate_t = _back(xp, attn_p.reshape(n_p, -1), rnn_p.reshape(n_p, d_rnn), g_a, g_r, *back_w)
    y_p = _peer(hn, x1, eidx_t, gate_t, peer_u, peer_v)

    xs = pad(x_sample.reshape(n_s, d))
    pos_s = pad(past + jnp.tile(jnp.arange(ts, dtype=f32), db))
    q, k_s, v_s, qi, ki_s, wi, xr_s, yr, g_a, g_r, _, _, _ = _front(xs, pos_s, n_sp, *front_w)
    sm = lambda a: a[:n_s]
    k_s, v_s = (sm(a[0].transpose(2, 0, 1)) for a in (k_s, v_s))
    ki_s = sm(ki_s[0].T)
    attn_s = _sample_attention(sm(q), flat(k_s), flat(v_s), sm(qi), ki_s, sm(wi[:, :N_IDX_HEADS]),
                               cache_k, cache_v, cache_kidx, page_table)
    rnn_s, h_s = _rglru_sample(sm(xr_s), sm(yr), state_conv, state_rglru, *rnn_w)
    x1, hn, eidx_t, gate_t = _back(xs, pad(attn_s), pad(rnn_s), g_a, g_r, *back_w)
    y_s = _peer(hn, x1, eidx_t, gate_t, peer_u, peer_v)

    conv_p = pr(xr_p)[:, sp - (CONV_W - 1):, :]
    conv_s = jnp.concatenate([state_conv[:, ts:, :], sm(xr_s).reshape(db, ts, d_rnn)], axis=1)
    return (y_p.reshape(bp, sp, d), sm(y_s).reshape(db, ts, d),
            k_p.transpose(0, 3, 1, 2), v_p.transpose(0, 3, 1, 2), ki_p.transpose(0, 2, 1),
            conv_p, h_p,
            heads(k_s, db, ts), heads(v_s, db, ts), ki_s.reshape(db, ts, IDX_DIM),
            conv_s, h_s)
```

```python
import functools

import numpy as np
import jax
import jax.numpy as jnp
from jax import lax
from jax.experimental import pallas as pl
from jax.experimental.pallas import tpu as pltpu

f32 = jnp.float32
bf16 = jnp.bfloat16
i32 = jnp.int32

N_HEADS = 16
HEAD_DIM = 64
N_KV_HEADS = 4
KV_GROUP = N_HEADS // N_KV_HEADS
ROT_DIM = HEAD_DIM // 4
ROT_HALF = ROT_DIM // 2
ROPE_THETA = 500000.0
QK_EPS = 1e-6
N_IDX_HEADS = 8
IDX_DIM = 64
TOPK_MAX = 256
PAGE_SIZE = 128
CONV_W = 4
LRU_C = 8.0
PEER_HEADS = 8
PEER_KEY_DIM = 128
N_KEYS = 128
PEER_TOPK = 16
PEER_SLOTS = PEER_HEADS * PEER_TOPK
NORM_EPS = 1e-6

LANE = 128
SUBLANE = 8
PACKED_ROWS = 16
VMEM_LIMIT_BYTES = 56 * 1024 * 1024

INT_MIN = -(2 ** 31)
NEG_BIG = -1e30

TOKEN_TILE = 256
Q_TILE = LANE
KEY_CHUNK = 512
KEY_SUB = 256
SOFTMAX_PARTIALS = 32
V_ROWS = HEAD_DIM + PACKED_ROWS
BOUND_MARGIN = 1.02
MIN_TRUSTED_SUM = 1e-25
BITS_PER_EXIT_TEST = 4
RNN_CHUNK = 256
PEER_TOKENS = 64
PAGES_PER_STEP = 8
PEER_X = PEER_SLOTS * PACKED_ROWS


def _params(*sem):
    return pltpu.CompilerParams(dimension_semantics=sem, vmem_limit_bytes=VMEM_LIMIT_BYTES)


def _resident(shape):
    nd = len(shape)
    return pl.BlockSpec(shape, lambda *_: (0,) * nd, pipeline_mode=pl.Buffered(1))


def _gelu(x):
    return 0.5 * x * (1.0 + jnp.tanh(np.sqrt(2.0 / np.pi).astype(np.float32) * (x + 0.044715 * (x * x * x))))


def _dot(a, b):
    return jnp.dot(a, b, preferred_element_type=f32)


def _dot_nt(a, b):
    return lax.dot_general(a, b, (((1,), (1,)), ((), ())), preferred_element_type=f32)


def _front_kernel(x_ref, g1_ref, wa_ref, wb_ref, bg_ref, qg_ref, kg_ref, c_ref, s1_ref, s2_ref, bd_ref,
                  q_ref, k_ref, v_ref, qi_ref, ki_ref, wi_ref, xr_ref, yr_ref, ga_ref, gr_ref,
                  kb_ref, kib_ref, vt_ref, *, d_model, d_rnn):
    x = x_ref[...]
    h = (x * lax.rsqrt(jnp.mean(x * x, axis=-1, keepdims=True) + NORM_EPS) * g1_ref[...]).astype(bf16)
    cos, sin_hi, sin_lo = c_ref[...], s1_ref[...], s2_ref[...]
    bd = bd_ref[...]

    def rope(t):
        return t * cos + pltpu.roll(t, LANE - ROT_HALF, 1) * sin_hi + pltpu.roll(t, ROT_HALF, 1) * sin_lo

    def head_norm(t, g):
        sq = t * t
        hi = sq.astype(bf16)
        lo = (sq - hi.astype(f32)).astype(bf16)
        ss = _dot(hi, bd) + _dot(lo, bd)
        return t * lax.rsqrt(ss * (1.0 / HEAD_DIM) + QK_EPS) * g

    q_w = N_HEADS * HEAD_DIM
    kv_w = N_KV_HEADS * HEAD_DIM
    qi_w = N_IDX_HEADS * IDX_DIM
    pa = _dot(h, wa_ref[...])
    off = 0
    for j in range(q_w // LANE):
        t = pa[:, off + j * LANE: off + (j + 1) * LANE]
        q_ref[:, j * LANE:(j + 1) * LANE] = (rope(head_norm(t, qg_ref[...])) * (HEAD_DIM ** -0.5)).astype(bf16)
    off += q_w
    for j in range(kv_w // LANE):
        t = pa[:, off + j * LANE: off + (j + 1) * LANE]
        kt = rope(head_norm(t, kg_ref[...]))
        ktt = kt.T
        for u in range(LANE // HEAD_DIM):
            k_ref[0, j * (LANE // HEAD_DIM) + u] = ktt[u * HEAD_DIM:(u + 1) * HEAD_DIM, :]
        kb_ref[:, j * LANE:(j + 1) * LANE] = kt.astype(bf16)
    off += kv_w
    vv = pa[:, off:off + kv_w]
    vvt = vv.T
    for g in range(N_KV_HEADS):
        v_ref[0, g] = vvt[g * HEAD_DIM:(g + 1) * HEAD_DIM, :]
    vt = vvt.astype(bf16)
    ones_row = (lax.broadcasted_iota(i32, (V_ROWS - HEAD_DIM, vv.shape[0]), 0) == 0).astype(bf16)
    for g in range(N_KV_HEADS):
        vt_ref[0, g, 0:HEAD_DIM, :] = vt[g * HEAD_DIM:(g + 1) * HEAD_DIM, :]
        vt_ref[0, g, HEAD_DIM:V_ROWS, :] = ones_row
    off += kv_w
    for j in range(qi_w // LANE):
        t = pa[:, off + j * LANE: off + (j + 1) * LANE]
        qi_ref[:, j * LANE:(j + 1) * LANE] = rope(t).astype(bf16)
    off += qi_w
    ki_wide = rope(pa[:, off:off + LANE])
    ki_ref[0] = ki_wide.T[0:IDX_DIM, :]
    kib_ref[...] = ki_wide[:, :IDX_DIM].astype(bf16)
    off += LANE
    wi_ref[...] = pa[:, off:off + LANE] * (N_IDX_HEADS ** -0.5)

    xr_ref[...] = _dot(h, wb_ref[:, 0:d_rnn])
    yr_ref[...] = _dot(h, wb_ref[:, d_rnn:2 * d_rnn])
    o = 2 * d_rnn
    ga_ref[...] = jax.nn.sigmoid(_dot(h, wb_ref[:, o:o + d_model]) + bg_ref[:, 0:d_model])
    gr_ref[...] = jax.nn.sigmoid(_dot(h, wb_ref[:, o + d_model:o + 2 * d_model]) + bg_ref[:, d_model:2 * d_model])


def _rope_tables(pos):
    inv = ROPE_THETA ** (-jnp.arange(ROT_HALF, dtype=f32) / ROT_HALF)
    ang = pos.astype(f32)[:, None] * inv[None, :]
    c, s = jnp.cos(ang), jnp.sin(ang)
    n = pos.shape[0]
    z = lambda w: jnp.zeros((n, w), f32)
    cos = jnp.concatenate([c, c, jnp.ones((n, HEAD_DIM - ROT_DIM), f32)], axis=1)
    s_hi = jnp.concatenate([-s, z(HEAD_DIM - ROT_HALF)], axis=1)
    s_lo = jnp.concatenate([z(ROT_HALF), s, z(HEAD_DIM - ROT_DIM)], axis=1)
    rep = LANE // HEAD_DIM
    return jnp.tile(cos, (1, rep)), jnp.tile(s_hi, (1, rep)), jnp.tile(s_lo, (1, rep))


def _front(x, pos, seq, norm1_g, w_in, b_gates, q_norm_g, k_norm_g, d_rnn):
    n, d_model = x.shape
    q_w, kv_w, qi_w = N_HEADS * HEAD_DIM, N_KV_HEADS * HEAD_DIM, N_IDX_HEADS * IDX_DIM
    sizes = [q_w, kv_w, kv_w, qi_w, IDX_DIM, N_IDX_HEADS, d_rnn, d_rnn, d_model, d_model]
    offs = np.cumsum([0] + sizes)
    parts = [w_in[:, offs[i]:offs[i + 1]] for i in range(len(sizes))]
    padl = lambda w: jnp.pad(w, ((0, 0), (0, LANE - w.shape[1])))
    wa = jnp.concatenate(parts[:4] + [padl(parts[4]), padl(parts[5])], axis=1).astype(bf16)
    wb = jnp.concatenate(parts[6:], axis=1).astype(bf16)
    cos, s_hi, s_lo = _rope_tables(pos)
    rep = LANE // HEAD_DIM
    head_of_lane = np.arange(LANE) // HEAD_DIM
    bd = jnp.asarray(head_of_lane[:, None] == head_of_lane[None, :], bf16)
    tm = TOKEN_TILE
    row = lambda w: pl.BlockSpec((tm, w), lambda i: (i, 0))
    outs = [(q_w, bf16), (kv_w, f32), (kv_w, f32), (qi_w, bf16), (IDX_DIM, f32), (LANE, f32),
            (d_rnn, f32), (d_rnn, f32), (d_model, f32), (d_model, f32), (kv_w, bf16), (IDX_DIM, bf16)]
    per_chunk = KEY_CHUNK // tm
    assert n % KEY_CHUNK == 0
    tps = seq // tm
    assert n % seq == 0 and seq % tm == 0
    per_head = pl.BlockSpec((1, N_KV_HEADS, HEAD_DIM, tm), lambda i: (i // tps, 0, 0, i % tps))
    per_idx = pl.BlockSpec((1, IDX_DIM, tm), lambda i: (i // tps, 0, i % tps))
    special = {1: (per_head, (n // seq, N_KV_HEADS, HEAD_DIM, seq)), 2: (per_head, (n // seq, N_KV_HEADS, HEAD_DIM, seq)),
               4: (per_idx, (n // seq, IDX_DIM, seq))}
    vt_spec = pl.BlockSpec((1, N_KV_HEADS, V_ROWS, tm), lambda i: (i // per_chunk, 0, 0, i % per_chunk))
    vt_shape = jax.ShapeDtypeStruct((n // KEY_CHUNK, N_KV_HEADS, V_ROWS, KEY_CHUNK), bf16)
    return pl.pallas_call(
        functools.partial(_front_kernel, d_model=d_model, d_rnn=d_rnn),
        grid=(n // tm,),
        in_specs=[row(d_model), _resident((1, d_model)), _resident(wa.shape), _resident(wb.shape),
                  _resident((1, 2 * d_model)), _resident((1, LANE)), _resident((1, LANE)),
                  row(LANE), row(LANE), row(LANE), _resident((LANE, LANE))],
        out_specs=[special[i][0] if i in special else row(w) for i, (w, _) in enumerate(outs)] + [vt_spec],
        out_shape=[jax.ShapeDtypeStruct(special[i][1] if i in special else (n, w), dt)
                   for i, (w, dt) in enumerate(outs)] + [vt_shape],
        compiler_params=_params("parallel"),
        name="front",
    )(x, norm1_g.reshape(1, -1), wa, wb, b_gates.reshape(1, -1),
      jnp.tile(q_norm_g, rep).reshape(1, LANE), jnp.tile(k_norm_g, rep).reshape(1, LANE),
      cos, s_hi, s_lo, bd)


def _score_key(score):
    score = jnp.where(score == 0.0, 0.0, score)
    bits = pltpu.bitcast(score, i32)
    return bits ^ ((bits >> 31) & 0x7FFFFFFF)


def _kth_largest_key(count_ge, k, n_keys):
    c0 = count_ge(jnp.zeros(n_keys.shape, i32))
    ans = jnp.where(c0 >= k, 0, INT_MIN).astype(i32)
    cnt = jnp.where(c0 >= k, c0, n_keys)

    def unsettled(st):
        i, _, cnt = st
        return (i < 31) & (jnp.max(jnp.where((cnt == k) | (n_keys < k), 0, 1)) > 0)

    def body(st):
        i, ans, cnt = st
        for step in range(BITS_PER_EXIT_TEST):
            bit = jnp.int32(30) - i - step
            cand = ans + jnp.where(bit >= 0, lax.shift_left(jnp.int32(1), jnp.maximum(bit, 0)), 0)
            c = count_ge(cand)
            ans, cnt = jnp.where(c >= k, cand, ans), jnp.where(c >= k, c, cnt)
        return i + BITS_PER_EXIT_TEST, ans, cnt

    _, ans, cnt = lax.while_loop(unsettled, body, (jnp.int32(0), ans, cnt))
    return jnp.maximum(ans, INT_MIN + 1), cnt


def _prompt_attn_kernel(q_ref, qi_ref, wi_ref, ki_ref, k_ref, vt_ref, low_ref, o_ref, keys_ref, bias_ref,
                        m_ref, top_ref, kabs_ref, acc_ref, qz_ref, *, top_k):
    qb = pl.program_id(1)
    tq, ck, sub = Q_TILE, KEY_CHUNK, KEY_SUB
    n_chunks = (qb * tq + tq + ck - 1) // ck
    qpos = qb * tq + lax.broadcasted_iota(i32, (sub, tq), 1)
    krow = lax.broadcasted_iota(i32, (sub, tq), 0)

    qit = qi_ref[0].astype(f32).T.astype(bf16)
    w = wi_ref[0].T[0:N_IDX_HEADS, :] * (IDX_DIM ** -0.5)
    qt = q_ref[0].astype(f32).T.astype(bf16)
    kv_w = N_KV_HEADS * HEAD_DIM
    for g in range(N_KV_HEADS):
        q4 = jnp.concatenate([qt[(g * KV_GROUP + r) * HEAD_DIM:(g * KV_GROUP + r + 1) * HEAD_DIM, :]
                              for r in range(KV_GROUP)], axis=1)
        qz_ref[g] = jnp.zeros((kv_w, KV_GROUP * tq), bf16)
        qz_ref[g, g * HEAD_DIM:(g + 1) * HEAD_DIM, :] = q4

    def score_chunk(c, carry):
        for u in range(ck // sub):
            kc = ki_ref[0, c, u * sub:(u + 1) * sub, :]
            acc = jnp.zeros((sub, tq), f32)
            for h in range(N_IDX_HEADS):
                acc = acc + jnp.maximum(_dot(kc, qit[h * IDX_DIM:(h + 1) * IDX_DIM, :]), 0.0) * w[h:h + 1, :]
            causal = c * ck + u * sub + krow <= qpos
            keys_ref[c, u * sub:(u + 1) * sub, :] = jnp.where(causal, _score_key(acc), INT_MIN)
        return carry

    lax.fori_loop(0, n_chunks, score_chunk, 0)

    def count_ge(cand):
        lanes_of_acc = 8 * SUBLANE
        def body(c, acc):
            m = (keys_ref[c] >= cand).astype(f32)
            return acc + jnp.sum(m.reshape(ck // lanes_of_acc, lanes_of_acc, tq), axis=0)
        acc = lax.fori_loop(0, n_chunks, body, jnp.zeros((lanes_of_acc, tq), f32))
        return jnp.sum(acc, axis=0, keepdims=True)

    n_causal = (qb * tq + 1 + lax.broadcasted_iota(i32, (1, tq), 1)).astype(f32)
    thr, n_ge = _kth_largest_key(count_ge, float(top_k), n_causal)


    @pl.when(jnp.max(n_ge) > top_k)
    def _():
        need = top_k - count_ge(thr + 1)

        def body(c, before):
            kc = keys_ref[c]
            eq = kc == thr
            eqb = eq.astype(bf16)
            rank = _dot(low_ref[...], eqb) + before
            keys_ref[c] = jnp.where(eq & (rank >= need), INT_MIN, kc)
            return before + jnp.sum(eqb.astype(f32), axis=0, keepdims=True)

        lax.fori_loop(0, n_chunks, body, jnp.zeros((1, tq), f32))

    def to_bias(c, carry):
        bias_ref[c] = jnp.where(keys_ref[c] >= thr, 0.0, NEG_BIG)
        return carry

    lax.fori_loop(0, n_chunks, to_bias, 0)

    @pl.when(qb == 0)
    def _():
        def body(c, best):
            a = jnp.abs(k_ref[0, c].astype(f32))
            return jnp.maximum(best, jnp.max(a.reshape(ck // SUBLANE, SUBLANE, kv_w), axis=0))
        best = lax.fori_loop(0, k_ref.shape[1], body, jnp.zeros((SUBLANE, kv_w), f32))
        kabs_ref[...] = jnp.broadcast_to(jnp.max(best, axis=0, keepdims=True), (SUBLANE, kv_w))

    def sweep():
        acc_ref[...] = jnp.zeros(acc_ref.shape, f32)

        def accumulate(c, carry):
            bias = bias_ref[c]
            for g in range(N_KV_HEADS):
                s4 = _dot(k_ref[0, c], qz_ref[g])
                ps = [jnp.exp(s4[:, r * tq:(r + 1) * tq] + bias - top_ref[g * KV_GROUP + r]).astype(bf16)
                      for r in range(KV_GROUP)]
                acc_ref[g] = acc_ref[g] + _dot(vt_ref[c, g], jnp.concatenate(ps, axis=1))
            return carry

        lax.fori_loop(0, n_chunks, accumulate, 0)

    for g in range(N_KV_HEADS):
        bound = _dot(kabs_ref[...].astype(bf16), jnp.abs(qz_ref[g]))[0:1, :] * BOUND_MARGIN
        for r in range(KV_GROUP):
            top_ref[g * KV_GROUP + r] = bound[:, r * tq:(r + 1) * tq]
    sweep()
    smallest = acc_ref[0, HEAD_DIM:HEAD_DIM + 1, :]
    for g in range(1, N_KV_HEADS):
        smallest = jnp.minimum(smallest, acc_ref[g, HEAD_DIM:HEAD_DIM + 1, :])

    @pl.when(jnp.min(smallest) < MIN_TRUSTED_SUM)
    def _():
        part = m_ref.shape[1]
        m_ref[...] = jnp.full(m_ref.shape, NEG_BIG, f32)

        def col_max(c, carry):
            bias = bias_ref[c]
            for g in range(N_KV_HEADS):
                s4 = _dot(k_ref[0, c], qz_ref[g])
                for r in range(KV_GROUP):
                    h = g * KV_GROUP + r
                    s = (s4[:, r * tq:(r + 1) * tq] + bias).reshape(ck // part, part, tq)
                    m_ref[h] = jnp.maximum(m_ref[h], jnp.max(s, axis=0))
            return carry

        lax.fori_loop(0, n_chunks, col_max, 0)
        for h in range(N_HEADS):
            top_ref[h] = jnp.max(m_ref[h], axis=0, keepdims=True)
        sweep()

    heads = []
    for g in range(N_KV_HEADS):
        o4 = acc_ref[g, 0:HEAD_DIM, :] / acc_ref[g, HEAD_DIM:HEAD_DIM + 1, :]
        heads += [o4[:, r * tq:(r + 1) * tq] for r in range(KV_GROUP)]
    o_ref[0] = jnp.concatenate(heads, axis=0).T.astype(o_ref.dtype)


def _prompt_attention(q, qi, wi, ki, k, vt):
    b, s, _ = q.shape
    top_k = min(TOPK_MAX, s // 4)
    tq, ck = Q_TILE, KEY_CHUNK
    nq, nc = s // tq, s // ck
    gq = KV_GROUP * tq
    kv_w = N_KV_HEADS * HEAD_DIM
    low = jnp.asarray(np.arange(ck)[:, None] > np.arange(ck)[None, :], bf16)
    per_query = lambda w: pl.BlockSpec((1, tq, w), lambda i, j: (i, j, 0))
    return pl.pallas_call(
        functools.partial(_prompt_attn_kernel, top_k=top_k),
        grid=(b, nq),
        in_specs=[
            per_query(q.shape[2]), per_query(qi.shape[2]), per_query(wi.shape[2]),
            pl.BlockSpec((1, nc, ck, IDX_DIM), lambda i, j: (i, 0, 0, 0)),
            pl.BlockSpec((1, nc, ck, kv_w), lambda i, j: (i, 0, 0, 0)),
            pl.BlockSpec((nc, N_KV_HEADS, V_ROWS, ck), lambda i, j: (i, 0, 0, 0)),
            _resident((ck, ck)),
        ],
        out_specs=per_query(q.shape[2]),
        out_shape=jax.ShapeDtypeStruct(q.shape, bf16),
        scratch_shapes=[pltpu.VMEM((nc, ck, tq), i32), pltpu.VMEM((nc, ck, tq), f32),
                        pltpu.VMEM((N_HEADS, SOFTMAX_PARTIALS, tq), f32), pltpu.VMEM((N_HEADS, 1, tq), f32),
                        pltpu.VMEM((SUBLANE, kv_w), f32),
                        pltpu.VMEM((N_KV_HEADS, V_ROWS, gq), f32),
                        pltpu.VMEM((N_KV_HEADS, kv_w, gq), bf16)],
        compiler_params=_params("parallel", "arbitrary"),
        name="prompt_attention",
    )(q, qi, wi, ki.reshape(b, nc, ck, IDX_DIM), k.reshape(b, nc, ck, kv_w), vt, low)


def _sample_select_kernel(pt_ref, qi_ref, wi_ref, kin_ref, tri_ref, lt_ref, *rest, top_k, n_pages):
    page_refs, (sel_ref, seln_ref, keys_ref) = rest[:PAGES_PER_STEP], rest[PAGES_PER_STEP:]
    j = pl.program_id(1)
    qi = qi_ref[0]
    w = wi_ref[0]

    for i in range(PAGES_PER_STEP):
        s = _dot(qi, page_refs[i][0].astype(bf16))
        score = jnp.sum(jnp.maximum(s, 0.0) * w, axis=0, keepdims=True)
        keys_ref[pl.ds(j * PAGES_PER_STEP + i, 1), :] = _score_key(score)

    @pl.when(j == pl.num_programs(1) - 1)
    def _():
        kn = kin_ref[0].astype(bf16).astype(f32)
        s_new = jnp.sum(qi.astype(f32) * kn, axis=1, keepdims=True)
        key_new = _score_key(jnp.sum(jnp.maximum(s_new, 0.0) * w, axis=0, keepdims=True))
        keys = keys_ref[...]

        def count_ge(cand):
            c = jnp.sum((keys >= cand).astype(f32), axis=1, keepdims=True)
            return jnp.sum(c, axis=0, keepdims=True) + (key_new >= cand).astype(f32)

        thr, _ = _kth_largest_key(count_ge, float(top_k), jnp.full((1, 1), n_pages * PAGE_SIZE + 1, f32))
        need = top_k - (count_ge(thr + 1))
        eq = keys == thr
        eqb = eq.astype(bf16)
        in_row = _dot(eqb, tri_ref[...])
        per_page = jnp.sum(eqb.astype(f32), axis=1, keepdims=True)
        before = _dot(lt_ref[...], per_page.astype(bf16) * jnp.ones((1, LANE), bf16))[:, 0:1]
        rank = in_row + before
        sel = (keys > thr) | (eq & (rank < need))
        sel_ref[0] = sel.astype(f32)
        n_eq = jnp.sum(per_page, axis=0, keepdims=True)
        sel_new = (key_new > thr) | ((key_new == thr) & (n_eq < need))
        seln_ref[0] = jnp.broadcast_to(sel_new.astype(f32), (1, LANE))


def _sample_attend_kernel(pt_ref, q_ref, sel_ref, seln_ref, kn_ref, vn_ref, gm_ref, *rest, n_pages):
    k_refs = rest[:PAGES_PER_STEP]
    v_refs = rest[PAGES_PER_STEP:2 * PAGES_PER_STEP]
    o_ref, m_ref, l_ref, acc_ref = rest[2 * PAGES_PER_STEP:]
    j = pl.program_id(1)
    q = q_ref[0]

    @pl.when(j == 0)
    def _():
        m_ref[...] = jnp.full(m_ref.shape, NEG_BIG, f32)
        l_ref[...] = jnp.zeros(l_ref.shape, f32)
        acc_ref[...] = jnp.zeros(acc_ref.shape, f32)

    def update(s, vals):
        m = m_ref[...]
        m_new = jnp.maximum(m, jnp.max(s, axis=1, keepdims=True))
        alpha = jnp.exp(m - m_new)
        p = jnp.exp(s - m_new)
        l_ref[...] = alpha * l_ref[...] + jnp.sum(p, axis=1, keepdims=True)
        acc_ref[...] = alpha * acc_ref[...] + vals(p)
        m_ref[...] = m_new

    for i in range(PAGES_PER_STEP):
        sel = sel_ref[0, pl.ds(j * PAGES_PER_STEP + i, 1), :] > 0.5
        s = jnp.where(sel, _dot(q, k_refs[i][0].astype(bf16)), NEG_BIG)
        update(s, lambda p, i=i: _dot_nt(p.astype(bf16), v_refs[i][0].astype(bf16)))

    @pl.when(j == pl.num_programs(1) - 1)
    def _():
        kn = kn_ref[0].astype(bf16).astype(f32)
        vn = vn_ref[0].astype(bf16).astype(f32)
        s = jnp.sum(q.astype(f32) * kn, axis=1, keepdims=True)
        s = jnp.where(seln_ref[0][:, 0:1] > 0.5, s, NEG_BIG)
        update(s, lambda p: p.astype(bf16).astype(f32) * vn)
        full = acc_ref[...] / l_ref[...] * gm_ref[...]
        out = full[:, 0:HEAD_DIM]
        for g in range(1, N_KV_HEADS):
            out = out + full[:, g * HEAD_DIM:(g + 1) * HEAD_DIM]
        o_ref[0] = out.astype(o_ref.dtype)


def _sample_attention(q, k_new, v_new, qi, ki_new, wi, cache_k, cache_v, cache_kidx, page_table):
    db = q.shape[0]
    n_pages = page_table.shape[1]
    past = n_pages * PAGE_SIZE
    top_k = min(TOPK_MAX, (past + 1) // 4)
    pps = PAGES_PER_STEP
    steps = n_pages // pps
    kv_w = N_KV_HEADS * HEAD_DIM
    n_pool = cache_k.shape[0]
    ck2 = cache_k.transpose(0, 2, 3, 1).reshape(n_pool, kv_w, PAGE_SIZE)
    cv2 = cache_v.transpose(0, 2, 3, 1).reshape(n_pool, kv_w, PAGE_SIZE)
    cki = cache_kidx.transpose(0, 2, 1)

    def page_spec(width, i):
        return pl.BlockSpec((1, width, PAGE_SIZE), lambda b, j, pt, i=i: (pt[b, j * pps + i], 0, 0))

    per_seq = lambda *shape: pl.BlockSpec((1,) + shape, lambda b, j, pt: (b,) + (0,) * len(shape))
    const = lambda *shape: pl.BlockSpec(shape, lambda b, j, pt: (0,) * len(shape))
    tri = jnp.asarray(np.arange(PAGE_SIZE)[:, None] < np.arange(PAGE_SIZE)[None, :], bf16)
    lower = jnp.asarray(np.arange(n_pages)[:, None] > np.arange(n_pages)[None, :], bf16)

    sel, sel_new = pl.pallas_call(
        functools.partial(_sample_select_kernel, top_k=top_k, n_pages=n_pages),
        grid_spec=pltpu.PrefetchScalarGridSpec(
            num_scalar_prefetch=1,
            grid=(db, steps),
            in_specs=[per_seq(N_IDX_HEADS, IDX_DIM), per_seq(N_IDX_HEADS, 1), per_seq(1, IDX_DIM),
                      const(PAGE_SIZE, PAGE_SIZE), const(n_pages, n_pages)]
                     + [page_spec(IDX_DIM, i) for i in range(pps)],
            out_specs=[per_seq(n_pages, PAGE_SIZE), per_seq(1, LANE)],
            scratch_shapes=[pltpu.VMEM((n_pages, PAGE_SIZE), i32)],
        ),
        out_shape=[jax.ShapeDtypeStruct((db, n_pages, PAGE_SIZE), f32), jax.ShapeDtypeStruct((db, 1, LANE), f32)],
        compiler_params=_params("parallel", "arbitrary"),
        name="sample_select",
    )(page_table, qi.reshape(db, N_IDX_HEADS, IDX_DIM), (wi * (IDX_DIM ** -0.5)).reshape(db, N_IDX_HEADS, 1),
      ki_new.reshape(db, 1, IDX_DIM), tri, lower, *([cki] * pps))

    group_mask = np.repeat(np.arange(N_HEADS)[:, None] // KV_GROUP == np.arange(N_KV_HEADS)[None, :], HEAD_DIM, axis=1)
    q_bd = jnp.where(group_mask[None], jnp.tile(q.reshape(db, N_HEADS, HEAD_DIM), (1, 1, N_KV_HEADS)), 0).astype(bf16)
    out = pl.pallas_call(
        functools.partial(_sample_attend_kernel, n_pages=n_pages),
        grid_spec=pltpu.PrefetchScalarGridSpec(
            num_scalar_prefetch=1,
            grid=(db, steps),
            in_specs=[per_seq(N_HEADS, kv_w), per_seq(n_pages, PAGE_SIZE), per_seq(1, LANE),
                      per_seq(1, kv_w), per_seq(1, kv_w), const(N_HEADS, kv_w)]
                     + [page_spec(kv_w, i) for i in range(pps)] * 2,
            out_specs=per_seq(N_HEADS, HEAD_DIM),
            scratch_shapes=[pltpu.VMEM((N_HEADS, 1), f32), pltpu.VMEM((N_HEADS, 1), f32), pltpu.VMEM((N_HEADS, kv_w), f32)],
        ),
        out_shape=jax.ShapeDtypeStruct((db, N_HEADS, HEAD_DIM), bf16),
        compiler_params=_params("parallel", "arbitrary"),
        name="sample_attend",
    )(page_table, q_bd, sel, sel_new, k_new.reshape(db, 1, kv_w), v_new.reshape(db, 1, kv_w),
      jnp.asarray(group_mask, f32), *([ck2] * pps), *([cv2] * pps))
    return out.reshape(db, N_HEADS * HEAD_DIM)


def _rglru_coeffs(xc, wr_ref, br_ref, wi_ref, bi_ref, lam_ref):
    xb = xc.astype(bf16)
    r = jax.nn.sigmoid(_dot(xb, wr_ref[...]) + br_ref[...])
    i = jax.nn.sigmoid(_dot(xb, wi_ref[...]) + bi_ref[...])
    z = -lam_ref[...]
    softplus = jnp.maximum(z, 0.0) + jnp.log1p(jnp.exp(-jnp.abs(z)))
    log_a = -LRU_C * r * softplus
    a = jnp.exp(log_a)
    th = jnp.tanh(log_a)
    b = jnp.sqrt(-2.0 * th / (1.0 - th)) * (i * xc)
    return a, b


def _rglru_seq_kernel(xr_ref, yr_ref, cw_ref, cb_ref, wr_ref, br_ref, wi_ref, bi_ref, lam_ref,
                      o_ref, hl_ref, xprev_ref, h_ref, a_ref, b_ref):
    t = pl.program_id(1)
    tc, d = xr_ref.shape[1], xr_ref.shape[2]

    @pl.when(t == 0)
    def _():
        xprev_ref[...] = jnp.zeros(xprev_ref.shape, f32)
        h_ref[...] = jnp.zeros(h_ref.shape, f32)

    xr = xr_ref[0]
    xcat = jnp.concatenate([xprev_ref[...], xr], axis=0)
    xc = cb_ref[...] + cw_ref[0:1, :] * xcat[SUBLANE - 3:SUBLANE - 3 + tc]
    for j in range(1, CONV_W):
        xc = xc + cw_ref[j:j + 1, :] * xcat[SUBLANE - 3 + j:SUBLANE - 3 + j + tc]
    xprev_ref[...] = xr[tc - SUBLANE:, :]
    a, b = _rglru_coeffs(xc, wr_ref, br_ref, wi_ref, bi_ref, lam_ref)
    a_ref[...] = a
    b_ref[...] = b
    row = lax.broadcasted_iota(i32, (SUBLANE, d), 0)

    def slab(s, h_prev):
        r0 = pl.multiple_of(s * SUBLANE, SUBLANE)
        a8 = a_ref[pl.ds(r0, SUBLANE), :]
        b8 = b_ref[pl.ds(r0, SUBLANE), :]
        for sh in (1, 2, 4):
            keep = row >= sh
            b8 = jnp.where(keep, a8 * pltpu.roll(b8, sh, 0) + b8, b8)
            a8 = jnp.where(keep, a8 * pltpu.roll(a8, sh, 0), a8)
        h8 = a8 * h_prev + b8
        o_ref[0, pl.ds(r0, SUBLANE), :] = (h8 * _gelu(yr_ref[0, pl.ds(r0, SUBLANE), :])).astype(o_ref.dtype)
        return h8[SUBLANE - 1:SUBLANE, :]

    h_last = lax.fori_loop(0, tc // SUBLANE, slab, h_ref[...])
    h_ref[...] = h_last
    hl_ref[0] = h_last


def _block_diag(w):
    n, a, b = w.shape
    eye = jnp.eye(n, dtype=w.dtype)
    return (eye[:, None, :, None] * w[:, :, None, :]).reshape(n * a, n * b).astype(bf16)


def _rglru_prompt(xr, yr, conv_w, conv_b, w_rgate, b_rgate, w_igate, b_igate, lru_lambda):
    b, t, d = xr.shape
    tc = RNN_CHUNK
    vec = lambda v: v.reshape(1, d)
    seq = pl.BlockSpec((1, tc, d), lambda i, j: (i, j, 0))
    out, h_last = pl.pallas_call(
        _rglru_seq_kernel,
        grid=(b, t // tc),
        in_specs=[seq, seq, _resident((CONV_W, d)), _resident((1, d)), _resident((d, d)), _resident((1, d)),
                  _resident((d, d)), _resident((1, d)), _resident((1, d))],
        out_specs=[seq, pl.BlockSpec((1, 1, d), lambda i, j: (i, 0, 0))],
        out_shape=[jax.ShapeDtypeStruct((b, t, d), bf16), jax.ShapeDtypeStruct((b, 1, d), f32)],
        scratch_shapes=[pltpu.VMEM((SUBLANE, d), f32), pltpu.VMEM((1, d), f32),
                        pltpu.VMEM((tc, d), f32), pltpu.VMEM((tc, d), f32)],
        compiler_params=_params("parallel", "arbitrary"),
        name="rglru_prompt",
    )(xr, yr, conv_w, vec(conv_b), _block_diag(w_rgate), vec(b_rgate), _block_diag(w_igate), vec(b_igate),
      vec(lru_lambda))
    return out, h_last.reshape(b, d)


def _rglru_step_kernel(xr_ref, yr_ref, c0_ref, c1_ref, c2_ref, h0_ref, cw_ref, cb_ref, wr_ref, br_ref, wi_ref,
                       bi_ref, lam_ref, o_ref, h_ref):
    xc = cb_ref[...] + cw_ref[0:1, :] * c0_ref[...]
    xc = xc + cw_ref[1:2, :] * c1_ref[...]
    xc = xc + cw_ref[2:3, :] * c2_ref[...]
    xc = xc + cw_ref[3:4, :] * xr_ref[...]
    a, b = _rglru_coeffs(xc, wr_ref, br_ref, wi_ref, bi_ref, lam_ref)
    h = a * h0_ref[...] + b
    h_ref[...] = h
    o_ref[...] = (h * _gelu(yr_ref[...])).astype(o_ref.dtype)


def _rglru_sample(xr, yr, state_conv, h0, conv_w, conv_b, w_rgate, b_rgate, w_igate, b_igate, lru_lambda):
    db, d = xr.shape
    vec = lambda v: v.reshape(1, d)
    return pl.pallas_call(
        _rglru_step_kernel,
        out_shape=[jax.ShapeDtypeStruct((db, d), bf16), jax.ShapeDtypeStruct((db, d), f32)],
        compiler_params=pltpu.CompilerParams(vmem_limit_bytes=VMEM_LIMIT_BYTES),
        name="rglru_sample",
    )(xr, yr, state_conv[:, 0], state_conv[:, 1], state_conv[:, 2], h0, conv_w, vec(conv_b),
      _block_diag(w_rgate), vec(b_rgate), _block_diag(w_igate), vec(b_igate), vec(lru_lambda))


def _top_rows(x, k, payload=None):
    n = x.shape[0]
    row = lax.broadcasted_iota(i32, x.shape, 0)
    vals, picked = [], []
    for _ in range(k):
        m = jnp.max(x, axis=0, keepdims=True)
        r = jnp.min(jnp.where(x == m, row, n), axis=0, keepdims=True)
        hit = row == r
        vals.append(m)
        picked.append(r if payload is None else jnp.sum(jnp.where(hit, payload, 0), axis=0, keepdims=True))
        x = jnp.where(hit, -jnp.inf, x)
    return jnp.concatenate(vals, axis=0), jnp.concatenate(picked, axis=0)


_PAIR_COUNTS = [PEER_TOPK // (r1 + 1) for r1 in range(PEER_TOPK)]


def _back_kernel(x_ref, ao_ref, ro_ref, ga_ref, gr_ref, woa_ref, wor_ref, wout_ref, g2_ref, wqt_ref, k1_ref, k2_ref,
                 x1_ref, hn_ref, eidx_ref, gate_ref):
    mixed = ga_ref[...] * _dot(ao_ref[...], woa_ref[...]) + gr_ref[...] * _dot(ro_ref[...], wor_ref[...])
    x1 = x_ref[...] + _dot(mixed.astype(bf16), wout_ref[...])
    x1_ref[...] = x1
    hn = (x1 * lax.rsqrt(jnp.mean(x1 * x1, axis=-1, keepdims=True) + NORM_EPS) * g2_ref[...]).astype(bf16)
    hn_ref[...] = hn.astype(f32)
    qt = _dot_nt(wqt_ref[...], hn).astype(bf16)
    half = PEER_KEY_DIM // 2
    for h in range(PEER_HEADS):
        base = h * PEER_KEY_DIM
        v1, i1 = _top_rows(_dot(k1_ref[...], qt[base:base + half, :]), PEER_TOPK)
        v2, i2 = _top_rows(_dot(k2_ref[...], qt[base + half:base + PEER_KEY_DIM, :]), PEER_TOPK)
        cand, cid = [], []
        for r1, n2 in enumerate(_PAIR_COUNTS):
            cand.append(v1[r1:r1 + 1, :] + v2[0:n2, :])
            cid.append(i1[r1:r1 + 1, :] * N_KEYS + i2[0:n2, :])
        pad = -sum(_PAIR_COUNTS) % SUBLANE
        tm = v1.shape[1]
        cand.append(jnp.full((pad, tm), -jnp.inf, f32))
        cid.append(jnp.zeros((pad, tm), i32))
        sv, eidx = _top_rows(jnp.concatenate(cand, axis=0), PEER_TOPK, jnp.concatenate(cid, axis=0))
        p = jnp.exp(sv - sv[0:1, :])
        eidx_ref[h * PEER_TOPK:(h + 1) * PEER_TOPK, :] = eidx
        gate_ref[h * PEER_TOPK:(h + 1) * PEER_TOPK, :] = p / jnp.sum(p, axis=0, keepdims=True)


def _back(x, attn_o, rnn_o, g_a, g_r, w_o_attn, w_o_rnn, w_out, norm2_g, w_peer_q, peer_k1, peer_k2):
    n, d = x.shape
    tm = TOKEN_TILE
    row = lambda w: pl.BlockSpec((tm, w), lambda i: (i, 0))
    col = pl.BlockSpec((PEER_SLOTS, tm), lambda i: (0, i))
    ws = [w_o_attn.astype(bf16), w_o_rnn.astype(bf16), w_out.astype(bf16), norm2_g.reshape(1, d),
          w_peer_q.T.astype(bf16), peer_k1.astype(bf16), peer_k2.astype(bf16)]
    return pl.pallas_call(
        _back_kernel,
        grid=(n // tm,),
        in_specs=[row(d), row(attn_o.shape[1]), row(rnn_o.shape[1]), row(d), row(d)] + [_resident(w.shape) for w in ws],
        out_specs=[row(d), row(d), col, col],
        out_shape=[jax.ShapeDtypeStruct((n, d), f32), jax.ShapeDtypeStruct((n, d), f32),
                   jax.ShapeDtypeStruct((PEER_SLOTS, n), i32), jax.ShapeDtypeStruct((PEER_SLOTS, n), f32)],
        compiler_params=_params("parallel"),
        name="back",
    )(x, attn_o, rnn_o, g_a, g_r, *ws)


def _pack_rows(table, reverse_pieces=False):
    e, d = table.shape
    bits = lax.bitcast_convert_type(table.astype(bf16), jnp.uint16).astype(jnp.uint32).reshape(e // 2, 2, d // LANE, LANE)
    if reverse_pieces:
        bits = bits[:, :, ::-1, :]
    return lax.bitcast_convert_type(bits[:, 0] | (bits[:, 1] << 16), i32).reshape(e // 2 * (d // LANE), LANE)


def _gather_tiles(table_ref, row_ref, t):
    starts = row_ref.at[t]
    tiles = [table_ref[pl.ds(pl.multiple_of(starts[s], SUBLANE), SUBLANE), :] for s in range(PEER_SLOTS)]
    return pltpu.bitcast(jnp.concatenate(tiles, axis=0), bf16)


def _token_tiles(x8):
    tiles = x8.reshape(SUBLANE, x8.shape[1] // LANE, LANE)
    return [tiles[j] for j in range(SUBLANE)]


def _peer_act_kernel(start_ref, hn_ref, gate_ref, u_ref, coef_ref):
    row = lax.broadcasted_iota(i32, (SUBLANE, LANE), 0)
    n_blk = PEER_X // LANE

    def token_group(gi, carry):
        t0 = pl.multiple_of(gi * SUBLANE, SUBLANE)
        acts = [jnp.zeros((SUBLANE, LANE), f32) for _ in range(n_blk)]
        h_tiles = _token_tiles(hn_ref[pl.ds(t0, SUBLANE), :])
        for j in range(SUBLANE):
            rows = _gather_tiles(u_ref, start_ref, t0 + j)
            r = _dot_nt(h_tiles[j].astype(bf16), rows)
            for blk in range(n_blk):
                z = pltpu.roll(r[:, blk * LANE:(blk + 1) * LANE], 0, 1, stride=2, stride_axis=0)
                for sh in (4, 2, 1):
                    z = z + pltpu.roll(z, sh, 0)
                acts[blk] = jnp.where(row == j, z, acts[blk])
        for blk in range(n_blk):
            cols = slice(blk * LANE, (blk + 1) * LANE)
            coef = gate_ref[pl.ds(t0, SUBLANE), cols] * _gelu(acts[blk])
            coef_ref[pl.ds(t0, SUBLANE), cols] = pltpu.roll(coef, LANE - (PACKED_ROWS - 2), 1)
        return carry

    lax.fori_loop(0, hn_ref.shape[0] // SUBLANE, token_group, 0)


def _peer_out_kernel(start_ref, coef_ref, x1_ref, v_ref, y_ref):
    n_blk = PEER_X // LANE

    def token_group(gi, carry):
        t0 = pl.multiple_of(gi * SUBLANE, SUBLANE)
        coef = coef_ref[pl.ds(t0, SUBLANE), :]
        outs = []
        for j in range(SUBLANE):
            rows = _gather_tiles(v_ref, start_ref, t0 + j)
            pick = [pltpu.roll(jnp.broadcast_to(coef[j:j + 1, blk * LANE:(blk + 1) * LANE], (SUBLANE, LANE)),
                               0, 1, stride=2, stride_axis=0) for blk in range(n_blk)]
            outs.append(_dot(jnp.concatenate(pick, axis=1).astype(bf16), rows))
        x1 = x1_ref[pl.ds(t0, SUBLANE), :]
        y_ref[pl.ds(t0, SUBLANE), :] = x1 + jnp.stack(outs, axis=0).reshape(x1.shape)
        return carry

    lax.fori_loop(0, x1_ref.shape[0] // SUBLANE, token_group, 0)


def _peer(hn, x1, eidx_t, gate_t, peer_u, peer_v):
    n, d = x1.shape
    tt = PEER_TOKENS
    rows = d // LANE
    eidx = eidx_t.T
    start = (eidx >> 1) * SUBLANE
    col = jnp.tile(jnp.arange(PACKED_ROWS, dtype=i32), PEER_SLOTS)[None, :]
    gate = jnp.where(col == PACKED_ROWS - 2 + jnp.repeat(eidx & 1, PACKED_ROWS, axis=1),
                     jnp.repeat(gate_t.T, PACKED_ROWS, axis=1), 0.0)
    smem = pl.BlockSpec((tt, PEER_SLOTS), lambda i: (i, 0), memory_space=pltpu.SMEM)
    tokens = pl.BlockSpec((tt, d), lambda i: (i, 0))
    wide = pl.BlockSpec((tt, PEER_X), lambda i: (i, 0))
    table = _resident((peer_u.shape[0] // 2 * rows, LANE))
    coef = pl.pallas_call(
        _peer_act_kernel,
        grid=(n // tt,),
        in_specs=[smem, tokens, wide, table],
        out_specs=wide,
        out_shape=jax.ShapeDtypeStruct((n, PEER_X), f32),
        compiler_params=_params("parallel"),
        name="peer_act",
    )(start, hn, gate, _pack_rows(peer_u, reverse_pieces=True))
    y = pl.pallas_call(
        _peer_out_kernel,
        grid=(n // tt,),
        in_specs=[smem, wide, tokens, table],
        out_specs=tokens,
        out_shape=jax.ShapeDtypeStruct((n, d), f32),
        compiler_params=_params("parallel"),
        name="peer_out",
    )(start, coef, x1, _pack_rows(peer_v))
    return y


def kernel(x_prompt, x_sample, cache_k, cache_v, cache_kidx, page_table, state_conv, state_rglru, norm1_g, w_in, b_gates, q_norm_g, k_norm_g, conv_w, conv_b, w_rgate, b_rgate, w_igate, b_igate, lru_lambda, w_o_attn, w_o_rnn, w_out, norm2_g, w_peer_q, peer_k1, peer_k2, peer_u, peer_v):
    bp, sp, d = x_prompt.shape
    db, ts, _ = x_sample.shape
    assert ts == 1, "the sample path handles one new token per sequence"
    d_rnn = conv_w.shape[1]
    past = page_table.shape[1] * PAGE_SIZE
    n_p, n_s = bp * sp, db * ts
    assert n_p % TOKEN_TILE == 0
    n_sp = -(-n_s // KEY_CHUNK) * KEY_CHUNK
    rnn_w = (conv_w, conv_b, w_rgate, b_rgate, w_igate, b_igate, lru_lambda)
    front_w = (norm1_g, w_in, b_gates, q_norm_g, k_norm_g, d_rnn)
    back_w = (w_o_attn, w_o_rnn, w_out, norm2_g, w_peer_q, peer_k1, peer_k2)
    pad = lambda a: jnp.concatenate([a, jnp.zeros((n_sp - n_s,) + a.shape[1:], a.dtype)])
    heads = lambda a, b_, t_: a.reshape(b_, t_, N_KV_HEADS, HEAD_DIM)
    flat = lambda a: a.reshape(a.shape[0], N_KV_HEADS * HEAD_DIM)

    xp = x_prompt.reshape(n_p, d)
    q, k_p, v_p, qi, ki_p, wi, xr_p, yr, g_a, g_r, kb, kib, vt = _front(
        xp, jnp.tile(jnp.arange(sp, dtype=f32), bp), sp, *front_w)
    pr = lambda a: a.reshape(bp, sp, a.shape[1])
    attn_p = _prompt_attention(pr(q), pr(qi), pr(wi), pr(kib), pr(kb), vt)
    rnn_p, h_p = _rglru_prompt(pr(xr_p), pr(yr), *rnn_w)
    x1, hn, eidx_t, gate_t = _back(xp, attn_p.reshape(n_p, -1), rnn_p.reshape(n_p, d_rnn), g_a, g_r, *back_w)
    y_p = _peer(hn, x1, eidx_t, gate_t, peer_u, peer_v)

    xs = pad(x_sample.reshape(n_s, d))
    pos_s = pad(past + jnp.tile(jnp.arange(ts, dtype=f32), db))
    q, k_s, v_s, qi, ki_s, wi, xr_s, yr, g_a, g_r, _, _, _ = _front(xs, pos_s, n_sp, *front_w)
    sm = lambda a: a[:n_s]
    k_s, v_s = (sm(a[0].transpose(2, 0, 1)) for a in (k_s, v_s))
    ki_s = sm(ki_s[0].T)
    attn_s = _sample_attention(sm(q), flat(k_s), flat(v_s), sm(qi), ki_s, sm(wi[:, :N_IDX_HEADS]),
                               cache_k, cache_v, cache_kidx, page_table)
    rnn_s, h_s = _rglru_sample(sm(xr_s), sm(yr), state_conv, state_rglru, *rnn_w)
    x1, hn, eidx_t, gate_t = _back(xs, pad(attn_s), pad(rnn_s), g_a, g_r, *back_w)
    y_s = _peer(hn, x1, eidx_t, gate_t, peer_u, peer_v)

    conv_p = pr(xr_p)[:, sp - (CONV_W - 1):, :]
    conv_s = jnp.concatenate([state_conv[:, ts:, :], sm(xr_s).reshape(db, ts, d_rnn)], axis=1)
    return (y_p.reshape(bp, sp, d), sm(y_s).reshape(db, ts, d),
            k_p.transpose(0, 3, 1, 2), v_p.transpose(0, 3, 1, 2), ki_p.transpose(0, 2, 1),
            conv_p, h_p,
            heads(k_s, db, ts), heads(v_s, db, ts), ki_s.reshape(db, ts, IDX_DIM),
            conv_s, h_s)
```

```python
import functools

import numpy as np
import jax
import jax.numpy as jnp
from jax import lax
from jax.experimental import pallas as pl
from jax.experimental.pallas import tpu as pltpu

f32 = jnp.float32
bf16 = jnp.bfloat16
i32 = jnp.int32

N_HEADS = 16
HEAD_DIM = 64
N_KV_HEADS = 4
KV_GROUP = N_HEADS // N_KV_HEADS
ROT_DIM = HEAD_DIM // 4
ROT_HALF = ROT_DIM // 2
ROPE_THETA = 500000.0
QK_EPS = 1e-6
N_IDX_HEADS = 8
IDX_DIM = 64
TOPK_MAX = 256
PAGE_SIZE = 128
CONV_W = 4
LRU_C = 8.0
PEER_HEADS = 8
PEER_KEY_DIM = 128
N_KEYS = 128
PEER_TOPK = 16
PEER_SLOTS = PEER_HEADS * PEER_TOPK
NORM_EPS = 1e-6

LANE = 128
SUBLANE = 8
PACKED_ROWS = 16
VMEM_LIMIT_BYTES = 56 * 1024 * 1024

INT_MIN = -(2 ** 31)
NEG_BIG = -1e30

TOKEN_TILE = 256
Q_TILE = LANE
KEY_CHUNK = 512
KEY_SUB = 256
SOFTMAX_PARTIALS = 32
V_ROWS = HEAD_DIM + PACKED_ROWS
BOUND_MARGIN = 1.02
MIN_TRUSTED_SUM = 1e-25
BITS_PER_EXIT_TEST = 4
RNN_CHUNK = 256
PEER_TOKENS = 256
PAGES_PER_STEP = 8
PEER_X = PEER_SLOTS * PACKED_ROWS


def _params(*sem):
    return pltpu.CompilerParams(dimension_semantics=sem, vmem_limit_bytes=VMEM_LIMIT_BYTES)


def _resident(shape):
    nd = len(shape)
    return pl.BlockSpec(shape, lambda *_: (0,) * nd, pipeline_mode=pl.Buffered(1))


def _gelu(x):
    return 0.5 * x * (1.0 + jnp.tanh(np.sqrt(2.0 / np.pi).astype(np.float32) * (x + 0.044715 * (x * x * x))))


def _dot(a, b):
    return jnp.dot(a, b, preferred_element_type=f32)


def _dot_nt(a, b):
    return lax.dot_general(a, b, (((1,), (1,)), ((), ())), preferred_element_type=f32)


def _front_kernel(x_ref, g1_ref, wa_ref, wb_ref, bg_ref, qg_ref, kg_ref, c_ref, s1_ref, s2_ref, bd_ref,
                  q_ref, k_ref, v_ref, qi_ref, ki_ref, wi_ref, xr_ref, yr_ref, ga_ref, gr_ref,
                  kb_ref, kib_ref, vt_ref, *, d_model, d_rnn):
    x = x_ref[...]
    h = (x * lax.rsqrt(jnp.mean(x * x, axis=-1, keepdims=True) + NORM_EPS) * g1_ref[...]).astype(bf16)
    cos, sin_hi, sin_lo = c_ref[...], s1_ref[...], s2_ref[...]
    bd = bd_ref[...]

    def rope(t):
        return t * cos + pltpu.roll(t, LANE - ROT_HALF, 1) * sin_hi + pltpu.roll(t, ROT_HALF, 1) * sin_lo

    def head_norm(t, g):
        sq = t * t
        hi = sq.astype(bf16)
        lo = (sq - hi.astype(f32)).astype(bf16)
        ss = _dot(hi, bd) + _dot(lo, bd)
        return t * lax.rsqrt(ss * (1.0 / HEAD_DIM) + QK_EPS) * g

    q_w = N_HEADS * HEAD_DIM
    kv_w = N_KV_HEADS * HEAD_DIM
    qi_w = N_IDX_HEADS * IDX_DIM
    pa = _dot(h, wa_ref[...])
    off = 0
    for j in range(q_w // LANE):
        t = pa[:, off + j * LANE: off + (j + 1) * LANE]
        q_ref[:, j * LANE:(j + 1) * LANE] = (rope(head_norm(t, qg_ref[...])) * (HEAD_DIM ** -0.5)).astype(bf16)
    off += q_w
    for j in range(kv_w // LANE):
        t = pa[:, off + j * LANE: off + (j + 1) * LANE]
        kt = rope(head_norm(t, kg_ref[...]))
        ktt = kt.T
        for u in range(LANE // HEAD_DIM):
            k_ref[0, j * (LANE // HEAD_DIM) + u] = ktt[u * HEAD_DIM:(u + 1) * HEAD_DIM, :]
        kb_ref[:, j * LANE:(j + 1) * LANE] = kt.astype(bf16)
    off += kv_w
    vv = pa[:, off:off + kv_w]
    vvt = vv.T
    for g in range(N_KV_HEADS):
        v_ref[0, g] = vvt[g * HEAD_DIM:(g + 1) * HEAD_DIM, :]
    vt = vvt.astype(bf16)
    ones_row = (lax.broadcasted_iota(i32, (V_ROWS - HEAD_DIM, vv.shape[0]), 0) == 0).astype(bf16)
    for g in range(N_KV_HEADS):
        vt_ref[0, g, 0:HEAD_DIM, :] = vt[g * HEAD_DIM:(g + 1) * HEAD_DIM, :]
        vt_ref[0, g, HEAD_DIM:V_ROWS, :] = ones_row
    off += kv_w
    for j in range(qi_w // LANE):
        t = pa[:, off + j * LANE: off + (j + 1) * LANE]
        qi_ref[:, j * LANE:(j + 1) * LANE] = rope(t).astype(bf16)
    off += qi_w
    ki_wide = rope(pa[:, off:off + LANE])
    ki_ref[0] = ki_wide.T[0:IDX_DIM, :]
    kib_ref[...] = ki_wide[:, :IDX_DIM].astype(bf16)
    off += LANE
    wi_ref[...] = pa[:, off:off + LANE] * (N_IDX_HEADS ** -0.5)

    xr_ref[...] = _dot(h, wb_ref[:, 0:d_rnn])
    yr_ref[...] = _dot(h, wb_ref[:, d_rnn:2 * d_rnn])
    o = 2 * d_rnn
    ga_ref[...] = jax.nn.sigmoid(_dot(h, wb_ref[:, o:o + d_model]) + bg_ref[:, 0:d_model])
    gr_ref[...] = jax.nn.sigmoid(_dot(h, wb_ref[:, o + d_model:o + 2 * d_model]) + bg_ref[:, d_model:2 * d_model])


def _rope_tables(pos):
    inv = ROPE_THETA ** (-jnp.arange(ROT_HALF, dtype=f32) / ROT_HALF)
    ang = pos.astype(f32)[:, None] * inv[None, :]
    c, s = jnp.cos(ang), jnp.sin(ang)
    n = pos.shape[0]
    z = lambda w: jnp.zeros((n, w), f32)
    cos = jnp.concatenate([c, c, jnp.ones((n, HEAD_DIM - ROT_DIM), f32)], axis=1)
    s_hi = jnp.concatenate([-s, z(HEAD_DIM - ROT_HALF)], axis=1)
    s_lo = jnp.concatenate([z(ROT_HALF), s, z(HEAD_DIM - ROT_DIM)], axis=1)
    rep = LANE // HEAD_DIM
    return jnp.tile(cos, (1, rep)), jnp.tile(s_hi, (1, rep)), jnp.tile(s_lo, (1, rep))


def _front(x, pos, seq, norm1_g, w_in, b_gates, q_norm_g, k_norm_g, d_rnn):
    n, d_model = x.shape
    q_w, kv_w, qi_w = N_HEADS * HEAD_DIM, N_KV_HEADS * HEAD_DIM, N_IDX_HEADS * IDX_DIM
    sizes = [q_w, kv_w, kv_w, qi_w, IDX_DIM, N_IDX_HEADS, d_rnn, d_rnn, d_model, d_model]
    offs = np.cumsum([0] + sizes)
    parts = [w_in[:, offs[i]:offs[i + 1]] for i in range(len(sizes))]
    padl = lambda w: jnp.pad(w, ((0, 0), (0, LANE - w.shape[1])))
    wa = jnp.concatenate(parts[:4] + [padl(parts[4]), padl(parts[5])], axis=1).astype(bf16)
    wb = jnp.concatenate(parts[6:], axis=1).astype(bf16)
    cos, s_hi, s_lo = _rope_tables(pos)
    rep = LANE // HEAD_DIM
    head_of_lane = np.arange(LANE) // HEAD_DIM
    bd = jnp.asarray(head_of_lane[:, None] == head_of_lane[None, :], bf16)
    tm = TOKEN_TILE
    row = lambda w: pl.BlockSpec((tm, w), lambda i: (i, 0))
    outs = [(q_w, bf16), (kv_w, f32), (kv_w, f32), (qi_w, bf16), (IDX_DIM, f32), (LANE, f32),
            (d_rnn, f32), (d_rnn, f32), (d_model, f32), (d_model, f32), (kv_w, bf16), (IDX_DIM, bf16)]
    per_chunk = KEY_CHUNK // tm
    assert n % KEY_CHUNK == 0
    tps = seq // tm
    assert n % seq == 0 and seq % tm == 0
    per_head = pl.BlockSpec((1, N_KV_HEADS, HEAD_DIM, tm), lambda i: (i // tps, 0, 0, i % tps))
    per_idx = pl.BlockSpec((1, IDX_DIM, tm), lambda i: (i // tps, 0, i % tps))
    special = {1: (per_head, (n // seq, N_KV_HEADS, HEAD_DIM, seq)), 2: (per_head, (n // seq, N_KV_HEADS, HEAD_DIM, seq)),
               4: (per_idx, (n // seq, IDX_DIM, seq))}
    vt_spec = pl.BlockSpec((1, N_KV_HEADS, V_ROWS, tm), lambda i: (i // per_chunk, 0, 0, i % per_chunk))
    vt_shape = jax.ShapeDtypeStruct((n // KEY_CHUNK, N_KV_HEADS, V_ROWS, KEY_CHUNK), bf16)
    return pl.pallas_call(
        functools.partial(_front_kernel, d_model=d_model, d_rnn=d_rnn),
        grid=(n // tm,),
        in_specs=[row(d_model), _resident((1, d_model)), _resident(wa.shape), _resident(wb.shape),
                  _resident((1, 2 * d_model)), _resident((1, LANE)), _resident((1, LANE)),
                  row(LANE), row(LANE), row(LANE), _resident((LANE, LANE))],
        out_specs=[special[i][0] if i in special else row(w) for i, (w, _) in enumerate(outs)] + [vt_spec],
        out_shape=[jax.ShapeDtypeStruct(special[i][1] if i in special else (n, w), dt)
                   for i, (w, dt) in enumerate(outs)] + [vt_shape],
        compiler_params=_params("parallel"),
        name="front",
    )(x, norm1_g.reshape(1, -1), wa, wb, b_gates.reshape(1, -1),
      jnp.tile(q_norm_g, rep).reshape(1, LANE), jnp.tile(k_norm_g, rep).reshape(1, LANE),
      cos, s_hi, s_lo, bd)


def _score_key(score):
    score = jnp.where(score == 0.0, 0.0, score)
    bits = pltpu.bitcast(score, i32)
    return bits ^ ((bits >> 31) & 0x7FFFFFFF)


def _kth_largest_key(count_ge, k, n_keys):
    c0 = count_ge(jnp.zeros(n_keys.shape, i32))
    ans = jnp.where(c0 >= k, 0, INT_MIN).astype(i32)
    cnt = jnp.where(c0 >= k, c0, n_keys)

    def unsettled(st):
        i, _, cnt = st
        return (i < 31) & (jnp.max(jnp.where((cnt == k) | (n_keys < k), 0, 1)) > 0)

    def body(st):
        i, ans, cnt = st
        for step in range(BITS_PER_EXIT_TEST):
            bit = jnp.int32(30) - i - step
            cand = ans + jnp.where(bit >= 0, lax.shift_left(jnp.int32(1), jnp.maximum(bit, 0)), 0)
            c = count_ge(cand)
            ans, cnt = jnp.where(c >= k, cand, ans), jnp.where(c >= k, c, cnt)
        return i + BITS_PER_EXIT_TEST, ans, cnt

    _, ans, cnt = lax.while_loop(unsettled, body, (jnp.int32(0), ans, cnt))
    return jnp.maximum(ans, INT_MIN + 1), cnt


def _prompt_attn_kernel(q_ref, qi_ref, wi_ref, ki_ref, k_ref, vt_ref, low_ref, o_ref, keys_ref, bias_ref,
                        m_ref, top_ref, kabs_ref, acc_ref, qz_ref, *, top_k):
    qb = pl.program_id(1)
    tq, ck, sub = Q_TILE, KEY_CHUNK, KEY_SUB
    n_chunks = (qb * tq + tq + ck - 1) // ck
    qpos = qb * tq + lax.broadcasted_iota(i32, (sub, tq), 1)
    krow = lax.broadcasted_iota(i32, (sub, tq), 0)

    qit = qi_ref[0].astype(f32).T.astype(bf16)
    w = wi_ref[0].T[0:N_IDX_HEADS, :] * (IDX_DIM ** -0.5)
    qt = q_ref[0].astype(f32).T.astype(bf16)
    kv_w = N_KV_HEADS * HEAD_DIM
    for g in range(N_KV_HEADS):
        q4 = jnp.concatenate([qt[(g * KV_GROUP + r) * HEAD_DIM:(g * KV_GROUP + r + 1) * HEAD_DIM, :]
                              for r in range(KV_GROUP)], axis=1)
        qz_ref[g] = jnp.zeros((kv_w, KV_GROUP * tq), bf16)
        qz_ref[g, g * HEAD_DIM:(g + 1) * HEAD_DIM, :] = q4

    def score_chunk(c, carry):
        for u in range(ck // sub):
            kc = ki_ref[0, c, u * sub:(u + 1) * sub, :]
            acc = jnp.zeros((sub, tq), f32)
            for h in range(N_IDX_HEADS):
                acc = acc + jnp.maximum(_dot(kc, qit[h * IDX_DIM:(h + 1) * IDX_DIM, :]), 0.0) * w[h:h + 1, :]
            causal = c * ck + u * sub + krow <= qpos
            keys_ref[c, u * sub:(u + 1) * sub, :] = jnp.where(causal, _score_key(acc), INT_MIN)
        return carry

    lax.fori_loop(0, n_chunks, score_chunk, 0)

    def count_ge(cand):
        lanes_of_acc = 8 * SUBLANE
        def body(c, acc):
            m = (keys_ref[c] >= cand).astype(f32)
            return acc + jnp.sum(m.reshape(ck // lanes_of_acc, lanes_of_acc, tq), axis=0)
        acc = lax.fori_loop(0, n_chunks, body, jnp.zeros((lanes_of_acc, tq), f32))
        return jnp.sum(acc, axis=0, keepdims=True)

    n_causal = (qb * tq + 1 + lax.broadcasted_iota(i32, (1, tq), 1)).astype(f32)
    thr, n_ge = _kth_largest_key(count_ge, float(top_k), n_causal)


    @pl.when(jnp.max(n_ge) > top_k)
    def _():
        need = top_k - count_ge(thr + 1)

        def body(c, before):
            kc = keys_ref[c]
            eq = kc == thr
            eqb = eq.astype(bf16)
            rank = _dot(low_ref[...], eqb) + before
            keys_ref[c] = jnp.where(eq & (rank >= need), INT_MIN, kc)
            return before + jnp.sum(eqb.astype(f32), axis=0, keepdims=True)

        lax.fori_loop(0, n_chunks, body, jnp.zeros((1, tq), f32))

    def to_bias(c, carry):
        bias_ref[c] = jnp.where(keys_ref[c] >= thr, 0.0, NEG_BIG)
        return carry

    lax.fori_loop(0, n_chunks, to_bias, 0)

    @pl.when(qb == 0)
    def _():
        def body(c, best):
            a = jnp.abs(k_ref[0, c].astype(f32))
            return jnp.maximum(best, jnp.max(a.reshape(ck // SUBLANE, SUBLANE, kv_w), axis=0))
        best = lax.fori_loop(0, k_ref.shape[1], body, jnp.zeros((SUBLANE, kv_w), f32))
        kabs_ref[...] = jnp.broadcast_to(jnp.max(best, axis=0, keepdims=True), (SUBLANE, kv_w))

    def sweep():
        acc_ref[...] = jnp.zeros(acc_ref.shape, f32)

        def accumulate(c, carry):
            bias = bias_ref[c]
            for g in range(N_KV_HEADS):
                s4 = _dot(k_ref[0, c], qz_ref[g])
                ps = [jnp.exp(s4[:, r * tq:(r + 1) * tq] + bias - top_ref[g * KV_GROUP + r]).astype(bf16)
                      for r in range(KV_GROUP)]
                acc_ref[g] = acc_ref[g] + _dot(vt_ref[c, g], jnp.concatenate(ps, axis=1))
            return carry

        lax.fori_loop(0, n_chunks, accumulate, 0)

    for g in range(N_KV_HEADS):
        bound = _dot(kabs_ref[...].astype(bf16), jnp.abs(qz_ref[g]))[0:1, :] * BOUND_MARGIN
        for r in range(KV_GROUP):
            top_ref[g * KV_GROUP + r] = bound[:, r * tq:(r + 1) * tq]
    sweep()
    smallest = acc_ref[0, HEAD_DIM:HEAD_DIM + 1, :]
    for g in range(1, N_KV_HEADS):
        smallest = jnp.minimum(smallest, acc_ref[g, HEAD_DIM:HEAD_DIM + 1, :])

    @pl.when(jnp.min(smallest) < MIN_TRUSTED_SUM)
    def _():
        part = m_ref.shape[1]
        m_ref[...] = jnp.full(m_ref.shape, NEG_BIG, f32)

        def col_max(c, carry):
            bias = bias_ref[c]
            for g in range(N_KV_HEADS):
                s4 = _dot(k_ref[0, c], qz_ref[g])
                for r in range(KV_GROUP):
                    h = g * KV_GROUP + r
                    s = (s4[:, r * tq:(r + 1) * tq] + bias).reshape(ck // part, part, tq)
                    m_ref[h] = jnp.maximum(m_ref[h], jnp.max(s, axis=0))
            return carry

        lax.fori_loop(0, n_chunks, col_max, 0)
        for h in range(N_HEADS):
            top_ref[h] = jnp.max(m_ref[h], axis=0, keepdims=True)
        sweep()

    heads = []
    for g in range(N_KV_HEADS):
        o4 = acc_ref[g, 0:HEAD_DIM, :] / acc_ref[g, HEAD_DIM:HEAD_DIM + 1, :]
        heads += [o4[:, r * tq:(r + 1) * tq] for r in range(KV_GROUP)]
    o_ref[0] = jnp.concatenate(heads, axis=0).T.astype(o_ref.dtype)


def _prompt_attention(q, qi, wi, ki, k, vt):
    b, s, _ = q.shape
    top_k = min(TOPK_MAX, s // 4)
    tq, ck = Q_TILE, KEY_CHUNK
    nq, nc = s // tq, s // ck
    gq = KV_GROUP * tq
    kv_w = N_KV_HEADS * HEAD_DIM
    low = jnp.asarray(np.arange(ck)[:, None] > np.arange(ck)[None, :], bf16)
    per_query = lambda w: pl.BlockSpec((1, tq, w), lambda i, j: (i, j, 0))
    return pl.pallas_call(
        functools.partial(_prompt_attn_kernel, top_k=top_k),
        grid=(b, nq),
        in_specs=[
            per_query(q.shape[2]), per_query(qi.shape[2]), per_query(wi.shape[2]),
            pl.BlockSpec((1, nc, ck, IDX_DIM), lambda i, j: (i, 0, 0, 0)),
            pl.BlockSpec((1, nc, ck, kv_w), lambda i, j: (i, 0, 0, 0)),
            pl.BlockSpec((nc, N_KV_HEADS, V_ROWS, ck), lambda i, j: (i, 0, 0, 0)),
            _resident((ck, ck)),
        ],
        out_specs=per_query(q.shape[2]),
        out_shape=jax.ShapeDtypeStruct(q.shape, bf16),
        scratch_shapes=[pltpu.VMEM((nc, ck, tq), i32), pltpu.VMEM((nc, ck, tq), f32),
                        pltpu.VMEM((N_HEADS, SOFTMAX_PARTIALS, tq), f32), pltpu.VMEM((N_HEADS, 1, tq), f32),
                        pltpu.VMEM((SUBLANE, kv_w), f32),
                        pltpu.VMEM((N_KV_HEADS, V_ROWS, gq), f32),
                        pltpu.VMEM((N_KV_HEADS, kv_w, gq), bf16)],
        compiler_params=_params("parallel", "arbitrary"),
        name="prompt_attention",
    )(q, qi, wi, ki.reshape(b, nc, ck, IDX_DIM), k.reshape(b, nc, ck, kv_w), vt, low)


def _sample_select_kernel(pt_ref, qi_ref, wi_ref, kin_ref, tri_ref, lt_ref, *rest, top_k, n_pages):
    page_refs, (sel_ref, seln_ref, keys_ref) = rest[:PAGES_PER_STEP], rest[PAGES_PER_STEP:]
    j = pl.program_id(1)
    qi = qi_ref[0]
    w = wi_ref[0]

    for i in range(PAGES_PER_STEP):
        s = _dot(qi, page_refs[i][0].astype(bf16))
        score = jnp.sum(jnp.maximum(s, 0.0) * w, axis=0, keepdims=True)
        keys_ref[pl.ds(j * PAGES_PER_STEP + i, 1), :] = _score_key(score)

    @pl.when(j == pl.num_programs(1) - 1)
    def _():
        kn = kin_ref[0].astype(bf16).astype(f32)
        s_new = jnp.sum(qi.astype(f32) * kn, axis=1, keepdims=True)
        key_new = _score_key(jnp.sum(jnp.maximum(s_new, 0.0) * w, axis=0, keepdims=True))
        keys = keys_ref[...]

        def count_ge(cand):
            c = jnp.sum((keys >= cand).astype(f32), axis=1, keepdims=True)
            return jnp.sum(c, axis=0, keepdims=True) + (key_new >= cand).astype(f32)

        thr, _ = _kth_largest_key(count_ge, float(top_k), jnp.full((1, 1), n_pages * PAGE_SIZE + 1, f32))
        need = top_k - (count_ge(thr + 1))
        eq = keys == thr
        eqb = eq.astype(bf16)
        in_row = _dot(eqb, tri_ref[...])
        per_page = jnp.sum(eqb.astype(f32), axis=1, keepdims=True)
        before = _dot(lt_ref[...], per_page.astype(bf16) * jnp.ones((1, LANE), bf16))[:, 0:1]
        rank = in_row + before
        sel = (keys > thr) | (eq & (rank < need))
        sel_ref[0] = sel.astype(f32)
        n_eq = jnp.sum(per_page, axis=0, keepdims=True)
        sel_new = (key_new > thr) | ((key_new == thr) & (n_eq < need))
        seln_ref[0] = jnp.broadcast_to(sel_new.astype(f32), (1, LANE))


def _sample_attend_kernel(pt_ref, q_ref, sel_ref, seln_ref, kn_ref, vn_ref, gm_ref, *rest, n_pages):
    k_refs = rest[:PAGES_PER_STEP]
    v_refs = rest[PAGES_PER_STEP:2 * PAGES_PER_STEP]
    o_ref, m_ref, l_ref, acc_ref = rest[2 * PAGES_PER_STEP:]
    j = pl.program_id(1)
    q = q_ref[0]

    @pl.when(j == 0)
    def _():
        m_ref[...] = jnp.full(m_ref.shape, NEG_BIG, f32)
        l_ref[...] = jnp.zeros(l_ref.shape, f32)
        acc_ref[...] = jnp.zeros(acc_ref.shape, f32)

    def update(s, vals):
        m = m_ref[...]
        m_new = jnp.maximum(m, jnp.max(s, axis=1, keepdims=True))
        alpha = jnp.exp(m - m_new)
        p = jnp.exp(s - m_new)
        l_ref[...] = alpha * l_ref[...] + jnp.sum(p, axis=1, keepdims=True)
        acc_ref[...] = alpha * acc_ref[...] + vals(p)
        m_ref[...] = m_new

    for i in range(PAGES_PER_STEP):
        sel = sel_ref[0, pl.ds(j * PAGES_PER_STEP + i, 1), :] > 0.5
        s = jnp.where(sel, _dot(q, k_refs[i][0].astype(bf16)), NEG_BIG)
        update(s, lambda p, i=i: _dot_nt(p.astype(bf16), v_refs[i][0].astype(bf16)))

    @pl.when(j == pl.num_programs(1) - 1)
    def _():
        kn = kn_ref[0].astype(bf16).astype(f32)
        vn = vn_ref[0].astype(bf16).astype(f32)
        s = jnp.sum(q.astype(f32) * kn, axis=1, keepdims=True)
        s = jnp.where(seln_ref[0][:, 0:1] > 0.5, s, NEG_BIG)
        update(s, lambda p: p.astype(bf16).astype(f32) * vn)
        full = acc_ref[...] / l_ref[...] * gm_ref[...]
        out = full[:, 0:HEAD_DIM]
        for g in range(1, N_KV_HEADS):
            out = out + full[:, g * HEAD_DIM:(g + 1) * HEAD_DIM]
        o_ref[0] = out.astype(o_ref.dtype)


def _sample_attention(q, k_new, v_new, qi, ki_new, wi, cache_k, cache_v, cache_kidx, page_table):
    db = q.shape[0]
    n_pages = page_table.shape[1]
    past = n_pages * PAGE_SIZE
    top_k = min(TOPK_MAX, (past + 1) // 4)
    pps = PAGES_PER_STEP
    steps = n_pages // pps
    kv_w = N_KV_HEADS * HEAD_DIM
    n_pool = cache_k.shape[0]
    ck2 = cache_k.transpose(0, 2, 3, 1).reshape(n_pool, kv_w, PAGE_SIZE)
    cv2 = cache_v.transpose(0, 2, 3, 1).reshape(n_pool, kv_w, PAGE_SIZE)
    cki = cache_kidx.transpose(0, 2, 1)

    def page_spec(width, i):
        return pl.BlockSpec((1, width, PAGE_SIZE), lambda b, j, pt, i=i: (pt[b, j * pps + i], 0, 0))

    per_seq = lambda *shape: pl.BlockSpec((1,) + shape, lambda b, j, pt: (b,) + (0,) * len(shape))
    const = lambda *shape: pl.BlockSpec(shape, lambda b, j, pt: (0,) * len(shape))
    tri = jnp.asarray(np.arange(PAGE_SIZE)[:, None] < np.arange(PAGE_SIZE)[None, :], bf16)
    lower = jnp.asarray(np.arange(n_pages)[:, None] > np.arange(n_pages)[None, :], bf16)

    sel, sel_new = pl.pallas_call(
        functools.partial(_sample_select_kernel, top_k=top_k, n_pages=n_pages),
        grid_spec=pltpu.PrefetchScalarGridSpec(
            num_scalar_prefetch=1,
            grid=(db, steps),
            in_specs=[per_seq(N_IDX_HEADS, IDX_DIM), per_seq(N_IDX_HEADS, 1), per_seq(1, IDX_DIM),
                      const(PAGE_SIZE, PAGE_SIZE), const(n_pages, n_pages)]
                     + [page_spec(IDX_DIM, i) for i in range(pps)],
            out_specs=[per_seq(n_pages, PAGE_SIZE), per_seq(1, LANE)],
            scratch_shapes=[pltpu.VMEM((n_pages, PAGE_SIZE), i32)],
        ),
        out_shape=[jax.ShapeDtypeStruct((db, n_pages, PAGE_SIZE), f32), jax.ShapeDtypeStruct((db, 1, LANE), f32)],
        compiler_params=_params("parallel", "arbitrary"),
        name="sample_select",
    )(page_table, qi.reshape(db, N_IDX_HEADS, IDX_DIM), (wi * (IDX_DIM ** -0.5)).reshape(db, N_IDX_HEADS, 1),
      ki_new.reshape(db, 1, IDX_DIM), tri, lower, *([cki] * pps))

    group_mask = np.repeat(np.arange(N_HEADS)[:, None] // KV_GROUP == np.arange(N_KV_HEADS)[None, :], HEAD_DIM, axis=1)
    q_bd = jnp.where(group_mask[None], jnp.tile(q.reshape(db, N_HEADS, HEAD_DIM), (1, 1, N_KV_HEADS)), 0).astype(bf16)
    out = pl.pallas_call(
        functools.partial(_sample_attend_kernel, n_pages=n_pages),
        grid_spec=pltpu.PrefetchScalarGridSpec(
            num_scalar_prefetch=1,
            grid=(db, steps),
            in_specs=[per_seq(N_HEADS, kv_w), per_seq(n_pages, PAGE_SIZE), per_seq(1, LANE),
                      per_seq(1, kv_w), per_seq(1, kv_w), const(N_HEADS, kv_w)]
                     + [page_spec(kv_w, i) for i in range(pps)] * 2,
            out_specs=per_seq(N_HEADS, HEAD_DIM),
            scratch_shapes=[pltpu.VMEM((N_HEADS, 1), f32), pltpu.VMEM((N_HEADS, 1), f32), pltpu.VMEM((N_HEADS, kv_w), f32)],
        ),
        out_shape=jax.ShapeDtypeStruct((db, N_HEADS, HEAD_DIM), bf16),
        compiler_params=_params("parallel", "arbitrary"),
        name="sample_attend",
    )(page_table, q_bd, sel, sel_new, k_new.reshape(db, 1, kv_w), v_new.reshape(db, 1, kv_w),
      jnp.asarray(group_mask, f32), *([ck2] * pps), *([cv2] * pps))
    return out.reshape(db, N_HEADS * HEAD_DIM)


def _rglru_coeffs(xc, wr_ref, br_ref, wi_ref, bi_ref, lam_ref):
    xb = xc.astype(bf16)
    r = jax.nn.sigmoid(_dot(xb, wr_ref[...]) + br_ref[...])
    i = jax.nn.sigmoid(_dot(xb, wi_ref[...]) + bi_ref[...])
    z = -lam_ref[...]
    softplus = jnp.maximum(z, 0.0) + jnp.log1p(jnp.exp(-jnp.abs(z)))
    log_a = -LRU_C * r * softplus
    a = jnp.exp(log_a)
    th = jnp.tanh(log_a)
    b = jnp.sqrt(-2.0 * th / (1.0 - th)) * (i * xc)
    return a, b


def _rglru_seq_kernel(xr_ref, yr_ref, cw_ref, cb_ref, wr_ref, br_ref, wi_ref, bi_ref, lam_ref,
                      o_ref, hl_ref, xprev_ref, h_ref, a_ref, b_ref):
    t = pl.program_id(1)
    tc, d = xr_ref.shape[1], xr_ref.shape[2]

    @pl.when(t == 0)
    def _():
        xprev_ref[...] = jnp.zeros(xprev_ref.shape, f32)
        h_ref[...] = jnp.zeros(h_ref.shape, f32)

    xr = xr_ref[0]
    xcat = jnp.concatenate([xprev_ref[...], xr], axis=0)
    xc = cb_ref[...] + cw_ref[0:1, :] * xcat[SUBLANE - 3:SUBLANE - 3 + tc]
    for j in range(1, CONV_W):
        xc = xc + cw_ref[j:j + 1, :] * xcat[SUBLANE - 3 + j:SUBLANE - 3 + j + tc]
    xprev_ref[...] = xr[tc - SUBLANE:, :]
    a, b = _rglru_coeffs(xc, wr_ref, br_ref, wi_ref, bi_ref, lam_ref)
    a_ref[...] = a
    b_ref[...] = b
    row = lax.broadcasted_iota(i32, (SUBLANE, d), 0)

    def slab(s, h_prev):
        r0 = pl.multiple_of(s * SUBLANE, SUBLANE)
        a8 = a_ref[pl.ds(r0, SUBLANE), :]
        b8 = b_ref[pl.ds(r0, SUBLANE), :]
        for sh in (1, 2, 4):
            keep = row >= sh
            b8 = jnp.where(keep, a8 * pltpu.roll(b8, sh, 0) + b8, b8)
            a8 = jnp.where(keep, a8 * pltpu.roll(a8, sh, 0), a8)
        h8 = a8 * h_prev + b8
        o_ref[0, pl.ds(r0, SUBLANE), :] = (h8 * _gelu(yr_ref[0, pl.ds(r0, SUBLANE), :])).astype(o_ref.dtype)
        return h8[SUBLANE - 1:SUBLANE, :]

    h_last = lax.fori_loop(0, tc // SUBLANE, slab, h_ref[...])
    h_ref[...] = h_last
    hl_ref[0] = h_last


def _block_diag(w):
    n, a, b = w.shape
    eye = jnp.eye(n, dtype=w.dtype)
    return (eye[:, None, :, None] * w[:, :, None, :]).reshape(n * a, n * b).astype(bf16)


def _rglru_prompt(xr, yr, conv_w, conv_b, w_rgate, b_rgate, w_igate, b_igate, lru_lambda):
    b, t, d = xr.shape
    tc = RNN_CHUNK
    vec = lambda v: v.reshape(1, d)
    seq = pl.BlockSpec((1, tc, d), lambda i, j: (i, j, 0))
    out, h_last = pl.pallas_call(
        _rglru_seq_kernel,
        grid=(b, t // tc),
        in_specs=[seq, seq, _resident((CONV_W, d)), _resident((1, d)), _resident((d, d)), _resident((1, d)),
                  _resident((d, d)), _resident((1, d)), _resident((1, d))],
        out_specs=[seq, pl.BlockSpec((1, 1, d), lambda i, j: (i, 0, 0))],
        out_shape=[jax.ShapeDtypeStruct((b, t, d), bf16), jax.ShapeDtypeStruct((b, 1, d), f32)],
        scratch_shapes=[pltpu.VMEM((SUBLANE, d), f32), pltpu.VMEM((1, d), f32),
                        pltpu.VMEM((tc, d), f32), pltpu.VMEM((tc, d), f32)],
        compiler_params=_params("parallel", "arbitrary"),
        name="rglru_prompt",
    )(xr, yr, conv_w, vec(conv_b), _block_diag(w_rgate), vec(b_rgate), _block_diag(w_igate), vec(b_igate),
      vec(lru_lambda))
    return out, h_last.reshape(b, d)


def _rglru_step_kernel(xr_ref, yr_ref, c0_ref, c1_ref, c2_ref, h0_ref, cw_ref, cb_ref, wr_ref, br_ref, wi_ref,
                       bi_ref, lam_ref, o_ref, h_ref):
    xc = cb_ref[...] + cw_ref[0:1, :] * c0_ref[...]
    xc = xc + cw_ref[1:2, :] * c1_ref[...]
    xc = xc + cw_ref[2:3, :] * c2_ref[...]
    xc = xc + cw_ref[3:4, :] * xr_ref[...]
    a, b = _rglru_coeffs(xc, wr_ref, br_ref, wi_ref, bi_ref, lam_ref)
    h = a * h0_ref[...] + b
    h_ref[...] = h
    o_ref[...] = (h * _gelu(yr_ref[...])).astype(o_ref.dtype)


def _rglru_sample(xr, yr, state_conv, h0, conv_w, conv_b, w_rgate, b_rgate, w_igate, b_igate, lru_lambda):
    db, d = xr.shape
    vec = lambda v: v.reshape(1, d)
    return pl.pallas_call(
        _rglru_step_kernel,
        out_shape=[jax.ShapeDtypeStruct((db, d), bf16), jax.ShapeDtypeStruct((db, d), f32)],
        compiler_params=pltpu.CompilerParams(vmem_limit_bytes=VMEM_LIMIT_BYTES),
        name="rglru_sample",
    )(xr, yr, state_conv[:, 0], state_conv[:, 1], state_conv[:, 2], h0, conv_w, vec(conv_b),
      _block_diag(w_rgate), vec(b_rgate), _block_diag(w_igate), vec(b_igate), vec(lru_lambda))


def _top_rows(x, k, payload=None):
    n = x.shape[0]
    row = lax.broadcasted_iota(i32, x.shape, 0)
    vals, picked = [], []
    for _ in range(k):
        m = jnp.max(x, axis=0, keepdims=True)
        r = jnp.min(jnp.where(x == m, row, n), axis=0, keepdims=True)
        hit = row == r
        vals.append(m)
        picked.append(r if payload is None else jnp.sum(jnp.where(hit, payload, 0), axis=0, keepdims=True))
        x = jnp.where(hit, -jnp.inf, x)
    return jnp.concatenate(vals, axis=0), jnp.concatenate(picked, axis=0)


_PAIR_COUNTS = [PEER_TOPK // (r1 + 1) for r1 in range(PEER_TOPK)]


def _back_kernel(x_ref, ao_ref, ro_ref, ga_ref, gr_ref, woa_ref, wor_ref, wout_ref, g2_ref, wqt_ref, k1_ref, k2_ref,
                 x1_ref, hn_ref, eidx_ref, gate_ref):
    mixed = ga_ref[...] * _dot(ao_ref[...], woa_ref[...]) + gr_ref[...] * _dot(ro_ref[...], wor_ref[...])
    x1 = x_ref[...] + _dot(mixed.astype(bf16), wout_ref[...])
    x1_ref[...] = x1
    hn = (x1 * lax.rsqrt(jnp.mean(x1 * x1, axis=-1, keepdims=True) + NORM_EPS) * g2_ref[...]).astype(bf16)
    hn_ref[...] = hn.astype(f32)
    qt = _dot_nt(wqt_ref[...], hn).astype(bf16)
    half = PEER_KEY_DIM // 2
    for h in range(PEER_HEADS):
        base = h * PEER_KEY_DIM
        v1, i1 = _top_rows(_dot(k1_ref[...], qt[base:base + half, :]), PEER_TOPK)
        v2, i2 = _top_rows(_dot(k2_ref[...], qt[base + half:base + PEER_KEY_DIM, :]), PEER_TOPK)
        cand, cid = [], []
        for r1, n2 in enumerate(_PAIR_COUNTS):
            cand.append(v1[r1:r1 + 1, :] + v2[0:n2, :])
            cid.append(i1[r1:r1 + 1, :] * N_KEYS + i2[0:n2, :])
        pad = -sum(_PAIR_COUNTS) % SUBLANE
        tm = v1.shape[1]
        cand.append(jnp.full((pad, tm), -jnp.inf, f32))
        cid.append(jnp.zeros((pad, tm), i32))
        sv, eidx = _top_rows(jnp.concatenate(cand, axis=0), PEER_TOPK, jnp.concatenate(cid, axis=0))
        p = jnp.exp(sv - sv[0:1, :])
        eidx_ref[h * PEER_TOPK:(h + 1) * PEER_TOPK, :] = eidx
        gate_ref[h * PEER_TOPK:(h + 1) * PEER_TOPK, :] = p / jnp.sum(p, axis=0, keepdims=True)


def _back(x, attn_o, rnn_o, g_a, g_r, w_o_attn, w_o_rnn, w_out, norm2_g, w_peer_q, peer_k1, peer_k2):
    n, d = x.shape
    tm = TOKEN_TILE
    row = lambda w: pl.BlockSpec((tm, w), lambda i: (i, 0))
    col = pl.BlockSpec((PEER_SLOTS, tm), lambda i: (0, i))
    ws = [w_o_attn.astype(bf16), w_o_rnn.astype(bf16), w_out.astype(bf16), norm2_g.reshape(1, d),
          w_peer_q.T.astype(bf16), peer_k1.astype(bf16), peer_k2.astype(bf16)]
    return pl.pallas_call(
        _back_kernel,
        grid=(n // tm,),
        in_specs=[row(d), row(attn_o.shape[1]), row(rnn_o.shape[1]), row(d), row(d)] + [_resident(w.shape) for w in ws],
        out_specs=[row(d), row(d), col, col],
        out_shape=[jax.ShapeDtypeStruct((n, d), f32), jax.ShapeDtypeStruct((n, d), f32),
                   jax.ShapeDtypeStruct((PEER_SLOTS, n), i32), jax.ShapeDtypeStruct((PEER_SLOTS, n), f32)],
        compiler_params=_params("parallel"),
        name="back",
    )(x, attn_o, rnn_o, g_a, g_r, *ws)


def _pack_rows(table, reverse_pieces=False):
    e, d = table.shape
    bits = lax.bitcast_convert_type(table.astype(bf16), jnp.uint16).astype(jnp.uint32).reshape(e // 2, 2, d // LANE, LANE)
    if reverse_pieces:
        bits = bits[:, :, ::-1, :]
    return lax.bitcast_convert_type(bits[:, 0] | (bits[:, 1] << 16), i32).reshape(e // 2 * (d // LANE), LANE)


def _gather_tiles(table_ref, row_ref, t):
    starts = row_ref.at[t]
    tiles = [table_ref[pl.ds(pl.multiple_of(starts[s], SUBLANE), SUBLANE), :] for s in range(PEER_SLOTS)]
    return pltpu.bitcast(jnp.concatenate(tiles, axis=0), bf16)


def _token_tiles(x8):
    tiles = x8.reshape(SUBLANE, x8.shape[1] // LANE, LANE)
    return [tiles[j] for j in range(SUBLANE)]


def _peer_act_kernel(start_ref, hn_ref, gate_ref, u_ref, coef_ref):
    row = lax.broadcasted_iota(i32, (SUBLANE, LANE), 0)
    n_blk = PEER_X // LANE

    def token_group(gi, carry):
        t0 = pl.multiple_of(gi * SUBLANE, SUBLANE)
        acts = [jnp.zeros((SUBLANE, LANE), f32) for _ in range(n_blk)]
        h_tiles = _token_tiles(hn_ref[pl.ds(t0, SUBLANE), :])
        for j in range(SUBLANE):
            rows = _gather_tiles(u_ref, start_ref, t0 + j)
            r = _dot_nt(h_tiles[j].astype(bf16), rows)
            for blk in range(n_blk):
                z = pltpu.roll(r[:, blk * LANE:(blk + 1) * LANE], 0, 1, stride=2, stride_axis=0)
                for sh in (4, 2, 1):
                    z = z + pltpu.roll(z, sh, 0)
                acts[blk] = jnp.where(row == j, z, acts[blk])
        for blk in range(n_blk):
            cols = slice(blk * LANE, (blk + 1) * LANE)
            coef = gate_ref[pl.ds(t0, SUBLANE), cols] * _gelu(acts[blk])
            coef_ref[pl.ds(t0, SUBLANE), cols] = pltpu.roll(coef, LANE - (PACKED_ROWS - 2), 1)
        return carry

    lax.fori_loop(0, hn_ref.shape[0] // SUBLANE, token_group, 0)


def _peer_out_kernel(start_ref, coef_ref, x1_ref, v_ref, y_ref):
    n_blk = PEER_X // LANE

    def token_group(gi, carry):
        t0 = pl.multiple_of(gi * SUBLANE, SUBLANE)
        coef = coef_ref[pl.ds(t0, SUBLANE), :]
        outs = []
        for j in range(SUBLANE):
            rows = _gather_tiles(v_ref, start_ref, t0 + j)
            pick = [pltpu.roll(jnp.broadcast_to(coef[j:j + 1, blk * LANE:(blk + 1) * LANE], (SUBLANE, LANE)),
                               0, 1, stride=2, stride_axis=0) for blk in range(n_blk)]
            outs.append(_dot(jnp.concatenate(pick, axis=1).astype(bf16), rows))
        x1 = x1_ref[pl.ds(t0, SUBLANE), :]
        y_ref[pl.ds(t0, SUBLANE), :] = x1 + jnp.stack(outs, axis=0).reshape(x1.shape)
        return carry

    lax.fori_loop(0, x1_ref.shape[0] // SUBLANE, token_group, 0)


def _peer(hn, x1, eidx_t, gate_t, peer_u, peer_v):
    n, d = x1.shape
    tt = PEER_TOKENS
    rows = d // LANE
    eidx = eidx_t.T
    start = (eidx >> 1) * SUBLANE
    col = jnp.tile(jnp.arange(PACKED_ROWS, dtype=i32), PEER_SLOTS)[None, :]
    gate = jnp.where(col == PACKED_ROWS - 2 + jnp.repeat(eidx & 1, PACKED_ROWS, axis=1),
                     jnp.repeat(gate_t.T, PACKED_ROWS, axis=1), 0.0)
    smem = pl.BlockSpec((tt, PEER_SLOTS), lambda i: (i, 0), memory_space=pltpu.SMEM)
    tokens = pl.BlockSpec((tt, d), lambda i: (i, 0))
    wide = pl.BlockSpec((tt, PEER_X), lambda i: (i, 0))
    table = _resident((peer_u.shape[0] // 2 * rows, LANE))
    coef = pl.pallas_call(
        _peer_act_kernel,
        grid=(n // tt,),
        in_specs=[smem, tokens, wide, table],
        out_specs=wide,
        out_shape=jax.ShapeDtypeStruct((n, PEER_X), f32),
        compiler_params=_params("parallel"),
        name="peer_act",
    )(start, hn, gate, _pack_rows(peer_u, reverse_pieces=True))
    y = pl.pallas_call(
        _peer_out_kernel,
        grid=(n // tt,),
        in_specs=[smem, wide, tokens, table],
        out_specs=tokens,
        out_shape=jax.ShapeDtypeStruct((n, d), f32),
        compiler_params=_params("parallel"),
        name="peer_out",
    )(start, coef, x1, _pack_rows(peer_v))
    return y


def kernel(x_prompt, x_sample, cache_k, cache_v, cache_kidx, page_table, state_conv, state_rglru, norm1_g, w_in, b_gates, q_norm_g, k_norm_g, conv_w, conv_b, w_rgate, b_rgate, w_igate, b_igate, lru_lambda, w_o_attn, w_o_rnn, w_out, norm2_g, w_peer_q, peer_k1, peer_k2, peer_u, peer_v):
    bp, sp, d = x_prompt.shape
    db, ts, _ = x_sample.shape
    assert ts == 1, "the sample path handles one new token per sequence"
    d_rnn = conv_w.shape[1]
    past = page_table.shape[1] * PAGE_SIZE
    n_p, n_s = bp * sp, db * ts
    assert n_p % TOKEN_TILE == 0
    n_sp = -(-n_s // KEY_CHUNK) * KEY_CHUNK
    rnn_w = (conv_w, conv_b, w_rgate, b_rgate, w_igate, b_igate, lru_lambda)
    front_w = (norm1_g, w_in, b_gates, q_norm_g, k_norm_g, d_rnn)
    back_w = (w_o_attn, w_o_rnn, w_out, norm2_g, w_peer_q, peer_k1, peer_k2)
    pad = lambda a: jnp.concatenate([a, jnp.zeros((n_sp - n_s,) + a.shape[1:], a.dtype)])
    heads = lambda a, b_, t_: a.reshape(b_, t_, N_KV_HEADS, HEAD_DIM)
    flat = lambda a: a.reshape(a.shape[0], N_KV_HEADS * HEAD_DIM)

    xp = x_prompt.reshape(n_p, d)
    q, k_p, v_p, qi, ki_p, wi, xr_p, yr, g_a, g_r, kb, kib, vt = _front(
        xp, jnp.tile(jnp.arange(sp, dtype=f32), bp), sp, *front_w)
    pr = lambda a: a.reshape(bp, sp, a.shape[1])
    attn_p = _prompt_attention(pr(q), pr(qi), pr(wi), pr(kib), pr(kb), vt)
    rnn_p, h_p = _rglru_prompt(pr(xr_p), pr(yr), *rnn_w)
    x1, hn, eidx_t, gate_t = _back(xp, attn_p.reshape(n_p, -1), rnn_p.reshape(n_p, d_rnn), g_a, g_r, *back_w)
    y_p = _peer(hn, x1, eidx_t, gate_t, peer_u, peer_v)

    xs = pad(x_sample.reshape(n_s, d))
    pos_s = pad(past + jnp.tile(jnp.arange(ts, dtype=f32), db))
    q, k_s, v_s, qi, ki_s, wi, xr_s, yr, g_a, g_r, _, _, _ = _front(xs, pos_s, n_sp, *front_w)
    sm = lambda a: a[:n_s]
    k_s, v_s = (sm(a[0].transpose(2, 0, 1)) for a in (k_s, v_s))
    ki_s = sm(ki_s[0].T)
    attn_s = _sample_attention(sm(q), flat(k_s), flat(v_s), sm(qi), ki_s, sm(wi[:, :N_IDX_HEADS]),
                               cache_k, cache_v, cache_kidx, page_table)
    rnn_s, h_s = _rglru_sample(sm(xr_s), sm(yr), state_conv, state_rglru, *rnn_w)
    x1, hn, eidx_t, gate_t = _back(xs, pad(attn_s), pad(rnn_s), g_a, g_r, *back_w)
    y_s = _peer(hn, x1, eidx_t, gate_t, peer_u, peer_v)

    conv_p = pr(xr_p)[:, sp - (CONV_W - 1):, :]
    conv_s = jnp.concatenate([state_conv[:, ts:, :], sm(xr_s).reshape(db, ts, d_rnn)], axis=1)
    return (y_p.reshape(bp, sp, d), sm(y_s).reshape(db, ts, d),
            k_p.transpose(0, 3, 1, 2), v_p.transpose(0, 3, 1, 2), ki_p.transpose(0, 2, 1),
            conv_p, h_p,
            heads(k_s, db, ts), heads(v_s, db, ts), ki_s.reshape(db, ts, IDX_DIM),
            conv_s, h_s)
```
